```python
import math
import jax, jax.numpy as jnp
from jax import lax
import numpy as np

D_MODEL = 1024
BATCH = 4
SEQ = 4096
DEPTH = 1

CHUNK = 64
Q_BLOCK = 128
D_MIX = D_MODEL
D_ATTN = D_MIX // 2
D_CONV = D_MIX - D_ATTN
N_HEADS = 4
HEAD_DIM = D_ATTN // N_HEADS
SUB_DIM = HEAD_DIM // 2
ROPE_DIM = SUB_DIM // 4
ROPE_THETA = 500000.0
CONV_WIDTH = 31
N_EXPERTS = 64
TOP_K = 8
N_GROUPS = 8
TOPK_GROUPS = 4
D_EXPERT = 256
D_SHARED = 256
ROUTED_SCALE = 2.5
EXPERT_BLOCK = 128
LN_EPS = 1e-5
ALPHA = (2.0 * DEPTH) ** 0.25
BETA = (8.0 * DEPTH) ** -0.25
D_IN = 3 * D_ATTN + 2 * D_CONV

kernel_name = "hybrid_diffattn_conformerconv_moe_deepnorm"


def layer_norm(x, g, b):
    xf = x.astype(jnp.float32)
    mu = jnp.mean(xf, axis=-1, keepdims=True)
    var = jnp.mean(jnp.square(xf - mu), axis=-1, keepdims=True)
    y = (xf - mu) * lax.rsqrt(var + LN_EPS) * g.astype(jnp.float32) + b.astype(jnp.float32)
    return y.astype(x.dtype)


def rms_norm(x, g):
    xf = x.astype(jnp.float32)
    y = xf * lax.rsqrt(jnp.mean(jnp.square(xf), axis=-1, keepdims=True) + LN_EPS)
    return (y * g.astype(jnp.float32)).astype(x.dtype)


def partial_rope(t, positions):
    inv_freq = ROPE_THETA ** (-jnp.arange(0, ROPE_DIM, 2, dtype=jnp.float32) / ROPE_DIM)
    ang = positions.astype(jnp.float32)[:, :, None] * inv_freq
    cos = jnp.cos(ang)[:, :, None, None, :]
    sin = jnp.sin(ang)[:, :, None, None, :]
    r1 = t[..., : ROPE_DIM // 2].astype(jnp.float32)
    r2 = t[..., ROPE_DIM // 2 : ROPE_DIM].astype(jnp.float32)
    rot = jnp.concatenate([r1 * cos - r2 * sin, r2 * cos + r1 * sin], axis=-1).astype(t.dtype)
    return jnp.concatenate([rot, t[..., ROPE_DIM:]], axis=-1)


def diff_attention(q, k, v, lam, sub_g, lambda_init):
    B, S = q.shape[0], q.shape[1]
    nb = S // Q_BLOCK
    kT = k.transpose(0, 2, 3, 1, 4)
    vT = v.transpose(0, 2, 1, 3)
    qb = (q * (SUB_DIM ** -0.5)).reshape(B, nb, Q_BLOCK, N_HEADS, 2, SUB_DIM)
    qb = qb.transpose(1, 0, 3, 4, 2, 5)
    key_chunk = jnp.arange(S) // CHUNK

    def block(args):
        q_blk, i = args
        s = jnp.einsum('bhcqd,bhckd->bhcqk', q_blk, kT).astype(jnp.float32)
        q_chunk = (i * Q_BLOCK + jnp.arange(Q_BLOCK)) // CHUNK
        mask = key_chunk[None, :] <= q_chunk[:, None]
        p = jax.nn.softmax(jnp.where(mask, s, -jnp.inf), axis=-1)
        a = p[:, :, 0] - lam * p[:, :, 1]
        return jnp.einsum('bhqk,bhkv->bhqv', a.astype(vT.dtype), vT)

    o = lax.map(block, (qb, jnp.arange(nb)))
    o = rms_norm(o, sub_g) * (1.0 - lambda_init)
    return o.transpose(1, 0, 3, 2, 4).reshape(B, S, D_ATTN)


def conformer_conv(u, conv_w, conv_b, ln_g, ln_b):
    a, gate = jnp.split(u, 2, axis=-1)
    h = a * jax.nn.sigmoid(gate)
    h = lax.conv_general_dilated(h, conv_w[:, None, :], window_strides=(1,),
                                 padding=[(CONV_WIDTH - 1, 0)],
                                 dimension_numbers=('NWC', 'WIO', 'NWC'),
                                 feature_group_count=D_CONV) + conv_b
    h = layer_norm(h, ln_g, ln_b)
    return jax.nn.silu(h)


def moe_ffn(x, w_router, router_bias, w_gate, w_up, w_down, ws_gate, ws_up, ws_down):
    B, S, D = x.shape
    N = B * S
    xf = x.reshape(N, D)
    scores = jax.nn.sigmoid((xf @ w_router).astype(jnp.float32))
    biased = scores + router_bias.astype(jnp.float32)
    grp = biased.reshape(N, N_GROUPS, N_EXPERTS // N_GROUPS)
    grp_score = lax.top_k(grp, 2)[0].sum(-1)
    _, top_grp = lax.top_k(grp_score, TOPK_GROUPS)
    grp_mask = jax.nn.one_hot(top_grp, N_GROUPS, dtype=jnp.float32).sum(1) > 0
    exp_mask = jnp.repeat(grp_mask, N_EXPERTS // N_GROUPS, axis=1)
    _, top_idx = lax.top_k(jnp.where(exp_mask, biased, -jnp.inf), TOP_K)
    top_s = jnp.take_along_axis(scores, top_idx, axis=1)
    gates = (top_s / jnp.sum(top_s, axis=-1, keepdims=True)) * ROUTED_SCALE

    NK = N * TOP_K
    flat_e = top_idx.reshape(NK)
    flat_tok = jnp.repeat(jnp.arange(N, dtype=jnp.int32), TOP_K)
    flat_gate = gates.reshape(NK)
    order = jnp.argsort(flat_e)
    se, stok, sgate = flat_e[order], flat_tok[order], flat_gate[order]
    counts = jnp.bincount(flat_e, length=N_EXPERTS)
    starts = jnp.cumsum(counts) - counts
    padded = (counts + EXPERT_BLOCK - 1) // EXPERT_BLOCK * EXPERT_BLOCK
    pad_end = jnp.cumsum(padded)
    pad_start = pad_end - padded
    dest = pad_start[se] + (jnp.arange(NK) - starts[se])
    n_blocks = (NK + N_EXPERTS * (EXPERT_BLOCK - 1) + EXPERT_BLOCK - 1) // EXPERT_BLOCK
    P = n_blocks * EXPERT_BLOCK
    row_tok = jnp.zeros((P,), jnp.int32).at[dest].set(stok)
    row_gate = jnp.zeros((P,), jnp.float32).at[dest].set(sgate)
    blk_exp = jnp.minimum(jnp.searchsorted(pad_end, jnp.arange(n_blocks) * EXPERT_BLOCK, side='right'),
                          N_EXPERTS - 1)

    def expert_block(args):
        tok, g, e = args
        h = xf[tok]
        a = jax.nn.silu(h @ w_gate[e]) * (h @ w_up[e])
        return (a @ w_down[e]) * g[:, None].astype(xf.dtype)

    y = lax.map(expert_block, (row_tok.reshape(n_blocks, EXPERT_BLOCK),
                               row_gate.reshape(n_blocks, EXPERT_BLOCK), blk_exp))
    routed = jnp.zeros((N, D), xf.dtype).at[row_tok].add(y.reshape(P, D))
    shared = (jax.nn.silu(xf @ ws_gate) * (xf @ ws_up)) @ ws_down
    return (routed + shared).reshape(B, S, D)


def setup_inputs(seed: int = 0) -> dict:
    key = jax.random.key(seed)
    ks = jax.random.split(key, 26)
    f32 = jnp.float32
    nrm = lambda k, shape, s: jax.random.normal(k, shape, f32) * s
    L = DEPTH
    offset = jax.random.randint(ks[1], (BATCH, 1), 0, 65536, dtype=jnp.int32)
    return {
        "x": jax.random.normal(ks[0], (BATCH, SEQ, D_MODEL), f32),
        "positions": offset + jnp.arange(SEQ, dtype=jnp.int32)[None, :],
        "w_in": nrm(ks[2], (L, D_MODEL, D_IN), D_MODEL ** -0.5),
        "lambda_q1": nrm(ks[3], (L, SUB_DIM), 0.1),
        "lambda_k1": nrm(ks[4], (L, SUB_DIM), 0.1),
        "lambda_q2": nrm(ks[5], (L, SUB_DIM), 0.1),
        "lambda_k2": nrm(ks[6], (L, SUB_DIM), 0.1),
        "subln_g": 1.0 + nrm(ks[7], (L, HEAD_DIM), 0.02),
        "conv_w": nrm(ks[8], (L, CONV_WIDTH, D_CONV), CONV_WIDTH ** -0.5),
        "conv_b": nrm(ks[9], (L, D_CONV), 0.02),
        "conv_ln_g": 1.0 + nrm(ks[10], (L, D_CONV), 0.02),
        "conv_ln_b": nrm(ks[11], (L, D_CONV), 0.02),
        "w_out": nrm(ks[12], (L, D_MIX, D_MODEL), BETA * D_MIX ** -0.5),
        "ln1_g": 1.0 + nrm(ks[13], (L, D_MODEL), 0.02),
        "ln1_b": nrm(ks[14], (L, D_MODEL), 0.02),
        "w_router": nrm(ks[15], (L, D_MODEL, N_EXPERTS), D_MODEL ** -0.5),
        "router_bias": nrm(ks[16], (L, N_EXPERTS), 0.01),
        "w_exp_gate": nrm(ks[17], (L, N_EXPERTS, D_MODEL, D_EXPERT), D_MODEL ** -0.5),
        "w_exp_up": nrm(ks[18], (L, N_EXPERTS, D_MODEL, D_EXPERT), D_MODEL ** -0.5),
        "w_exp_down": nrm(ks[19], (L, N_EXPERTS, D_EXPERT, D_MODEL), BETA * D_EXPERT ** -0.5),
        "w_sh_gate": nrm(ks[20], (L, D_MODEL, D_SHARED), D_MODEL ** -0.5),
        "w_sh_up": nrm(ks[21], (L, D_MODEL, D_SHARED), D_MODEL ** -0.5),
        "w_sh_down": nrm(ks[22], (L, D_SHARED, D_MODEL), BETA * D_SHARED ** -0.5),
        "ln2_g": 1.0 + nrm(ks[23], (L, D_MODEL), 0.02),
        "ln2_b": nrm(ks[24], (L, D_MODEL), 0.02),
    }


def reference(x, positions, w_in, lambda_q1, lambda_k1, lambda_q2, lambda_k2, subln_g,
              conv_w, conv_b, conv_ln_g, conv_ln_b, w_out, ln1_g, ln1_b,
              w_router, router_bias, w_exp_gate, w_exp_up, w_exp_down,
              w_sh_gate, w_sh_up, w_sh_down, ln2_g, ln2_b):
    B, S = x.shape[0], x.shape[1]
    for l in range(DEPTH):
        lambda_init = 0.8 - 0.6 * math.exp(-0.3 * l)
        proj = x @ w_in[l]
        q = proj[..., :D_ATTN].reshape(B, S, N_HEADS, 2, SUB_DIM)
        k = proj[..., D_ATTN:2 * D_ATTN].reshape(B, S, N_HEADS, 2, SUB_DIM)
        v = proj[..., 2 * D_ATTN:3 * D_ATTN].reshape(B, S, N_HEADS, HEAD_DIM)
        u = proj[..., 3 * D_ATTN:]
        q = partial_rope(q, positions)
        k = partial_rope(k, positions)
        lam = (jnp.exp(jnp.sum(lambda_q1[l].astype(jnp.float32) * lambda_k1[l].astype(jnp.float32)))
               - jnp.exp(jnp.sum(lambda_q2[l].astype(jnp.float32) * lambda_k2[l].astype(jnp.float32)))
               + lambda_init)
        attn_out = diff_attention(q, k, v, lam, subln_g[l], lambda_init)
        conv_out = conformer_conv(u, conv_w[l], conv_b[l], conv_ln_g[l], conv_ln_b[l])
        mix = jnp.concatenate([attn_out, conv_out], axis=-1) @ w_out[l]
        x = layer_norm(ALPHA * x + mix, ln1_g[l], ln1_b[l])
        ffn = moe_ffn(x, w_router[l], router_bias[l], w_exp_gate[l], w_exp_up[l], w_exp_down[l],
                      w_sh_gate[l], w_sh_up[l], w_sh_down[l])
        x = layer_norm(ALPHA * x + ffn, ln2_g[l], ln2_b[l])
    return x
```

```python
import functools
import math

import jax
import jax.numpy as jnp
from jax import lax
from jax.experimental import pallas as pl
from jax.experimental.pallas import tpu as pltpu

F32 = jnp.float32
BF16 = jnp.bfloat16
I32 = jnp.int32
U32 = jnp.uint32

N_HEADS = 4
HEAD_DIM = 128
SUB_DIM = 64
ROPE_DIM = 16
ROPE_THETA = 500000.0
CHUNK = 64
CONV_WIDTH = 31
N_EXPERTS = 64
TOP_K = 8
N_GROUPS = 8
GROUP_SIZE = N_EXPERTS // N_GROUPS
TOPK_GROUPS = 4
ROUTED_SCALE = 2.5
LN_EPS = 1e-5
DEPTH = 1
ALPHA = (2.0 * DEPTH) ** 0.25

LANES = 128
SUBLANES = 8
VMEM_LIMIT = 56 * 1024 * 1024

PROJ_ROWS = 512
ATTN_BLOCK = 256
MIX_ROWS = 256
CONV_HALO = 32
CONV_ROW_CHUNK = 64
EXPERT_ROWS = 128
MOVE_ROWS = 128


def _dot(a, b):
    return jnp.dot(a, b, preferred_element_type=F32)


def _layer_norm(z, g, b):
    mu = jnp.mean(z, axis=-1, keepdims=True)
    zc = z - mu
    var = jnp.mean(zc * zc, axis=-1, keepdims=True)
    return zc * lax.rsqrt(var + LN_EPS) * g + b


def _proj_kernel(x_ref, pos_ref, invf_ref, w_ref, qT_ref, k_ref, vT_ref, h_ref, *, d_attn, d_conv, blk):
    xb = x_ref[...].astype(BF16)
    tm = xb.shape[0]
    ang = pos_ref[...].astype(F32) * invf_ref[...]
    cos = jnp.cos(ang)
    sin = jnp.sin(ang)
    d = lax.broadcasted_iota(I32, (1, LANES), 1) & (SUB_DIM - 1)
    half = ROPE_DIM // 2
    c_mul = jnp.where(d < ROPE_DIM, cos, 1.0)
    s_lo = jnp.where(d < half, -sin, 0.0)
    s_hi = jnp.where((d >= half) & (d < ROPE_DIM), sin, 0.0)

    def rope(t):
        outs = []
        for c in range(t.shape[1] // LANES):
            ts = t[:, c * LANES:(c + 1) * LANES]
            outs.append(ts * c_mul + pltpu.roll(ts, LANES - half, 1) * s_lo + pltpu.roll(ts, half, 1) * s_hi)
        return jnp.concatenate(outs, axis=1)

    q = rope(_dot(xb, w_ref[:, 0:d_attn])) * (SUB_DIM ** -0.5)
    for c in range(tm // blk):
        qT_ref[0, c] = q[c * blk:(c + 1) * blk, :].T.astype(BF16)
    k = rope(_dot(xb, w_ref[:, d_attn:2 * d_attn]))
    k_ref[0] = k.astype(BF16)
    v = _dot(xb, w_ref[:, 2 * d_attn:3 * d_attn])
    for c in range(tm // blk):
        vT_ref[0, c] = v[c * blk:(c + 1) * blk, :].T.astype(BF16)
    a = _dot(xb, w_ref[:, 3 * d_attn:3 * d_attn + d_conv])
    gate = _dot(xb, w_ref[:, 3 * d_attn + d_conv:3 * d_attn + 2 * d_conv])
    h_ref[0] = a * jax.nn.sigmoid(gate)


def _proj(x, positions, w_in_bf16, d_attn, d_conv):
    B, S, D = x.shape
    tm, blk = PROJ_ROWS, ATTN_BLOCK
    nb = S // blk
    inv_freq = ROPE_THETA ** (-jnp.arange(0, ROPE_DIM, 2, dtype=F32) / ROPE_DIM)
    half = ROPE_DIM // 2
    sub = jnp.concatenate([inv_freq, inv_freq, jnp.zeros((SUB_DIM - ROPE_DIM,), F32)])
    invf = jnp.tile(sub, LANES // SUB_DIM).reshape(1, LANES)
    d_in = w_in_bf16.shape[1]
    kern = functools.partial(_proj_kernel, d_attn=d_attn, d_conv=d_conv, blk=blk)
    return pl.pallas_call(
        kern,
        grid=(B, S // tm),
        in_specs=[
            pl.BlockSpec((None, tm, D), lambda b, i: (b, i, 0)),
            pl.BlockSpec((None, tm, 1), lambda b, i: (b, i, 0)),
            pl.BlockSpec((1, LANES), lambda b, i: (0, 0)),
            pl.BlockSpec((D, d_in), lambda b, i: (0, 0)),
        ],
        out_specs=[
            pl.BlockSpec((1, tm // blk, d_attn, blk), lambda b, i: (b, i, 0, 0)),
            pl.BlockSpec((1, tm, d_attn), lambda b, i: (b, i, 0)),
            pl.BlockSpec((1, tm // blk, d_attn, blk), lambda b, i: (b, i, 0, 0)),
            pl.BlockSpec((1, tm, d_conv), lambda b, i: (b, i, 0)),
        ],
        out_shape=[
            jax.ShapeDtypeStruct((B, nb, d_attn, blk), BF16),
            jax.ShapeDtypeStruct((B, S, d_attn), BF16),
            jax.ShapeDtypeStruct((B, nb, d_attn, blk), BF16),
            jax.ShapeDtypeStruct((B, S, d_conv), F32),
        ],
        compiler_params=pltpu.CompilerParams(
            dimension_semantics=("parallel", "parallel"), vmem_limit_bytes=VMEM_LIMIT),
        name="proj",
    )(x, positions.reshape(B, S, 1), invf, w_in_bf16)


def _attn_kernel(lq1_ref, lk1_ref, lq2_ref, lk2_ref, g_ref, qT_ref, k_ref, vT_ref, o_ref, *, blk, lambda_init):
    i = pl.program_id(2)
    lam = (jnp.exp(jnp.sum(lq1_ref[...] * lk1_ref[...], axis=-1, keepdims=True))
           - jnp.exp(jnp.sum(lq2_ref[...] * lk2_ref[...], axis=-1, keepdims=True)) + lambda_init)
    qT = qT_ref[0, 0]
    row = lax.broadcasted_iota(I32, (HEAD_DIM, 1), 0)
    zero = jnp.zeros_like(qT)
    q_maps = (jnp.where(row < SUB_DIM, qT, zero), jnp.where(row >= SUB_DIM, qT, zero))

    def update(s, vb, m, l, acc):
        m_new = jnp.maximum(m, jnp.max(s, axis=0, keepdims=True))
        alpha = jnp.exp(m - m_new)
        p = jnp.exp(s - m_new)
        l_new = alpha * l + jnp.sum(p, axis=0, keepdims=True)
        acc_new = alpha * acc + _dot(vb, p.astype(BF16))
        return m_new, l_new, acc_new

    def step(j, carry, diagonal):
        kb = k_ref[0, pl.ds(pl.multiple_of(j * blk, blk), blk), :]
        vb = vT_ref[0, j]
        out = []
        for c in range(2):
            s = _dot(kb, q_maps[c])
            if diagonal:
                kc = lax.broadcasted_iota(I32, (blk, 1), 0) // CHUNK
                qc = lax.broadcasted_iota(I32, (1, blk), 1) // CHUNK
                s = jnp.where(kc <= qc, s, -jnp.inf)
            out.extend(update(s, vb, *carry[3 * c:3 * c + 3]))
        return tuple(out)

    init = []
    for _ in range(2):
        init += [jnp.full((1, blk), -jnp.inf, F32), jnp.zeros((1, blk), F32), jnp.zeros((HEAD_DIM, blk), F32)]
    carry = lax.fori_loop(0, i, lambda j, c: step(j, c, False), tuple(init))
    _, l1, a1, _, l2, a2 = step(i, carry, True)
    o = a1 / l1 - lam * (a2 / l2)
    o = o * lax.rsqrt(jnp.mean(o * o, axis=0, keepdims=True) + LN_EPS)
    o = o * g_ref[...] * (1.0 - lambda_init)
    o_ref[0] = o.T.astype(o_ref.dtype)


def _attention(qT, k, vT, lq1, lk1, lq2, lk2, sub_g, lambda_init):
    B, nb, d_attn, blk = qT.shape
    S = nb * blk
    kern = functools.partial(_attn_kernel, blk=blk, lambda_init=lambda_init)
    vec = pl.BlockSpec((1, SUB_DIM), lambda b, h, i: (0, 0))
    return pl.pallas_call(
        kern,
        grid=(B, N_HEADS, nb),
        in_specs=[
            vec, vec, vec, vec,
            pl.BlockSpec((HEAD_DIM, 1), lambda b, h, i: (0, 0)),
            pl.BlockSpec((1, 1, HEAD_DIM, blk), lambda b, h, i: (b, i, h, 0)),
            pl.BlockSpec((1, S, HEAD_DIM), lambda b, h, i: (b, 0, h)),
            pl.BlockSpec((1, nb, HEAD_DIM, blk), lambda b, h, i: (b, 0, h, 0)),
        ],
        out_specs=pl.BlockSpec((1, blk, HEAD_DIM), lambda b, h, i: (b, i, h)),
        out_shape=jax.ShapeDtypeStruct((B, S, d_attn), BF16),
        compiler_params=pltpu.CompilerParams(
            dimension_semantics=("parallel", "parallel", "arbitrary"), vmem_limit_bytes=VMEM_LIMIT),
        name="attn",
    )(lq1.reshape(1, SUB_DIM), lk1.reshape(1, SUB_DIM), lq2.reshape(1, SUB_DIM), lk2.reshape(1, SUB_DIM),
      sub_g.reshape(HEAD_DIM, 1), qT, k, vT)


def _route(scores, bias):
    n_e, tm = scores.shape
    neg = -jnp.inf
    biased = scores + bias
    g3 = biased.reshape(N_GROUPS, GROUP_SIZE, tm)
    io3 = lax.broadcasted_iota(I32, g3.shape, 1)
    m1 = jnp.max(g3, axis=1, keepdims=True)
    first = jnp.min(jnp.where(g3 == m1, io3, GROUP_SIZE), axis=1, keepdims=True)
    m2 = jnp.max(jnp.where(io3 == first, neg, g3), axis=1, keepdims=True)
    grp_score = (m1 + m2).reshape(N_GROUPS, tm)
    gio = lax.broadcasted_iota(I32, (N_GROUPS, tm), 0)
    grp_sel = jnp.zeros((N_GROUPS, tm), jnp.bool_)
    cur = grp_score
    for _ in range(TOPK_GROUPS):
        mx = jnp.max(cur, axis=0, keepdims=True)
        f = jnp.min(jnp.where(cur == mx, gio, N_GROUPS), axis=0, keepdims=True)
        hit = gio == f
        grp_sel = grp_sel | hit
        cur = jnp.where(hit, neg, cur)
    grp_f = jnp.where(grp_sel, 1.0, 0.0).astype(F32)
    exp_mask = jnp.broadcast_to(grp_f.reshape(N_GROUPS, 1, tm), (N_GROUPS, GROUP_SIZE, tm)).reshape(n_e, tm) > 0.5
    eio = lax.broadcasted_iota(I32, (n_e, tm), 0)
    cur = jnp.where(exp_mask, biased, neg)
    sel = jnp.zeros((n_e, tm), jnp.bool_)
    picks = []
    for _ in range(TOP_K):
        mx = jnp.max(cur, axis=0, keepdims=True)
        f = jnp.min(jnp.where(cur == mx, eio, n_e), axis=0, keepdims=True)
        hit = eio == f
        sel = sel | hit
        cur = jnp.where(hit, neg, cur)
        picks.append(f)
    denom = jnp.sum(jnp.where(sel, scores, 0.0), axis=0, keepdims=True)
    gates = jnp.where(sel, scores / denom * ROUTED_SCALE, 0.0)
    return sel, gates, picks, eio


def _mix_kernel(h_ref, halo_ref, cw_ref, cb_ref, cg_ref, cbeta_ref, attn_ref, wo_ref, x_ref, g1_ref, b1_ref,
                wrh_ref, wrl_ref, rb_ref,
                x1_ref, idx_ref, rank_ref, gate_ref, cnt_ref,
                buf, cbuf, carry, *, tm, d_attn):
    first_step = (pl.program_id(0) == 0) & (pl.program_id(1) == 0)

    @pl.when(first_step)
    def _():
        carry[...] = jnp.zeros_like(carry)

    @pl.when(pl.program_id(1) == 0)
    def _():
        buf[0:CONV_HALO, :] = jnp.zeros((CONV_HALO, buf.shape[1]), F32)

    @pl.when(pl.program_id(1) != 0)
    def _():
        buf[0:CONV_HALO, :] = halo_ref[...]

    buf[CONV_HALO:, :] = h_ref[...]
    d_conv = buf.shape[1]
    off = CONV_HALO - (CONV_WIDTH - 1)
    for c in range(d_conv // LANES):
        cs = slice(c * LANES, (c + 1) * LANES)
        for r in range(tm // CONV_ROW_CHUNK):
            acc = jnp.zeros((CONV_ROW_CHUNK, LANES), F32)
            for j in range(CONV_WIDTH):
                start = r * CONV_ROW_CHUNK + off + j
                acc = acc + cw_ref[j:j + 1, cs] * buf[start:start + CONV_ROW_CHUNK, cs]
            cbuf[r * CONV_ROW_CHUNK:(r + 1) * CONV_ROW_CHUNK, cs] = acc + cb_ref[:, cs]
    conv = _layer_norm(cbuf[...], cg_ref[...], cbeta_ref[...])
    conv = conv * jax.nn.sigmoid(conv)

    mix = _dot(attn_ref[...], wo_ref[0:d_attn, :]) + _dot(conv.astype(BF16), wo_ref[d_attn:, :])
    x1 = _layer_norm(ALPHA * x_ref[...] + mix, g1_ref[...], b1_ref[...])
    x1_ref[...] = x1

    xh = x1.astype(BF16)
    xl = (x1 - xh.astype(F32)).astype(BF16)
    logits = _dot(xh, wrh_ref[...]) + _dot(xl, wrh_ref[...]) + _dot(xh, wrl_ref[...])
    scores = jax.nn.sigmoid(logits).T
    sel, gates, picks, eio = _route(scores, rb_ref[...])

    t_row = lax.broadcasted_iota(I32, (tm, tm), 0)
    t_col = lax.broadcasted_iota(I32, (tm, tm), 1)
    before = jnp.where(t_row < t_col, 1.0, 0.0).astype(BF16)
    sel_f = jnp.where(sel, 1.0, 0.0).astype(F32)
    rank = _dot(sel_f.astype(BF16), before) + carry[...]
    carry[...] = carry[...] + jnp.sum(sel_f, axis=1, keepdims=True)
    cnt_ref[...] = carry[...].astype(I32)

    idx_rows, rank_rows, gate_rows = [], [], []
    for f in picks:
        hit = eio == f
        idx_rows.append(f)
        rank_rows.append(jnp.sum(jnp.where(hit, rank, 0.0), axis=0, keepdims=True))
        gate_rows.append(jnp.sum(jnp.where(hit, gates, 0.0), axis=0, keepdims=True))
    idx_ref[...] = jnp.concatenate(idx_rows, axis=0)
    rank_ref[...] = jnp.concatenate(rank_rows, axis=0).astype(I32)
    gate_ref[...] = jnp.concatenate(gate_rows, axis=0)


def _mix(h, attn, x, conv_w, conv_b, cln_g, cln_b, w_out_bf16, ln1_g, ln1_b, w_router, router_bias):
    B, S, D = x.shape
    d_conv = h.shape[-1]
    d_attn = attn.shape[-1]
    tm = MIX_ROWS
    nt = S // tm
    N = B * S
    n_e = w_router.shape[-1]
    wr_hi = w_router.astype(BF16)
    wr_lo = (w_router - wr_hi.astype(F32)).astype(BF16)
    halo_blocks = tm // CONV_HALO
    row = lambda a: a.reshape(1, -1)
    const = lambda shape: pl.BlockSpec(shape, lambda b, i: (0,) * len(shape))
    kern = functools.partial(_mix_kernel, tm=tm, d_attn=d_attn)
    return pl.pallas_call(
        kern,
        grid=(B, nt),
        in_specs=[
            pl.BlockSpec((None, tm, d_conv), lambda b, i: (b, i, 0)),
            pl.BlockSpec((None, CONV_HALO, d_conv), lambda b, i: (b, jnp.maximum(i * halo_blocks - 1, 0), 0)),
            const(conv_w.shape), const((1, d_conv)), const((1, d_conv)), const((1, d_conv)),
            pl.BlockSpec((None, tm, d_attn), lambda b, i: (b, i, 0)),
            const(w_out_bf16.shape),
            pl.BlockSpec((None, tm, D), lambda b, i: (b, i, 0)),
            const((1, D)), const((1, D)),
            const(wr_hi.shape), const(wr_lo.shape), const((n_e, 1)),
        ],
        out_specs=[
            pl.BlockSpec((tm, D), lambda b, i: (b * nt + i, 0)),
            pl.BlockSpec((TOP_K, tm), lambda b, i: (0, b * nt + i)),
            pl.BlockSpec((TOP_K, tm), lambda b, i: (0, b * nt + i)),
            pl.BlockSpec((TOP_K, tm), lambda b, i: (0, b * nt + i)),
            const((n_e, 1)),
        ],
        out_shape=[
            jax.ShapeDtypeStruct((N, D), F32),
            jax.ShapeDtypeStruct((TOP_K, N), I32),
            jax.ShapeDtypeStruct((TOP_K, N), I32),
            jax.ShapeDtypeStruct((TOP_K, N), F32),
            jax.ShapeDtypeStruct((n_e, 1), I32),
        ],
        scratch_shapes=[
            pltpu.VMEM((tm + CONV_HALO, d_conv), F32),
            pltpu.VMEM((tm, d_conv), F32),
            pltpu.VMEM((n_e, 1), F32),
        ],
        compiler_params=pltpu.CompilerParams(
            dimension_semantics=("arbitrary", "arbitrary"), vmem_limit_bytes=VMEM_LIMIT),
        name="mix",
    )(h, h, conv_w, row(conv_b), row(cln_g), row(cln_b), attn, w_out_bf16, x, row(ln1_g), row(ln1_b),
      wr_hi, wr_lo, router_bias.reshape(n_e, 1))


def _plan_kernel(start_ref, idx_ref, rank_ref, dest_ref, *, n_e):
    idx = idx_ref[...]
    base = jnp.zeros(idx.shape, I32)
    for e in range(n_e):
        base = jnp.where(idx == e, start_ref[e], base)
    dest_ref[...] = base + rank_ref[...]


def _plan(pad_start, idx8, rank8):
    k, N = idx8.shape
    tn = min(N, 2048)
    kern = functools.partial(_plan_kernel, n_e=pad_start.shape[0])
    return pl.pallas_call(
        kern,
        grid_spec=pltpu.PrefetchScalarGridSpec(
            num_scalar_prefetch=1,
            grid=(N // tn,),
            in_specs=[pl.BlockSpec((k, tn), lambda i, s: (0, i)), pl.BlockSpec((k, tn), lambda i, s: (0, i))],
            out_specs=pl.BlockSpec((k, tn), lambda i, s: (0, i)),
        ),
        out_shape=jax.ShapeDtypeStruct((k, N), I32),
        name="plan",
    )(pad_start, idx8, rank8)


def _dispatch_kernel(dest_ref, x_ref, init_ref, xs_ref, sem, *, tm):
    del init_ref

    def row_copy(r, d):
        return pltpu.make_async_copy(x_ref.at[pl.ds(r, 1), :], xs_ref.at[pl.ds(d, 1), :], sem)

    def issue(r, c):
        for k in range(TOP_K):
            row_copy(r, dest_ref[k, r]).start()
        return c

    lax.fori_loop(0, tm, issue, 0)

    def drain(r, c):
        for k in range(TOP_K):
            row_copy(0, 0).wait()
        return c

    lax.fori_loop(0, tm, drain, 0)


def _dispatch(x1, dest8, n_rows):
    N, D = x1.shape
    tm = MOVE_ROWS
    kern = functools.partial(_dispatch_kernel, tm=tm)
    return pl.pallas_call(
        kern,
        grid=(N // tm,),
        in_specs=[
            pl.BlockSpec((TOP_K, tm), lambda i: (0, i), memory_space=pltpu.SMEM),
            pl.BlockSpec((tm, D), lambda i: (i, 0)),
            pl.BlockSpec(memory_space=pl.ANY),
        ],
        out_specs=pl.BlockSpec(memory_space=pl.ANY),
        out_shape=jax.ShapeDtypeStruct((n_rows, D), x1.dtype),
        scratch_shapes=[pltpu.SemaphoreType.DMA],
        input_output_aliases={2: 0},
        compiler_params=pltpu.CompilerParams(dimension_semantics=("arbitrary",)),
        name="dispatch",
    )(dest8, x1, jnp.zeros((n_rows, D), x1.dtype))


def _expert_kernel(blk_exp_ref, n_used_ref, xs_ref, wgu_ref, wd_ref, y_ref, *, d_e):
    del blk_exp_ref

    @pl.when(pl.program_id(0) < n_used_ref[0])
    def _():
        xb = xs_ref[...].astype(BF16)
        gu = _dot(xb, wgu_ref[0])
        g = gu[:, :d_e]
        a = g * jax.nn.sigmoid(g) * gu[:, d_e:]
        y_ref[...] = _dot(a.astype(BF16), wd_ref[0])

    @pl.when(pl.program_id(0) >= n_used_ref[0])
    def _():
        y_ref[...] = jnp.zeros_like(y_ref)


def _experts(xs, blk_exp, n_used, wgu, wd):
    P, D = xs.shape
    n_blocks = P // EXPERT_ROWS
    d_e = wd.shape[1]
    kern = functools.partial(_expert_kernel, d_e=d_e)
    return pl.pallas_call(
        kern,
        grid_spec=pltpu.PrefetchScalarGridSpec(
            num_scalar_prefetch=2,
            grid=(n_blocks,),
            in_specs=[
                pl.BlockSpec((EXPERT_ROWS, D), lambda b, be, nu: (b, 0)),
                pl.BlockSpec((1, D, 2 * d_e), lambda b, be, nu: (be[b], 0, 0)),
                pl.BlockSpec((1, d_e, D), lambda b, be, nu: (be[b], 0, 0)),
            ],
            out_specs=pl.BlockSpec((EXPERT_ROWS, D), lambda b, be, nu: (b, 0)),
        ),
        out_shape=jax.ShapeDtypeStruct((P, D), F32),
        compiler_params=pltpu.CompilerParams(dimension_semantics=("arbitrary",), vmem_limit_bytes=VMEM_LIMIT),
        name="experts",
    )(blk_exp, n_used, xs, wgu, wd)


def _combine_kernel(dest_ref, gate_ref, x1_ref, wsgu_ref, wsd_ref, g2_ref, b2_ref, y_ref, o_ref, ybuf, sem,
                    *, tm, d_s):
    def row_copy(r, k, d):
        return pltpu.make_async_copy(y_ref.at[pl.ds(d, 1), :], ybuf.at[k, pl.ds(r, 1), :], sem)

    def issue(r, c):
        for k in range(TOP_K):
            row_copy(r, k, dest_ref[k, r]).start()
        return c

    lax.fori_loop(0, tm, issue, 0)

    x1 = x1_ref[...]
    gu = _dot(x1.astype(BF16), wsgu_ref[...])
    g = gu[:, :d_s]
    shared = _dot((g * jax.nn.sigmoid(g) * gu[:, d_s:]).astype(BF16), wsd_ref[...])
    gates = gate_ref[...].T

    def drain(r, c):
        for k in range(TOP_K):
            row_copy(0, k, 0).wait()
        return c

    lax.fori_loop(0, tm, drain, 0)

    routed = ybuf[0] * gates[:, 0:1]
    for k in range(1, TOP_K):
        routed = routed + ybuf[k] * gates[:, k:k + 1]
    o_ref[...] = _layer_norm(ALPHA * x1 + (routed + shared), g2_ref[...], b2_ref[...])


def _combine(y, dest8, gate8, x1, wsgu, wsd, ln2_g, ln2_b):
    N, D = x1.shape
    tm = MOVE_ROWS
    d_s = wsd.shape[0]
    kern = functools.partial(_combine_kernel, tm=tm, d_s=d_s)
    const = lambda shape: pl.BlockSpec(shape, lambda i: (0,) * len(shape))
    return pl.pallas_call(
        kern,
        grid=(N // tm,),
        in_specs=[
            pl.BlockSpec((TOP_K, tm), lambda i: (0, i), memory_space=pltpu.SMEM),
            pl.BlockSpec((TOP_K, tm), lambda i: (0, i)),
            pl.BlockSpec((tm, D), lambda i: (i, 0)),
            const(wsgu.shape), const(wsd.shape), const((1, D)), const((1, D)),
            pl.BlockSpec(memory_space=pl.ANY),
        ],
        out_specs=pl.BlockSpec((tm, D), lambda i: (i, 0)),
        out_shape=jax.ShapeDtypeStruct((N, D), F32),
        scratch_shapes=[pltpu.VMEM((TOP_K, tm, D), F32), pltpu.SemaphoreType.DMA],
        compiler_params=pltpu.CompilerParams(dimension_semantics=("arbitrary",), vmem_limit_bytes=VMEM_LIMIT),
        name="combine",
    )(dest8, gate8, x1, wsgu, wsd, ln2_g.reshape(1, D), ln2_b.reshape(1, D), y)


def _moe(x1, idx8, rank8, gate8, counts, w_gate, w_up, w_down, ws_gate, ws_up, ws_down, ln2_g, ln2_b):
    N, D = x1.shape
    n_e = w_gate.shape[0]
    blk = EXPERT_ROWS
    counts = counts.reshape(n_e)
    padded = (counts + blk - 1) // blk * blk
    pad_end = jnp.cumsum(padded)
    pad_start = (pad_end - padded).astype(I32)
    n_blocks = (N * TOP_K + n_e * (blk - 1) + blk - 1) // blk
    blk_exp = jnp.minimum(jnp.searchsorted(pad_end, jnp.arange(n_blocks) * blk, side='right'), n_e - 1).astype(I32)
    n_used = (pad_end[-1:] // blk).astype(I32)
    dest8 = _plan(pad_start, idx8, rank8)
    xs = _dispatch(x1, dest8, n_blocks * blk)
    wgu = jnp.concatenate([w_gate, w_up], axis=-1).astype(BF16)
    y = _experts(xs, blk_exp, n_used, wgu, w_down.astype(BF16))
    wsgu = jnp.concatenate([ws_gate, ws_up], axis=-1).astype(BF16)
    return _combine(y, dest8, gate8, x1, wsgu, ws_down.astype(BF16), ln2_g, ln2_b)


def kernel(x, positions, w_in, lambda_q1, lambda_k1, lambda_q2, lambda_k2, subln_g, conv_w, conv_b, conv_ln_g, conv_ln_b, w_out, ln1_g, ln1_b, w_router, router_bias, w_exp_gate, w_exp_up, w_exp_down, w_sh_gate, w_sh_up, w_sh_down, ln2_g, ln2_b):
    B, S, D = x.shape
    d_attn = N_HEADS * HEAD_DIM
    d_conv = conv_w.shape[-1]
    for l in range(DEPTH):
        lambda_init = 0.8 - 0.6 * math.exp(-0.3 * l)
        qT, k, vT, h = _proj(x, positions, w_in[l].astype(BF16), d_attn, d_conv)
        attn = _attention(qT, k, vT, lambda_q1[l], lambda_k1[l], lambda_q2[l], lambda_k2[l], subln_g[l],
                          lambda_init)
        x1, idx8, rank8, gate8, counts = _mix(h, attn, x, conv_w[l], conv_b[l], conv_ln_g[l], conv_ln_b[l],
                                              w_out[l].astype(BF16), ln1_g[l], ln1_b[l], w_router[l],
                                              router_bias[l])
        out = _moe(x1, idx8, rank8, gate8, counts, w_exp_gate[l], w_exp_up[l], w_exp_down[l],
                   w_sh_gate[l], w_sh_up[l], w_sh_down[l], ln2_g[l], ln2_b[l])
        x = out.reshape(B, S, D)
    return x
```

```python
import functools
import math

import jax
import jax.numpy as jnp
from jax import lax
from jax.experimental import pallas as pl
from jax.experimental.pallas import tpu as pltpu

F32 = jnp.float32
BF16 = jnp.bfloat16
I32 = jnp.int32
U32 = jnp.uint32

N_HEADS = 4
HEAD_DIM = 128
SUB_DIM = 64
ROPE_DIM = 16
ROPE_THETA = 500000.0
CHUNK = 64
CONV_WIDTH = 31
N_EXPERTS = 64
TOP_K = 8
N_GROUPS = 8
GROUP_SIZE = N_EXPERTS // N_GROUPS
TOPK_GROUPS = 4
ROUTED_SCALE = 2.5
LN_EPS = 1e-5
DEPTH = 1
ALPHA = (2.0 * DEPTH) ** 0.25
LOG2_E = math.log2(math.e)

LANES = 128
SUBLANES = 8
VMEM_LIMIT = 56 * 1024 * 1024

PROJ_ROWS = 512
ATTN_BLOCK = 256
MIX_ROWS = 256
CONV_HALO = 32
CONV_ROW_CHUNK = 64
MOE_WINDOW = 1024
EXPERT_ROWS = 128


def _dot(a, b):
    return jnp.dot(a, b, preferred_element_type=F32)


def _layer_norm(z, g, b):
    mu = jnp.mean(z, axis=-1, keepdims=True)
    zc = z - mu
    var = jnp.mean(zc * zc, axis=-1, keepdims=True)
    return zc * lax.rsqrt(var + LN_EPS) * g + b


def _proj_kernel(x_ref, pos_ref, invf_ref, w_ref, qT_ref, k_ref, vT_ref, h_ref, *, d_attn, d_conv, blk):
    xb = x_ref[...].astype(BF16)
    tm = xb.shape[0]
    ang = pos_ref[...].astype(F32) * invf_ref[...]
    cos = jnp.cos(ang)
    sin = jnp.sin(ang)
    d = lax.broadcasted_iota(I32, (1, LANES), 1) & (SUB_DIM - 1)
    half = ROPE_DIM // 2
    c_mul = jnp.where(d < ROPE_DIM, cos, 1.0)
    s_lo = jnp.where(d < half, -sin, 0.0)
    s_hi = jnp.where((d >= half) & (d < ROPE_DIM), sin, 0.0)

    def rope(t):
        outs = []
        for c in range(t.shape[1] // LANES):
            ts = t[:, c * LANES:(c + 1) * LANES]
            outs.append(ts * c_mul + pltpu.roll(ts, LANES - half, 1) * s_lo + pltpu.roll(ts, half, 1) * s_hi)
        return jnp.concatenate(outs, axis=1)

    q = rope(_dot(xb, w_ref[:, 0:d_attn])) * (SUB_DIM ** -0.5 * LOG2_E)
    for c in range(tm // blk):
        qT_ref[0, c] = q[c * blk:(c + 1) * blk, :].T.astype(BF16)
    k = rope(_dot(xb, w_ref[:, d_attn:2 * d_attn]))
    k_ref[0] = k.astype(BF16)
    v = _dot(xb, w_ref[:, 2 * d_attn:3 * d_attn])
    for c in range(tm // blk):
        vT_ref[0, c] = v[c * blk:(c + 1) * blk, :].T.astype(BF16)
    a = _dot(xb, w_ref[:, 3 * d_attn:3 * d_attn + d_conv])
    gate = _dot(xb, w_ref[:, 3 * d_attn + d_conv:3 * d_attn + 2 * d_conv])
    h_ref[0] = a * jax.nn.sigmoid(gate)


def _proj(x, positions, w_in_bf16, d_attn, d_conv):
    B, S, D = x.shape
    tm, blk = PROJ_ROWS, ATTN_BLOCK
    nb = S // blk
    inv_freq = ROPE_THETA ** (-jnp.arange(0, ROPE_DIM, 2, dtype=F32) / ROPE_DIM)
    half = ROPE_DIM // 2
    sub = jnp.concatenate([inv_freq, inv_freq, jnp.zeros((SUB_DIM - ROPE_DIM,), F32)])
    invf = jnp.tile(sub, LANES // SUB_DIM).reshape(1, LANES)
    d_in = w_in_bf16.shape[1]
    kern = functools.partial(_proj_kernel, d_attn=d_attn, d_conv=d_conv, blk=blk)
    return pl.pallas_call(
        kern,
        grid=(B, S // tm),
        in_specs=[
            pl.BlockSpec((None, tm, D), lambda b, i: (b, i, 0)),
            pl.BlockSpec((None, tm, 1), lambda b, i: (b, i, 0)),
            pl.BlockSpec((1, LANES), lambda b, i: (0, 0)),
            pl.BlockSpec((D, d_in), lambda b, i: (0, 0)),
        ],
        out_specs=[
            pl.BlockSpec((1, tm // blk, d_attn, blk), lambda b, i: (b, i, 0, 0)),
            pl.BlockSpec((1, tm, d_attn), lambda b, i: (b, i, 0)),
            pl.BlockSpec((1, tm // blk, d_attn, blk), lambda b, i: (b, i, 0, 0)),
            pl.BlockSpec((1, tm, d_conv), lambda b, i: (b, i, 0)),
        ],
        out_shape=[
            jax.ShapeDtypeStruct((B, nb, d_attn, blk), BF16),
            jax.ShapeDtypeStruct((B, S, d_attn), BF16),
            jax.ShapeDtypeStruct((B, nb, d_attn, blk), BF16),
            jax.ShapeDtypeStruct((B, S, d_conv), F32),
        ],
        compiler_params=pltpu.CompilerParams(
            dimension_semantics=("parallel", "parallel"), vmem_limit_bytes=VMEM_LIMIT),
        name="proj",
    )(x, positions.reshape(B, S, 1), invf, w_in_bf16)


def _attn_kernel(lq1_ref, lk1_ref, lq2_ref, lk2_ref, g_ref, qT_ref, k_ref, vT_ref, o_ref,
                 s_a, s_b, p_a, p_b, acc_ref, m_ref, *, blk, lambda_init):
    i = pl.program_id(2)
    lam = (jnp.exp(jnp.sum(lq1_ref[...] * lk1_ref[...], axis=-1, keepdims=True))
           - jnp.exp(jnp.sum(lq2_ref[...] * lk2_ref[...], axis=-1, keepdims=True)) + lambda_init)
    qT = qT_ref[0, 0]
    row = lax.broadcasted_iota(I32, (HEAD_DIM, 1), 0)
    zero = jnp.zeros_like(qT)
    q_both = jnp.concatenate([jnp.where(row < SUB_DIM, qT, zero), jnp.where(row >= SUB_DIM, qT, zero)], axis=1)

    ones = jnp.ones((2 * SUBLANES, blk), BF16)

    def scores_into(j, s_ref):
        kb = k_ref[0, pl.ds(pl.multiple_of(j * blk, blk), blk), :]
        s_ref[...] = _dot(kb, q_both)

    def weighted_values(j, p):
        return _dot(jnp.concatenate([vT_ref[0, j], ones], axis=0), p)

    def probs(s):
        m = m_ref[...]
        m_new = jnp.maximum(m, jnp.max(s, axis=0, keepdims=True))
        m_ref[...] = m_new
        return jnp.exp2(m - m_new), jnp.exp2(s - m_new).astype(BF16)

    def step(j, s_cur, s_nxt, p_prev, p_cur):
        scores_into(j + 1, s_nxt)
        pv_prev = weighted_values(jnp.maximum(j - 1, 0), p_prev[...])
        alpha, p = probs(s_cur[...])
        p_cur[...] = p
        acc_ref[...] = alpha * (acc_ref[...] + pv_prev)

    def finish(s_cur, p_prev):
        kc = lax.broadcasted_iota(I32, (blk, 1), 0) // CHUNK
        qc = (lax.broadcasted_iota(I32, (1, 2 * blk), 1) & (blk - 1)) // CHUNK
        pv_prev = weighted_values(jnp.maximum(i - 1, 0), p_prev[...])
        alpha, p = probs(jnp.where(kc <= qc, s_cur[...], -jnp.inf))
        acc = alpha * (acc_ref[...] + pv_prev) + weighted_values(i, p)
        o = acc[:HEAD_DIM] / acc[HEAD_DIM:HEAD_DIM + 1]
        o = o[:, :blk] - lam * o[:, blk:]
        o = o * lax.rsqrt(jnp.mean(o * o, axis=0, keepdims=True) + LN_EPS)
        o = o * g_ref[...] * (1.0 - lambda_init)
        o_ref[0] = o.T.astype(o_ref.dtype)

    m_ref[...] = jnp.full(m_ref.shape, -jnp.inf, F32)
    acc_ref[...] = jnp.zeros(acc_ref.shape, F32)
    p_b[...] = jnp.zeros(p_b.shape, BF16)
    scores_into(0, s_a)

    def pair(jj, c):
        step(2 * jj, s_a, s_b, p_b, p_a)
        step(2 * jj + 1, s_b, s_a, p_a, p_b)
        return c

    lax.fori_loop(0, i // 2, pair, 0)

    @pl.when(i % 2 == 1)
    def _():
        step(i - 1, s_a, s_b, p_b, p_a)
        finish(s_b, p_a)

    @pl.when(i % 2 == 0)
    def _():
        finish(s_a, p_b)


def _attention(qT, k, vT, lq1, lk1, lq2, lk2, sub_g, lambda_init):
    B, nb, d_attn, blk = qT.shape
    S = nb * blk
    kern = functools.partial(_attn_kernel, blk=blk, lambda_init=lambda_init)
    vec = pl.BlockSpec((1, SUB_DIM), lambda b, h, i: (0, 0))
    return pl.pallas_call(
        kern,
        grid=(B, N_HEADS, nb),
        in_specs=[
            vec, vec, vec, vec,
            pl.BlockSpec((HEAD_DIM, 1), lambda b, h, i: (0, 0)),
            pl.BlockSpec((1, 1, HEAD_DIM, blk), lambda b, h, i: (b, i, h, 0)),
            pl.BlockSpec((1, S, HEAD_DIM), lambda b, h, i: (b, 0, h)),
            pl.BlockSpec((1, nb, HEAD_DIM, blk), lambda b, h, i: (b, 0, h, 0)),
        ],
        out_specs=pl.BlockSpec((1, blk, HEAD_DIM), lambda b, h, i: (b, i, h)),
        out_shape=jax.ShapeDtypeStruct((B, S, d_attn), BF16),
        scratch_shapes=[
            pltpu.VMEM((blk, 2 * blk), F32), pltpu.VMEM((blk, 2 * blk), F32),
            pltpu.VMEM((blk, 2 * blk), BF16), pltpu.VMEM((blk, 2 * blk), BF16),
            pltpu.VMEM((HEAD_DIM + 2 * SUBLANES, 2 * blk), F32),
            pltpu.VMEM((1, 2 * blk), F32),
        ],
        compiler_params=pltpu.CompilerParams(
            dimension_semantics=("parallel", "parallel", "arbitrary"), vmem_limit_bytes=VMEM_LIMIT),
        name="attn",
    )(lq1.reshape(1, SUB_DIM), lk1.reshape(1, SUB_DIM), lq2.reshape(1, SUB_DIM), lk2.reshape(1, SUB_DIM),
      sub_g.reshape(HEAD_DIM, 1), qT, k, vT)


def _route(scores, bias):
    n_e, tm = scores.shape
    neg = -jnp.inf
    biased = scores + bias
    g3 = biased.reshape(N_GROUPS, GROUP_SIZE, tm)
    io3 = lax.broadcasted_iota(I32, g3.shape, 1)
    m1 = jnp.max(g3, axis=1, keepdims=True)
    first = jnp.min(jnp.where(g3 == m1, io3, GROUP_SIZE), axis=1, keepdims=True)
    m2 = jnp.max(jnp.where(io3 == first, neg, g3), axis=1, keepdims=True)
    grp_score = (m1 + m2).reshape(N_GROUPS, tm)
    gio = lax.broadcasted_iota(I32, (N_GROUPS, tm), 0)
    grp_sel = jnp.zeros((N_GROUPS, tm), jnp.bool_)
    cur = grp_score
    for _ in range(TOPK_GROUPS):
        mx = jnp.max(cur, axis=0, keepdims=True)
        f = jnp.min(jnp.where(cur == mx, gio, N_GROUPS), axis=0, keepdims=True)
        hit = gio == f
        grp_sel = grp_sel | hit
        cur = jnp.where(hit, neg, cur)
    grp_f = jnp.where(grp_sel, 1.0, 0.0).astype(F32)
    exp_mask = jnp.broadcast_to(grp_f.reshape(N_GROUPS, 1, tm), (N_GROUPS, GROUP_SIZE, tm)).reshape(n_e, tm) > 0.5
    eio = lax.broadcasted_iota(I32, (n_e, tm), 0)
    cur = jnp.where(exp_mask, biased, neg)
    sel = jnp.zeros((n_e, tm), jnp.bool_)
    for _ in range(TOP_K):
        mx = jnp.max(cur, axis=0, keepdims=True)
        f = jnp.min(jnp.where(cur == mx, eio, n_e), axis=0, keepdims=True)
        hit = eio == f
        sel = sel | hit
        cur = jnp.where(hit, neg, cur)
    denom = jnp.sum(jnp.where(sel, scores, 0.0), axis=0, keepdims=True)
    gates = jnp.where(sel, scores / denom * ROUTED_SCALE, 0.0)
    return sel, gates


def _mix_kernel(h_ref, halo_ref, cw_ref, cb_ref, cg_ref, cbeta_ref, attn_ref, wo_ref, x_ref, g1_ref, b1_ref,
                wrh_ref, wrl_ref, rb_ref,
                x1_ref, rank_ref, gate_ref, cnt_ref,
                buf, cbuf, carry, *, tm, d_attn, tiles_per_window):
    tile = pl.program_id(0) * pl.num_programs(1) + pl.program_id(1)

    @pl.when(tile % tiles_per_window == 0)
    def _():
        carry[...] = jnp.zeros_like(carry)

    @pl.when(pl.program_id(1) == 0)
    def _():
        buf[0:CONV_HALO, :] = jnp.zeros((CONV_HALO, buf.shape[1]), F32)

    @pl.when(pl.program_id(1) != 0)
    def _():
        buf[0:CONV_HALO, :] = halo_ref[...]

    buf[CONV_HALO:, :] = h_ref[...]
    d_conv = buf.shape[1]
    off = CONV_HALO - (CONV_WIDTH - 1)
    for c in range(d_conv // LANES):
        cs = slice(c * LANES, (c + 1) * LANES)
        for r in range(tm // CONV_ROW_CHUNK):
            acc = jnp.zeros((CONV_ROW_CHUNK, LANES), F32)
            for j in range(CONV_WIDTH):
                start = r * CONV_ROW_CHUNK + off + j
                acc = acc + cw_ref[j:j + 1, cs] * buf[start:start + CONV_ROW_CHUNK, cs]
            cbuf[r * CONV_ROW_CHUNK:(r + 1) * CONV_ROW_CHUNK, cs] = acc + cb_ref[:, cs]
    conv = _layer_norm(cbuf[...], cg_ref[...], cbeta_ref[...])
    conv = conv * jax.nn.sigmoid(conv)

    mix = _dot(attn_ref[...], wo_ref[0:d_attn, :]) + _dot(conv.astype(BF16), wo_ref[d_attn:, :])
    x1 = _layer_norm(ALPHA * x_ref[...] + mix, g1_ref[...], b1_ref[...])
    x1_ref[...] = x1

    xh = x1.astype(BF16)
    xl = (x1 - xh.astype(F32)).astype(BF16)
    logits = _dot(xh, wrh_ref[...]) + _dot(xl, wrh_ref[...]) + _dot(xh, wrl_ref[...])
    scores = jax.nn.sigmoid(logits).T
    sel, gates = _route(scores, rb_ref[...])
    gate_ref[...] = gates

    t_row = lax.broadcasted_iota(I32, (tm, tm), 0)
    t_col = lax.broadcasted_iota(I32, (tm, tm), 1)
    before = jnp.where(t_row < t_col, 1.0, 0.0).astype(BF16)
    sel_f = jnp.where(sel, 1.0, 0.0).astype(F32)
    rank = _dot(sel_f.astype(BF16), before) + carry[...]
    rank_ref[...] = jnp.where(sel, rank, -1.0).astype(I32)
    carry[...] = carry[...] + jnp.sum(sel_f, axis=1, keepdims=True)
    cnt_ref[...] = carry[...].astype(I32)


def _mix(h, attn, x, conv_w, conv_b, cln_g, cln_b, w_out_bf16, ln1_g, ln1_b, w_router, router_bias):
    B, S, D = x.shape
    d_conv = h.shape[-1]
    d_attn = attn.shape[-1]
    tm = MIX_ROWS
    nt = S // tm
    N = B * S
    n_e = w_router.shape[-1]
    wr_hi = w_router.astype(BF16)
    wr_lo = (w_router - wr_hi.astype(F32)).astype(BF16)
    halo_blocks = tm // CONV_HALO
    row = lambda a: a.reshape(1, -1)
    const = lambda shape: pl.BlockSpec(shape, lambda b, i: (0,) * len(shape))
    tiles_per_window = MOE_WINDOW // tm
    n_win = N // MOE_WINDOW
    kern = functools.partial(_mix_kernel, tm=tm, d_attn=d_attn, tiles_per_window=tiles_per_window)
    return pl.pallas_call(
        kern,
        grid=(B, nt),
        in_specs=[
            pl.BlockSpec((None, tm, d_conv), lambda b, i: (b, i, 0)),
            pl.BlockSpec((None, CONV_HALO, d_conv), lambda b, i: (b, jnp.maximum(i * halo_blocks - 1, 0), 0)),
            const(conv_w.shape), const((1, d_conv)), const((1, d_conv)), const((1, d_conv)),
            pl.BlockSpec((None, tm, d_attn), lambda b, i: (b, i, 0)),
            const(w_out_bf16.shape),
            pl.BlockSpec((None, tm, D), lambda b, i: (b, i, 0)),
            const((1, D)), const((1, D)),
            const(wr_hi.shape), const(wr_lo.shape), const((n_e, 1)),
        ],
        out_specs=[
            pl.BlockSpec((tm, D), lambda b, i: (b * nt + i, 0)),
            pl.BlockSpec((n_e, tm), lambda b, i: (0, b * nt + i)),
            pl.BlockSpec((n_e, tm), lambda b, i: (0, b * nt + i)),
            pl.BlockSpec((None, n_e, 1), lambda b, i: ((b * nt + i) // tiles_per_window, 0, 0)),
        ],
        out_shape=[
            jax.ShapeDtypeStruct((N, D), F32),
            jax.ShapeDtypeStruct((n_e, N), I32),
            jax.ShapeDtypeStruct((n_e, N), F32),
            jax.ShapeDtypeStruct((n_win, n_e, 1), I32),
        ],
        scratch_shapes=[
            pltpu.VMEM((tm + CONV_HALO, d_conv), F32),
            pltpu.VMEM((tm, d_conv), F32),
            pltpu.VMEM((n_e, 1), F32),
        ],
        compiler_params=pltpu.CompilerParams(
            dimension_semantics=("arbitrary", "arbitrary"), vmem_limit_bytes=VMEM_LIMIT),
        name="mix",
    )(h, h, conv_w, row(conv_b), row(cln_g), row(cln_b), attn, w_out_bf16, x, row(ln1_g), row(ln1_b),
      wr_hi, wr_lo, router_bias.reshape(n_e, 1))


def _moe_kernel(nchunk_ref, x1_ref, rank_ref, gate_ref, wgu_ref, wd_ref, wsgu_ref, wsd_ref, g2_ref, b2_ref,
                o_ref, xb_ref, *, rows, d_e, d_s):
    w = pl.program_id(0)
    e = pl.program_id(1)
    n_e = pl.num_programs(1)

    @pl.when(e == 0)
    def _():
        x1 = x1_ref[...]
        xb = x1.astype(BF16)
        xb_ref[...] = xb
        gu = _dot(xb, wsgu_ref[...])
        g = gu[:, :d_s]
        shared = _dot((g * jax.nn.sigmoid(g) * gu[:, d_s:]).astype(BF16), wsd_ref[...])
        o_ref[...] = ALPHA * x1 + shared

    rank = rank_ref[pl.ds(e, 1), :]
    gate = gate_ref[pl.ds(e, 1), :]

    def chunk(c, carry):
        hit = (lax.broadcasted_iota(I32, (rows, 1), 0) + c * rows) == rank
        onehot = jnp.where(hit, 1.0, 0.0).astype(BF16)
        xs = _dot(onehot, xb_ref[...]).astype(BF16)
        gu = _dot(xs, wgu_ref[0])
        g = gu[:, :d_e]
        y = _dot((g * jax.nn.sigmoid(g) * gu[:, d_e:]).astype(BF16), wd_ref[0])
        row_gate = jnp.sum(jnp.where(hit, gate, 0.0), axis=1, keepdims=True)
        yg = (y * row_gate).astype(BF16)
        o_ref[...] += lax.dot_general(onehot, yg, (((0,), (0,)), ((), ())), preferred_element_type=F32)
        return carry

    lax.fori_loop(0, nchunk_ref[w * n_e + e], chunk, 0)

    @pl.when(e == n_e - 1)
    def _():
        o_ref[...] = _layer_norm(o_ref[...], g2_ref[...], b2_ref[...])


def _moe(x1, rank, gates, counts, w_gate, w_up, w_down, ws_gate, ws_up, ws_down, ln2_g, ln2_b):
    N, D = x1.shape
    n_e, _, d_e = w_gate.shape
    d_s = ws_down.shape[0]
    W = MOE_WINDOW
    n_win = N // W
    rows = EXPERT_ROWS
    nchunk = ((counts.reshape(n_win * n_e) + rows - 1) // rows).astype(I32)
    wgu = jnp.concatenate([w_gate, w_up], axis=-1).astype(BF16)
    wsgu = jnp.concatenate([ws_gate, ws_up], axis=-1).astype(BF16)
    kern = functools.partial(_moe_kernel, rows=rows, d_e=d_e, d_s=d_s)
    const = lambda shape: pl.BlockSpec(shape, lambda w, e, nc: (0,) * len(shape))
    return pl.pallas_call(
        kern,
        grid_spec=pltpu.PrefetchScalarGridSpec(
            num_scalar_prefetch=1,
            grid=(n_win, n_e),
            in_specs=[
                pl.BlockSpec((W, D), lambda w, e, nc: (w, 0)),
                pl.BlockSpec((n_e, W), lambda w, e, nc: (0, w)),
                pl.BlockSpec((n_e, W), lambda w, e, nc: (0, w)),
                pl.BlockSpec((1, D, 2 * d_e), lambda w, e, nc: (e, 0, 0)),
                pl.BlockSpec((1, d_e, D), lambda w, e, nc: (e, 0, 0)),
                const(wsgu.shape), const((d_s, D)), const((1, D)), const((1, D)),
            ],
            out_specs=pl.BlockSpec((W, D), lambda w, e, nc: (w, 0)),
            scratch_shapes=[pltpu.VMEM((W, D), BF16)],
        ),
        out_shape=jax.ShapeDtypeStruct((N, D), F32),
        compiler_params=pltpu.CompilerParams(
            dimension_semantics=("arbitrary", "arbitrary"), vmem_limit_bytes=VMEM_LIMIT),
        name="moe",
    )(nchunk, x1, rank, gates, wgu, w_down.astype(BF16), wsgu, ws_down.astype(BF16),
      ln2_g.reshape(1, D), ln2_b.reshape(1, D))


def kernel(x, positions, w_in, lambda_q1, lambda_k1, lambda_q2, lambda_k2, subln_g, conv_w, conv_b, conv_ln_g, conv_ln_b, w_out, ln1_g, ln1_b, w_router, router_bias, w_exp_gate, w_exp_up, w_exp_down, w_sh_gate, w_sh_up, w_sh_down, ln2_g, ln2_b):
    B, S, D = x.shape
    d_attn = N_HEADS * HEAD_DIM
    d_conv = conv_w.shape[-1]
    for l in range(DEPTH):
        lambda_init = 0.8 - 0.6 * math.exp(-0.3 * l)
        qT, k, vT, h = _proj(x, positions, w_in[l].astype(BF16), d_attn, d_conv)
        attn = _attention(qT, k, vT, lambda_q1[l], lambda_k1[l], lambda_q2[l], lambda_k2[l], subln_g[l],
                          lambda_init)
        x1, rank, gates, counts = _mix(h, attn, x, conv_w[l], conv_b[l], conv_ln_g[l], conv_ln_b[l],
                                       w_out[l].astype(BF16), ln1_g[l], ln1_b[l], w_router[l], router_bias[l])
        out = _moe(x1, rank, gates, counts, w_exp_gate[l], w_exp_up[l], w_exp_down[l],
                   w_sh_gate[l], w_sh_up[l], w_sh_down[l], ln2_g[l], ln2_b[l])
        x = out.reshape(B, S, D)
    return x
```

```python
import functools
import math

import jax
import jax.numpy as jnp
from jax import lax
from jax.experimental import pallas as pl
from jax.experimental.pallas import tpu as pltpu

F32 = jnp.float32
BF16 = jnp.bfloat16
I32 = jnp.int32
U32 = jnp.uint32

N_HEADS = 4
HEAD_DIM = 128
SUB_DIM = 64
ROPE_DIM = 16
ROPE_THETA = 500000.0
CHUNK = 64
CONV_WIDTH = 31
N_EXPERTS = 64
TOP_K = 8
N_GROUPS = 8
GROUP_SIZE = N_EXPERTS // N_GROUPS
TOPK_GROUPS = 4
ROUTED_SCALE = 2.5
LN_EPS = 1e-5
DEPTH = 1
ALPHA = (2.0 * DEPTH) ** 0.25
LOG2_E = math.log2(math.e)

LANES = 128
SUBLANES = 8
VMEM_LIMIT = 56 * 1024 * 1024

PROJ_ROWS = 512
ATTN_BLOCK = 256
MIX_ROWS = 256
CONV_HALO = 32
CONV_ROW_CHUNK = 64
MOE_WINDOW = 1024
EXPERT_ROWS = 128


def _dot(a, b):
    return jnp.dot(a, b, preferred_element_type=F32)


def _layer_norm(z, g, b):
    mu = jnp.mean(z, axis=-1, keepdims=True)
    zc = z - mu
    var = jnp.mean(zc * zc, axis=-1, keepdims=True)
    return zc * lax.rsqrt(var + LN_EPS) * g + b


def _proj_kernel(x_ref, pos_ref, invf_ref, w_ref, qT_ref, k_ref, vT_ref, h_ref, *, d_attn, d_conv, blk):
    xb = x_ref[...].astype(BF16)
    tm = xb.shape[0]
    ang = pos_ref[...].astype(F32) * invf_ref[...]
    cos = jnp.cos(ang)
    sin = jnp.sin(ang)
    d = lax.broadcasted_iota(I32, (1, LANES), 1) & (SUB_DIM - 1)
    half = ROPE_DIM // 2
    c_mul = jnp.where(d < ROPE_DIM, cos, 1.0)
    s_lo = jnp.where(d < half, -sin, 0.0)
    s_hi = jnp.where((d >= half) & (d < ROPE_DIM), sin, 0.0)

    def rope(t):
        outs = []
        for c in range(t.shape[1] // LANES):
            ts = t[:, c * LANES:(c + 1) * LANES]
            outs.append(ts * c_mul + pltpu.roll(ts, LANES - half, 1) * s_lo + pltpu.roll(ts, half, 1) * s_hi)
        return jnp.concatenate(outs, axis=1)

    q = rope(_dot(xb, w_ref[:, 0:d_attn])) * (SUB_DIM ** -0.5 * LOG2_E)
    for c in range(tm // blk):
        qT_ref[0, c] = q[c * blk:(c + 1) * blk, :].T.astype(BF16)
    k = rope(_dot(xb, w_ref[:, d_attn:2 * d_attn]))
    k_ref[0] = k.astype(BF16)
    v = _dot(xb, w_ref[:, 2 * d_attn:3 * d_attn])
    for c in range(tm // blk):
        vT_ref[0, c] = v[c * blk:(c + 1) * blk, :].T.astype(BF16)
    a = _dot(xb, w_ref[:, 3 * d_attn:3 * d_attn + d_conv])
    gate = _dot(xb, w_ref[:, 3 * d_attn + d_conv:3 * d_attn + 2 * d_conv])
    h_ref[0] = a * jax.nn.sigmoid(gate)


def _proj(x, positions, w_in_bf16, d_attn, d_conv):
    B, S, D = x.shape
    tm, blk = PROJ_ROWS, ATTN_BLOCK
    nb = S // blk
    inv_freq = ROPE_THETA ** (-jnp.arange(0, ROPE_DIM, 2, dtype=F32) / ROPE_DIM)
    half = ROPE_DIM // 2
    sub = jnp.concatenate([inv_freq, inv_freq, jnp.zeros((SUB_DIM - ROPE_DIM,), F32)])
    invf = jnp.tile(sub, LANES // SUB_DIM).reshape(1, LANES)
    d_in = w_in_bf16.shape[1]
    kern = functools.partial(_proj_kernel, d_attn=d_attn, d_conv=d_conv, blk=blk)
    return pl.pallas_call(
        kern,
        grid=(B, S // tm),
        in_specs=[
            pl.BlockSpec((None, tm, D), lambda b, i: (b, i, 0)),
            pl.BlockSpec((None, tm, 1), lambda b, i: (b, i, 0)),
            pl.BlockSpec((1, LANES), lambda b, i: (0, 0)),
            pl.BlockSpec((D, d_in), lambda b, i: (0, 0)),
        ],
        out_specs=[
            pl.BlockSpec((1, tm // blk, d_attn, blk), lambda b, i: (b, i, 0, 0)),
            pl.BlockSpec((1, tm, d_attn), lambda b, i: (b, i, 0)),
            pl.BlockSpec((1, tm // blk, d_attn, blk), lambda b, i: (b, i, 0, 0)),
            pl.BlockSpec((1, tm, d_conv), lambda b, i: (b, i, 0)),
        ],
        out_shape=[
            jax.ShapeDtypeStruct((B, nb, d_attn, blk), BF16),
            jax.ShapeDtypeStruct((B, S, d_attn), BF16),
            jax.ShapeDtypeStruct((B, nb, d_attn, blk), BF16),
            jax.ShapeDtypeStruct((B, S, d_conv), F32),
        ],
        compiler_params=pltpu.CompilerParams(
            dimension_semantics=("parallel", "parallel"), vmem_limit_bytes=VMEM_LIMIT),
        name="proj",
    )(x, positions.reshape(B, S, 1), invf, w_in_bf16)


def _attn_kernel(lq1_ref, lk1_ref, lq2_ref, lk2_ref, g_ref, qT_ref, k_ref, vT_ref, o_ref,
                 s_a, s_b, p_a, p_b, acc_ref, m_ref, *, blk, lambda_init):
    i = pl.program_id(1)
    lam = (jnp.exp(jnp.sum(lq1_ref[...] * lk1_ref[...], axis=-1, keepdims=True))
           - jnp.exp(jnp.sum(lq2_ref[...] * lk2_ref[...], axis=-1, keepdims=True)) + lambda_init)
    row = lax.broadcasted_iota(I32, (HEAD_DIM, 1), 0)
    ones = jnp.ones((2 * SUBLANES, blk), BF16)
    heads = range(N_HEADS)

    def head_cols(h):
        return slice(h * HEAD_DIM, (h + 1) * HEAD_DIM)

    def q_both(h):
        qT = qT_ref[0, 0, head_cols(h), :]
        zero = jnp.zeros_like(qT)
        return jnp.concatenate([jnp.where(row < SUB_DIM, qT, zero), jnp.where(row >= SUB_DIM, qT, zero)], axis=1)

    def scores_into(h, j, s_ref):
        kb = k_ref[0, pl.ds(pl.multiple_of(j * blk, blk), blk), head_cols(h)]
        s_ref[h] = _dot(kb, q_both(h))

    def weighted_values(h, j, p):
        return _dot(jnp.concatenate([vT_ref[0, j, head_cols(h), :], ones], axis=0), p)

    def probs(h, s):
        m = m_ref[h]
        m_new = jnp.maximum(m, jnp.max(s, axis=0, keepdims=True))
        m_ref[h] = m_new
        return jnp.exp2(m - m_new), jnp.exp2(s - m_new).astype(BF16)

    def step(j, s_cur, s_nxt, p_prev, p_cur):
        for h in heads:
            scores_into(h, j + 1, s_nxt)
            pv_prev = weighted_values(h, jnp.maximum(j - 1, 0), p_prev[h])
            alpha, p = probs(h, s_cur[h])
            p_cur[h] = p
            acc_ref[h] = alpha * (acc_ref[h] + pv_prev)

    def finish(s_cur, p_prev):
        kc = lax.broadcasted_iota(I32, (blk, 1), 0) // CHUNK
        qc = (lax.broadcasted_iota(I32, (1, 2 * blk), 1) & (blk - 1)) // CHUNK
        for h in heads:
            pv_prev = weighted_values(h, jnp.maximum(i - 1, 0), p_prev[h])
            alpha, p = probs(h, jnp.where(kc <= qc, s_cur[h], -jnp.inf))
            acc = alpha * (acc_ref[h] + pv_prev) + weighted_values(h, i, p)
            o = acc[:HEAD_DIM] / acc[HEAD_DIM:HEAD_DIM + 1]
            o = o[:, :blk] - lam * o[:, blk:]
            o = o * lax.rsqrt(jnp.mean(o * o, axis=0, keepdims=True) + LN_EPS)
            o = o * g_ref[...] * (1.0 - lambda_init)
            o_ref[0, :, head_cols(h)] = o.T.astype(o_ref.dtype)

    m_ref[...] = jnp.full(m_ref.shape, -jnp.inf, F32)
    acc_ref[...] = jnp.zeros(acc_ref.shape, F32)
    p_b[...] = jnp.zeros(p_b.shape, BF16)
    for h in heads:
        scores_into(h, 0, s_a)

    def pair(jj, c):
        step(2 * jj, s_a, s_b, p_b, p_a)
        step(2 * jj + 1, s_b, s_a, p_a, p_b)
        return c

    lax.fori_loop(0, i // 2, pair, 0)

    @pl.when(i % 2 == 1)
    def _():
        step(i - 1, s_a, s_b, p_b, p_a)
        finish(s_b, p_a)

    @pl.when(i % 2 == 0)
    def _():
        finish(s_a, p_b)


def _attention(qT, k, vT, lq1, lk1, lq2, lk2, sub_g, lambda_init):
    B, nb, d_attn, blk = qT.shape
    S = nb * blk
    kern = functools.partial(_attn_kernel, blk=blk, lambda_init=lambda_init)
    vec = pl.BlockSpec((1, SUB_DIM), lambda b, i: (0, 0))
    return pl.pallas_call(
        kern,
        grid=(B, nb),
        in_specs=[
            vec, vec, vec, vec,
            pl.BlockSpec((HEAD_DIM, 1), lambda b, i: (0, 0)),
            pl.BlockSpec((1, 1, d_attn, blk), lambda b, i: (b, i, 0, 0)),
            pl.BlockSpec((1, S, d_attn), lambda b, i: (b, 0, 0)),
            pl.BlockSpec((1, nb, d_attn, blk), lambda b, i: (b, 0, 0, 0)),
        ],
        out_specs=pl.BlockSpec((1, blk, d_attn), lambda b, i: (b, i, 0)),
        out_shape=jax.ShapeDtypeStruct((B, S, d_attn), BF16),
        scratch_shapes=[
            pltpu.VMEM((N_HEADS, blk, 2 * blk), F32), pltpu.VMEM((N_HEADS, blk, 2 * blk), F32),
            pltpu.VMEM((N_HEADS, blk, 2 * blk), BF16), pltpu.VMEM((N_HEADS, blk, 2 * blk), BF16),
            pltpu.VMEM((N_HEADS, HEAD_DIM + 2 * SUBLANES, 2 * blk), F32),
            pltpu.VMEM((N_HEADS, 1, 2 * blk), F32),
        ],
        compiler_params=pltpu.CompilerParams(
            dimension_semantics=("parallel", "arbitrary"), vmem_limit_bytes=VMEM_LIMIT),
        name="attn",
    )(lq1.reshape(1, SUB_DIM), lk1.reshape(1, SUB_DIM), lq2.reshape(1, SUB_DIM), lk2.reshape(1, SUB_DIM),
      sub_g.reshape(HEAD_DIM, 1), qT, k, vT)


def _route(scores, bias):
    n_e, tm = scores.shape
    neg = -jnp.inf
    biased = scores + bias
    g3 = biased.reshape(N_GROUPS, GROUP_SIZE, tm)
    io3 = lax.broadcasted_iota(I32, g3.shape, 1)
    m1 = jnp.max(g3, axis=1, keepdims=True)
    first = jnp.min(jnp.where(g3 == m1, io3, GROUP_SIZE), axis=1, keepdims=True)
    m2 = jnp.max(jnp.where(io3 == first, neg, g3), axis=1, keepdims=True)
    grp_score = (m1 + m2).reshape(N_GROUPS, tm)
    gio = lax.broadcasted_iota(I32, (N_GROUPS, tm), 0)
    grp_sel = jnp.zeros((N_GROUPS, tm), jnp.bool_)
    cur = grp_score
    for _ in range(TOPK_GROUPS):
        mx = jnp.max(cur, axis=0, keepdims=True)
        f = jnp.min(jnp.where(cur == mx, gio, N_GROUPS), axis=0, keepdims=True)
        hit = gio == f
        grp_sel = grp_sel | hit
        cur = jnp.where(hit, neg, cur)
    grp_f = jnp.where(grp_sel, 1.0, 0.0).astype(F32)
    exp_mask = jnp.broadcast_to(grp_f.reshape(N_GROUPS, 1, tm), (N_GROUPS, GROUP_SIZE, tm)).reshape(n_e, tm) > 0.5
    eio = lax.broadcasted_iota(I32, (n_e, tm), 0)
    cur = jnp.where(exp_mask, biased, neg)
    sel = jnp.zeros((n_e, tm), jnp.bool_)
    for _ in range(TOP_K):
        mx = jnp.max(cur, axis=0, keepdims=True)
        f = jnp.min(jnp.where(cur == mx, eio, n_e), axis=0, keepdims=True)
        hit = eio == f
        sel = sel | hit
        cur = jnp.where(hit, neg, cur)
    denom = jnp.sum(jnp.where(sel, scores, 0.0), axis=0, keepdims=True)
    gates = jnp.where(sel, scores / denom * ROUTED_SCALE, 0.0)
    return sel, gates


def _mix_kernel(h_ref, halo_ref, cw_ref, cb_ref, cg_ref, cbeta_ref, attn_ref, wo_ref, x_ref, g1_ref, b1_ref,
                wrh_ref, wrl_ref, rb_ref,
                x1_ref, rank_ref, gate_ref, cnt_ref,
                buf, shifted, cbuf, carry, *, tm, d_attn, tiles_per_window):
    tile = pl.program_id(0) * pl.num_programs(1) + pl.program_id(1)

    @pl.when(tile % tiles_per_window == 0)
    def _():
        carry[...] = jnp.zeros_like(carry)

    @pl.when(pl.program_id(1) == 0)
    def _():
        buf[0:CONV_HALO, :] = jnp.zeros((CONV_HALO, buf.shape[1]), F32)

    @pl.when(pl.program_id(1) != 0)
    def _():
        buf[0:CONV_HALO, :] = halo_ref[...]

    buf[CONV_HALO:, :] = h_ref[...]
    d_conv = buf.shape[1]
    for s in range(1, SUBLANES):
        shifted[s - 1] = buf[s:s + shifted.shape[1], :]
    off = CONV_HALO - (CONV_WIDTH - 1)
    for c in range(d_conv // LANES):
        cs = slice(c * LANES, (c + 1) * LANES)
        for r in range(tm // CONV_ROW_CHUNK):
            acc = jnp.zeros((CONV_ROW_CHUNK, LANES), F32)
            for j in range(CONV_WIDTH):
                s = (off + j) % SUBLANES
                start = r * CONV_ROW_CHUNK + off + j - s
                src = buf if s == 0 else shifted.at[s - 1]
                acc = acc + cw_ref[j:j + 1, cs] * src[start:start + CONV_ROW_CHUNK, cs]
            cbuf[r * CONV_ROW_CHUNK:(r + 1) * CONV_ROW_CHUNK, cs] = acc + cb_ref[:, cs]
    conv = _layer_norm(cbuf[...], cg_ref[...], cbeta_ref[...])
    conv = conv * jax.nn.sigmoid(conv)

    mix = _dot(attn_ref[...], wo_ref[0:d_attn, :]) + _dot(conv.astype(BF16), wo_ref[d_attn:, :])
    x1 = _layer_norm(ALPHA * x_ref[...] + mix, g1_ref[...], b1_ref[...])
    x1_ref[...] = x1

    xh = x1.astype(BF16)
    xl = (x1 - xh.astype(F32)).astype(BF16)
    logits = _dot(xh, wrh_ref[...]) + _dot(xl, wrh_ref[...]) + _dot(xh, wrl_ref[...])
    scores = jax.nn.sigmoid(logits).T
    sel, gates = _route(scores, rb_ref[...])
    gate_ref[...] = gates

    t_row = lax.broadcasted_iota(I32, (tm, tm), 0)
    t_col = lax.broadcasted_iota(I32, (tm, tm), 1)
    before = jnp.where(t_row < t_col, 1.0, 0.0).astype(BF16)
    sel_f = jnp.where(sel, 1.0, 0.0).astype(F32)
    rank = _dot(sel_f.astype(BF16), before) + carry[...]
    rank_ref[...] = jnp.where(sel, rank, -1.0).astype(I32)
    carry[...] = carry[...] + jnp.sum(sel_f, axis=1, keepdims=True)
    cnt_ref[...] = carry[...].astype(I32)


def _mix(h, attn, x, conv_w, conv_b, cln_g, cln_b, w_out_bf16, ln1_g, ln1_b, w_router, router_bias):
    B, S, D = x.shape
    d_conv = h.shape[-1]
    d_attn = attn.shape[-1]
    tm = MIX_ROWS
    nt = S // tm
    N = B * S
    n_e = w_router.shape[-1]
    wr_hi = w_router.astype(BF16)
    wr_lo = (w_router - wr_hi.astype(F32)).astype(BF16)
    halo_blocks = tm // CONV_HALO
    row = lambda a: a.reshape(1, -1)
    const = lambda shape: pl.BlockSpec(shape, lambda b, i: (0,) * len(shape))
    tiles_per_window = MOE_WINDOW // tm
    n_win = N // MOE_WINDOW
    kern = functools.partial(_mix_kernel, tm=tm, d_attn=d_attn, tiles_per_window=tiles_per_window)
    return pl.pallas_call(
        kern,
        grid=(B, nt),
        in_specs=[
            pl.BlockSpec((None, tm, d_conv), lambda b, i: (b, i, 0)),
            pl.BlockSpec((None, CONV_HALO, d_conv), lambda b, i: (b, jnp.maximum(i * halo_blocks - 1, 0), 0)),
            const(conv_w.shape), const((1, d_conv)), const((1, d_conv)), const((1, d_conv)),
            pl.BlockSpec((None, tm, d_attn), lambda b, i: (b, i, 0)),
            const(w_out_bf16.shape),
            pl.BlockSpec((None, tm, D), lambda b, i: (b, i, 0)),
            const((1, D)), const((1, D)),
            const(wr_hi.shape), const(wr_lo.shape), const((n_e, 1)),
        ],
        out_specs=[
            pl.BlockSpec((tm, D), lambda b, i: (b * nt + i, 0)),
            pl.BlockSpec((n_e, tm), lambda b, i: (0, b * nt + i)),
            pl.BlockSpec((n_e, tm), lambda b, i: (0, b * nt + i)),
            pl.BlockSpec((None, n_e, 1), lambda b, i: ((b * nt + i) // tiles_per_window, 0, 0)),
        ],
        out_shape=[
            jax.ShapeDtypeStruct((N, D), F32),
            jax.ShapeDtypeStruct((n_e, N), I32),
            jax.ShapeDtypeStruct((n_e, N), F32),
            jax.ShapeDtypeStruct((n_win, n_e, 1), I32),
        ],
        scratch_shapes=[
            pltpu.VMEM((tm + CONV_HALO, d_conv), F32),
            pltpu.VMEM((SUBLANES - 1, tm + CONV_HALO - SUBLANES, d_conv), F32),
            pltpu.VMEM((tm, d_conv), F32),
            pltpu.VMEM((n_e, 1), F32),
        ],
        compiler_params=pltpu.CompilerParams(
            dimension_semantics=("arbitrary", "arbitrary"), vmem_limit_bytes=VMEM_LIMIT),
        name="mix",
    )(h, h, conv_w, row(conv_b), row(cln_g), row(cln_b), attn, w_out_bf16, x, row(ln1_g), row(ln1_b),
      wr_hi, wr_lo, router_bias.reshape(n_e, 1))


def _moe_kernel(nchunk_ref, x1_ref, rank_ref, gate_ref, wgu_ref, wd_ref, wsgu_ref, wsd_ref, g2_ref, b2_ref,
                o_ref, xb_ref, oh_st, yg_st, staged, *, rows, d_e, d_s):
    w = pl.program_id(0)
    e = pl.program_id(1)
    n_e = pl.num_programs(1)

    @pl.when(e == 0)
    def _():
        x1 = x1_ref[...]
        xb = x1.astype(BF16)
        xb_ref[...] = xb
        gu = _dot(xb, wsgu_ref[...])
        g = gu[:, :d_s]
        shared = _dot((g * jax.nn.sigmoid(g) * gu[:, d_s:]).astype(BF16), wsd_ref[...])
        o_ref[...] = ALPHA * x1 + shared
        staged[0] = 0

    rank = rank_ref[pl.ds(e, 1), :]
    gate = gate_ref[pl.ds(e, 1), :]

    def scatter_staged():
        o_ref[...] += lax.dot_general(oh_st[...], yg_st[...], (((0,), (0,)), ((), ())),
                                      preferred_element_type=F32)

    def chunk(c, carry):
        hit = (lax.broadcasted_iota(I32, (rows, 1), 0) + c * rows) == rank
        onehot = jnp.where(hit, 1.0, 0.0).astype(BF16)
        xs = _dot(onehot, xb_ref[...]).astype(BF16)
        gu = _dot(xs, wgu_ref[0])
        g = gu[:, :d_e]
        y = _dot((g * jax.nn.sigmoid(g) * gu[:, d_e:]).astype(BF16), wd_ref[0])
        row_gate = jnp.sum(jnp.where(hit, gate, 0.0), axis=1, keepdims=True)
        slot = staged[0]
        at = pl.ds(pl.multiple_of(slot * rows, rows), rows)
        oh_st[at, :] = onehot
        yg_st[at, :] = (y * row_gate).astype(BF16)

        @pl.when(slot == 1)
        def _():
            scatter_staged()

        staged[0] = 1 - slot
        return carry

    lax.fori_loop(0, nchunk_ref[w * n_e + e], chunk, 0)

    @pl.when(e == n_e - 1)
    def _():
        @pl.when(staged[0] == 1)
        def _():
            oh_st[rows:, :] = jnp.zeros((rows, oh_st.shape[1]), BF16)
            yg_st[rows:, :] = jnp.zeros((rows, yg_st.shape[1]), BF16)
            scatter_staged()

        o_ref[...] = _layer_norm(o_ref[...], g2_ref[...], b2_ref[...])


def _moe(x1, rank, gates, counts, w_gate, w_up, w_down, ws_gate, ws_up, ws_down, ln2_g, ln2_b):
    N, D = x1.shape
    n_e, _, d_e = w_gate.shape
    d_s = ws_down.shape[0]
    W = MOE_WINDOW
    n_win = N // W
    rows = EXPERT_ROWS
    nchunk = ((counts.reshape(n_win * n_e) + rows - 1) // rows).astype(I32)
    wgu = jnp.concatenate([w_gate, w_up], axis=-1).astype(BF16)
    wsgu = jnp.concatenate([ws_gate, ws_up], axis=-1).astype(BF16)
    kern = functools.partial(_moe_kernel, rows=rows, d_e=d_e, d_s=d_s)
    const = lambda shape: pl.BlockSpec(shape, lambda w, e, nc: (0,) * len(shape))
    return pl.pallas_call(
        kern,
        grid_spec=pltpu.PrefetchScalarGridSpec(
            num_scalar_prefetch=1,
            grid=(n_win, n_e),
            in_specs=[
                pl.BlockSpec((W, D), lambda w, e, nc: (w, 0)),
                pl.BlockSpec((n_e, W), lambda w, e, nc: (0, w)),
                pl.BlockSpec((n_e, W), lambda w, e, nc: (0, w)),
                pl.BlockSpec((1, D, 2 * d_e), lambda w, e, nc: (e, 0, 0)),
                pl.BlockSpec((1, d_e, D), lambda w, e, nc: (e, 0, 0)),
                const(wsgu.shape), const((d_s, D)), const((1, D)), const((1, D)),
            ],
            out_specs=pl.BlockSpec((W, D), lambda w, e, nc: (w, 0)),
            scratch_shapes=[pltpu.VMEM((W, D), BF16), pltpu.VMEM((2 * rows, W), BF16),
                            pltpu.VMEM((2 * rows, D), BF16), pltpu.SMEM((1,), I32)],
        ),
        out_shape=jax.ShapeDtypeStruct((N, D), F32),
        compiler_params=pltpu.CompilerParams(
            dimension_semantics=("arbitrary", "arbitrary"), vmem_limit_bytes=VMEM_LIMIT),
        name="moe",
    )(nchunk, x1, rank, gates, wgu, w_down.astype(BF16), wsgu, ws_down.astype(BF16),
      ln2_g.reshape(1, D), ln2_b.reshape(1, D))


def kernel(x, positions, w_in, lambda_q1, lambda_k1, lambda_q2, lambda_k2, subln_g, conv_w, conv_b, conv_ln_g, conv_ln_b, w_out, ln1_g, ln1_b, w_router, router_bias, w_exp_gate, w_exp_up, w_exp_down, w_sh_gate, w_sh_up, w_sh_down, ln2_g, ln2_b):
    B, S, D = x.shape
    d_attn = N_HEADS * HEAD_DIM
    d_conv = conv_w.shape[-1]
    for l in range(DEPTH):
        lambda_init = 0.8 - 0.6 * math.exp(-0.3 * l)
        qT, k, vT, h = _proj(x, positions, w_in[l].astype(BF16), d_attn, d_conv)
        attn = _attention(qT, k, vT, lambda_q1[l], lambda_k1[l], lambda_q2[l], lambda_k2[l], subln_g[l],
                          lambda_init)
        x1, rank, gates, counts = _mix(h, attn, x, conv_w[l], conv_b[l], conv_ln_g[l], conv_ln_b[l],
                                       w_out[l].astype(BF16), ln1_g[l], ln1_b[l], w_router[l], router_bias[l])
        out = _moe(x1, rank, gates, counts, w_exp_gate[l], w_exp_up[l], w_exp_down[l],
                   w_sh_gate[l], w_sh_up[l], w_sh_down[l], ln2_g[l], ln2_b[l])
        x = out.reshape(B, S, D)
    return x
```

```python
import functools
import math

import jax
import jax.numpy as jnp
from jax import lax
from jax.experimental import pallas as pl
from jax.experimental.pallas import tpu as pltpu

F32 = jnp.float32
BF16 = jnp.bfloat16
I32 = jnp.int32
U32 = jnp.uint32

N_HEADS = 4
HEAD_DIM = 128
SUB_DIM = 64
ROPE_DIM = 16
ROPE_THETA = 500000.0
CHUNK = 64
CONV_WIDTH = 31
N_EXPERTS = 64
TOP_K = 8
N_GROUPS = 8
GROUP_SIZE = N_EXPERTS // N_GROUPS
TOPK_GROUPS = 4
ROUTED_SCALE = 2.5
LN_EPS = 1e-5
DEPTH = 1
ALPHA = (2.0 * DEPTH) ** 0.25
LOG2_E = math.log2(math.e)

LANES = 128
SUBLANES = 8
VMEM_LIMIT = 56 * 1024 * 1024

PROJ_ROWS = 512
ATTN_BLOCK = 256
MIX_ROWS = 256
CONV_HALO = 32
CONV_ROW_CHUNK = 64
MOE_WINDOW = 1024
EXPERT_ROWS = 128
MOE_EXPERTS_PER_STEP = 4


def _dot(a, b):
    return jnp.dot(a, b, preferred_element_type=F32)


def _layer_norm(z, g, b):
    mu = jnp.mean(z, axis=-1, keepdims=True)
    zc = z - mu
    var = jnp.mean(zc * zc, axis=-1, keepdims=True)
    return zc * lax.rsqrt(var + LN_EPS) * g + b


def _proj_kernel(x_ref, pos_ref, invf_ref, w_ref, qT_ref, k_ref, vT_ref, h_ref, *, d_attn, d_conv, blk):
    xb = x_ref[...].astype(BF16)
    tm = xb.shape[0]
    ang = pos_ref[...].astype(F32) * invf_ref[...]
    cos = jnp.cos(ang)
    sin = jnp.sin(ang)
    d = lax.broadcasted_iota(I32, (1, LANES), 1) & (SUB_DIM - 1)
    half = ROPE_DIM // 2
    c_mul = jnp.where(d < ROPE_DIM, cos, 1.0)
    s_lo = jnp.where(d < half, -sin, 0.0)
    s_hi = jnp.where((d >= half) & (d < ROPE_DIM), sin, 0.0)

    def rope(t):
        outs = []
        for c in range(t.shape[1] // LANES):
            ts = t[:, c * LANES:(c + 1) * LANES]
            outs.append(ts * c_mul + pltpu.roll(ts, LANES - half, 1) * s_lo + pltpu.roll(ts, half, 1) * s_hi)
        return jnp.concatenate(outs, axis=1)

    q = rope(_dot(xb, w_ref[:, 0:d_attn])) * (SUB_DIM ** -0.5 * LOG2_E)
    for c in range(tm // blk):
        qT_ref[0, c] = q[c * blk:(c + 1) * blk, :].T.astype(BF16)
    k = rope(_dot(xb, w_ref[:, d_attn:2 * d_attn]))
    k_ref[0] = k.astype(BF16)
    v = _dot(xb, w_ref[:, 2 * d_attn:3 * d_attn])
    for c in range(tm // blk):
        vT_ref[0, c] = v[c * blk:(c + 1) * blk, :].T.astype(BF16)
    a = _dot(xb, w_ref[:, 3 * d_attn:3 * d_attn + d_conv])
    gate = _dot(xb, w_ref[:, 3 * d_attn + d_conv:3 * d_attn + 2 * d_conv])
    h_ref[0] = a * jax.nn.sigmoid(gate)


def _proj(x, positions, w_in_bf16, d_attn, d_conv):
    B, S, D = x.shape
    tm, blk = PROJ_ROWS, ATTN_BLOCK
    nb = S // blk
    inv_freq = ROPE_THETA ** (-jnp.arange(0, ROPE_DIM, 2, dtype=F32) / ROPE_DIM)
    half = ROPE_DIM // 2
    sub = jnp.concatenate([inv_freq, inv_freq, jnp.zeros((SUB_DIM - ROPE_DIM,), F32)])
    invf = jnp.tile(sub, LANES // SUB_DIM).reshape(1, LANES)
    d_in = w_in_bf16.shape[1]
    kern = functools.partial(_proj_kernel, d_attn=d_attn, d_conv=d_conv, blk=blk)
    return pl.pallas_call(
        kern,
        grid=(B, S // tm),
        in_specs=[
            pl.BlockSpec((None, tm, D), lambda b, i: (b, i, 0)),
            pl.BlockSpec((None, tm, 1), lambda b, i: (b, i, 0)),
            pl.BlockSpec((1, LANES), lambda b, i: (0, 0)),
            pl.BlockSpec((D, d_in), lambda b, i: (0, 0)),
        ],
        out_specs=[
            pl.BlockSpec((1, tm // blk, d_attn, blk), lambda b, i: (b, i, 0, 0)),
            pl.BlockSpec((1, tm, d_attn), lambda b, i: (b, i, 0)),
            pl.BlockSpec((1, tm // blk, d_attn, blk), lambda b, i: (b, i, 0, 0)),
            pl.BlockSpec((1, tm, d_conv), lambda b, i: (b, i, 0)),
        ],
        out_shape=[
            jax.ShapeDtypeStruct((B, nb, d_attn, blk), BF16),
            jax.ShapeDtypeStruct((B, S, d_attn), BF16),
            jax.ShapeDtypeStruct((B, nb, d_attn, blk), BF16),
            jax.ShapeDtypeStruct((B, S, d_conv), F32),
        ],
        compiler_params=pltpu.CompilerParams(
            dimension_semantics=("parallel", "parallel"), vmem_limit_bytes=VMEM_LIMIT),
        name="proj",
    )(x, positions.reshape(B, S, 1), invf, w_in_bf16)


def _attn_kernel(lq1_ref, lk1_ref, lq2_ref, lk2_ref, g_ref, qT_ref, k_ref, vT_ref, o_ref,
                 s_a, s_b, p_a, p_b, acc_ref, m_ref, *, blk, lambda_init):
    i = pl.program_id(1)
    lam = (jnp.exp(jnp.sum(lq1_ref[...] * lk1_ref[...], axis=-1, keepdims=True))
           - jnp.exp(jnp.sum(lq2_ref[...] * lk2_ref[...], axis=-1, keepdims=True)) + lambda_init)
    row = lax.broadcasted_iota(I32, (HEAD_DIM, 1), 0)
    ones = jnp.ones((2 * SUBLANES, blk), BF16)
    heads = range(N_HEADS)

    def head_cols(h):
        return slice(h * HEAD_DIM, (h + 1) * HEAD_DIM)

    def q_both(h):
        qT = qT_ref[0, 0, head_cols(h), :]
        zero = jnp.zeros_like(qT)
        return jnp.concatenate([jnp.where(row < SUB_DIM, qT, zero), jnp.where(row >= SUB_DIM, qT, zero)], axis=1)

    def scores_into(h, j, s_ref):
        kb = k_ref[0, pl.ds(pl.multiple_of(j * blk, blk), blk), head_cols(h)]
        s_ref[h] = _dot(kb, q_both(h))

    def weighted_values(h, j, p):
        return _dot(jnp.concatenate([vT_ref[0, j, head_cols(h), :], ones], axis=0), p)

    def probs(h, s):
        m = m_ref[h]
        m_new = jnp.maximum(m, jnp.max(s, axis=0, keepdims=True))
        m_ref[h] = m_new
        return jnp.exp2(m - m_new), jnp.exp2(s - m_new).astype(BF16)

    def step(j, s_cur, s_nxt, p_prev, p_cur):
        for h in heads:
            scores_into(h, j + 1, s_nxt)
            pv_prev = weighted_values(h, jnp.maximum(j - 1, 0), p_prev[h])
            alpha, p = probs(h, s_cur[h])
            p_cur[h] = p
            acc_ref[h] = alpha * (acc_ref[h] + pv_prev)

    def finish(s_cur, p_prev):
        kc = lax.broadcasted_iota(I32, (blk, 1), 0) // CHUNK
        qc = (lax.broadcasted_iota(I32, (1, 2 * blk), 1) & (blk - 1)) // CHUNK
        for h in heads:
            pv_prev = weighted_values(h, jnp.maximum(i - 1, 0), p_prev[h])
            alpha, p = probs(h, jnp.where(kc <= qc, s_cur[h], -jnp.inf))
            acc = alpha * (acc_ref[h] + pv_prev) + weighted_values(h, i, p)
            o = acc[:HEAD_DIM] / acc[HEAD_DIM:HEAD_DIM + 1]
            o = o[:, :blk] - lam * o[:, blk:]
            o = o * lax.rsqrt(jnp.mean(o * o, axis=0, keepdims=True) + LN_EPS)
            o = o * g_ref[...] * (1.0 - lambda_init)
            o_ref[0, :, head_cols(h)] = o.T.astype(o_ref.dtype)

    m_ref[...] = jnp.full(m_ref.shape, -jnp.inf, F32)
    acc_ref[...] = jnp.zeros(acc_ref.shape, F32)
    p_b[...] = jnp.zeros(p_b.shape, BF16)
    for h in heads:
        scores_into(h, 0, s_a)

    def pair(jj, c):
        step(2 * jj, s_a, s_b, p_b, p_a)
        step(2 * jj + 1, s_b, s_a, p_a, p_b)
        return c

    lax.fori_loop(0, i // 2, pair, 0)

    @pl.when(i % 2 == 1)
    def _():
        step(i - 1, s_a, s_b, p_b, p_a)
        finish(s_b, p_a)

    @pl.when(i % 2 == 0)
    def _():
        finish(s_a, p_b)


def _attention(qT, k, vT, lq1, lk1, lq2, lk2, sub_g, lambda_init):
    B, nb, d_attn, blk = qT.shape
    S = nb * blk
    kern = functools.partial(_attn_kernel, blk=blk, lambda_init=lambda_init)
    vec = pl.BlockSpec((1, SUB_DIM), lambda b, i: (0, 0))
    return pl.pallas_call(
        kern,
        grid=(B, nb),
        in_specs=[
            vec, vec, vec, vec,
            pl.BlockSpec((HEAD_DIM, 1), lambda b, i: (0, 0)),
            pl.BlockSpec((1, 1, d_attn, blk), lambda b, i: (b, i, 0, 0)),
            pl.BlockSpec((1, S, d_attn), lambda b, i: (b, 0, 0)),
            pl.BlockSpec((1, nb, d_attn, blk), lambda b, i: (b, 0, 0, 0)),
        ],
        out_specs=pl.BlockSpec((1, blk, d_attn), lambda b, i: (b, i, 0)),
        out_shape=jax.ShapeDtypeStruct((B, S, d_attn), BF16),
        scratch_shapes=[
            pltpu.VMEM((N_HEADS, blk, 2 * blk), F32), pltpu.VMEM((N_HEADS, blk, 2 * blk), F32),
            pltpu.VMEM((N_HEADS, blk, 2 * blk), BF16), pltpu.VMEM((N_HEADS, blk, 2 * blk), BF16),
            pltpu.VMEM((N_HEADS, HEAD_DIM + 2 * SUBLANES, 2 * blk), F32),
            pltpu.VMEM((N_HEADS, 1, 2 * blk), F32),
        ],
        compiler_params=pltpu.CompilerParams(
            dimension_semantics=("parallel", "arbitrary"), vmem_limit_bytes=VMEM_LIMIT),
        name="attn",
    )(lq1.reshape(1, SUB_DIM), lk1.reshape(1, SUB_DIM), lq2.reshape(1, SUB_DIM), lk2.reshape(1, SUB_DIM),
      sub_g.reshape(HEAD_DIM, 1), qT, k, vT)


def _route(scores, bias):
    n_e, tm = scores.shape
    neg = -jnp.inf
    biased = scores + bias
    g3 = biased.reshape(N_GROUPS, GROUP_SIZE, tm)
    io3 = lax.broadcasted_iota(I32, g3.shape, 1)
    m1 = jnp.max(g3, axis=1, keepdims=True)
    first = jnp.min(jnp.where(g3 == m1, io3, GROUP_SIZE), axis=1, keepdims=True)
    m2 = jnp.max(jnp.where(io3 == first, neg, g3), axis=1, keepdims=True)
    grp_score = (m1 + m2).reshape(N_GROUPS, tm)
    gio = lax.broadcasted_iota(I32, (N_GROUPS, tm), 0)
    grp_sel = jnp.zeros((N_GROUPS, tm), jnp.bool_)
    cur = grp_score
    for _ in range(TOPK_GROUPS):
        mx = jnp.max(cur, axis=0, keepdims=True)
        f = jnp.min(jnp.where(cur == mx, gio, N_GROUPS), axis=0, keepdims=True)
        hit = gio == f
        grp_sel = grp_sel | hit
        cur = jnp.where(hit, neg, cur)
    grp_f = jnp.where(grp_sel, 1.0, 0.0).astype(F32)
    exp_mask = jnp.broadcast_to(grp_f.reshape(N_GROUPS, 1, tm), (N_GROUPS, GROUP_SIZE, tm)).reshape(n_e, tm) > 0.5
    eio = lax.broadcasted_iota(I32, (n_e, tm), 0)
    cur = jnp.where(exp_mask, biased, neg)
    sel = jnp.zeros((n_e, tm), jnp.bool_)
    for _ in range(TOP_K):
        mx = jnp.max(cur, axis=0, keepdims=True)
        f = jnp.min(jnp.where(cur == mx, eio, n_e), axis=0, keepdims=True)
        hit = eio == f
        sel = sel | hit
        cur = jnp.where(hit, neg, cur)
    denom = jnp.sum(jnp.where(sel, scores, 0.0), axis=0, keepdims=True)
    gates = jnp.where(sel, scores / denom * ROUTED_SCALE, 0.0)
    return sel, gates


def _mix_kernel(h_ref, halo_ref, cw_ref, cb_ref, cg_ref, cbeta_ref, attn_ref, wo_ref, x_ref, g1_ref, b1_ref,
                wrh_ref, wrl_ref, rb_ref,
                x1_ref, rank_ref, gate_ref, cnt_ref,
                buf, shifted, cbuf, carry, *, tm, d_attn, tiles_per_window):
    tile = pl.program_id(0) * pl.num_programs(1) + pl.program_id(1)

    @pl.when(tile % tiles_per_window == 0)
    def _():
        carry[...] = jnp.zeros_like(carry)

    @pl.when(pl.program_id(1) == 0)
    def _():
        buf[0:CONV_HALO, :] = jnp.zeros((CONV_HALO, buf.shape[1]), F32)

    @pl.when(pl.program_id(1) != 0)
    def _():
        buf[0:CONV_HALO, :] = halo_ref[...]

    buf[CONV_HALO:, :] = h_ref[...]
    d_conv = buf.shape[1]
    for s in range(1, SUBLANES):
        shifted[s - 1] = buf[s:s + shifted.shape[1], :]
    off = CONV_HALO - (CONV_WIDTH - 1)
    for c in range(d_conv // LANES):
        cs = slice(c * LANES, (c + 1) * LANES)
        for r in range(tm // CONV_ROW_CHUNK):
            acc = jnp.zeros((CONV_ROW_CHUNK, LANES), F32)
            for j in range(CONV_WIDTH):
                s = (off + j) % SUBLANES
                start = r * CONV_ROW_CHUNK + off + j - s
                src = buf if s == 0 else shifted.at[s - 1]
                acc = acc + cw_ref[j:j + 1, cs] * src[start:start + CONV_ROW_CHUNK, cs]
            cbuf[r * CONV_ROW_CHUNK:(r + 1) * CONV_ROW_CHUNK, cs] = acc + cb_ref[:, cs]
    conv = _layer_norm(cbuf[...], cg_ref[...], cbeta_ref[...])
    conv = conv * jax.nn.sigmoid(conv)

    mix = _dot(attn_ref[...], wo_ref[0:d_attn, :]) + _dot(conv.astype(BF16), wo_ref[d_attn:, :])
    x1 = _layer_norm(ALPHA * x_ref[...] + mix, g1_ref[...], b1_ref[...])
    x1_ref[...] = x1

    xh = x1.astype(BF16)
    xl = (x1 - xh.astype(F32)).astype(BF16)
    logits = _dot(xh, wrh_ref[...]) + _dot(xl, wrh_ref[...]) + _dot(xh, wrl_ref[...])
    scores = jax.nn.sigmoid(logits).T
    sel, gates = _route(scores, rb_ref[...])
    gate_ref[...] = gates

    t_row = lax.broadcasted_iota(I32, (tm, tm), 0)
    t_col = lax.broadcasted_iota(I32, (tm, tm), 1)
    before = jnp.where(t_row < t_col, 1.0, 0.0).astype(BF16)
    sel_f = jnp.where(sel, 1.0, 0.0).astype(F32)
    rank = _dot(sel_f.astype(BF16), before) + carry[...]
    rank_ref[...] = jnp.where(sel, rank, -1.0).astype(I32)
    carry[...] = carry[...] + jnp.sum(sel_f, axis=1, keepdims=True)
    cnt_ref[...] = carry[...].astype(I32)


def _mix(h, attn, x, conv_w, conv_b, cln_g, cln_b, w_out_bf16, ln1_g, ln1_b, w_router, router_bias):
    B, S, D = x.shape
    d_conv = h.shape[-1]
    d_attn = attn.shape[-1]
    tm = MIX_ROWS
    nt = S // tm
    N = B * S
    n_e = w_router.shape[-1]
    wr_hi = w_router.astype(BF16)
    wr_lo = (w_router - wr_hi.astype(F32)).astype(BF16)
    halo_blocks = tm // CONV_HALO
    row = lambda a: a.reshape(1, -1)
    const = lambda shape: pl.BlockSpec(shape, lambda b, i: (0,) * len(shape))
    tiles_per_window = MOE_WINDOW // tm
    kern = functools.partial(_mix_kernel, tm=tm, d_attn=d_attn, tiles_per_window=tiles_per_window)
    return pl.pallas_call(
        kern,
        grid=(B, nt),
        in_specs=[
            pl.BlockSpec((None, tm, d_conv), lambda b, i: (b, i, 0)),
            pl.BlockSpec((None, CONV_HALO, d_conv), lambda b, i: (b, jnp.maximum(i * halo_blocks - 1, 0), 0)),
            const(conv_w.shape), const((1, d_conv)), const((1, d_conv)), const((1, d_conv)),
            pl.BlockSpec((None, tm, d_attn), lambda b, i: (b, i, 0)),
            const(w_out_bf16.shape),
            pl.BlockSpec((None, tm, D), lambda b, i: (b, i, 0)),
            const((1, D)), const((1, D)),
            const(wr_hi.shape), const(wr_lo.shape), const((n_e, 1)),
        ],
        out_specs=[
            pl.BlockSpec((tm, D), lambda b, i: (b * nt + i, 0)),
            pl.BlockSpec((n_e, tm), lambda b, i: (0, b * nt + i)),
            pl.BlockSpec((n_e, tm), lambda b, i: (0, b * nt + i)),
            pl.BlockSpec((None, n_e, 1), lambda b, i: (b * nt + i, 0, 0)),
        ],
        out_shape=[
            jax.ShapeDtypeStruct((N, D), F32),
            jax.ShapeDtypeStruct((n_e, N), I32),
            jax.ShapeDtypeStruct((n_e, N), F32),
            jax.ShapeDtypeStruct((B * nt, n_e, 1), I32),
        ],
        scratch_shapes=[
            pltpu.VMEM((tm + CONV_HALO, d_conv), F32),
            pltpu.VMEM((SUBLANES - 1, tm + CONV_HALO - SUBLANES, d_conv), F32),
            pltpu.VMEM((tm, d_conv), F32),
            pltpu.VMEM((n_e, 1), F32),
        ],
        compiler_params=pltpu.CompilerParams(
            dimension_semantics=("arbitrary", "arbitrary"), vmem_limit_bytes=VMEM_LIMIT),
        name="mix",
    )(h, h, conv_w, row(conv_b), row(cln_g), row(cln_b), attn, w_out_bf16, x, row(ln1_g), row(ln1_b),
      wr_hi, wr_lo, router_bias.reshape(n_e, 1))


def _moe_kernel(nchunk_ref, tail_ok_ref, x1_ref, rank_ref, gate_ref, wgu_ref, wd_ref, wsgu_ref, wsd_ref,
                g2_ref, b2_ref, o_ref, xb_ref, *, rows, d_e, d_s, n_e, per_step, tail):
    w = pl.program_id(0)
    step = pl.program_id(1)

    @pl.when(step == 0)
    def _():
        x1 = x1_ref[...]
        xb = x1.astype(BF16)
        xb_ref[...] = xb
        gu = _dot(xb, wsgu_ref[...])
        g = gu[:, :d_s]
        shared = _dot((g * jax.nn.sigmoid(g) * gu[:, d_s:]).astype(BF16), wsd_ref[...])
        o_ref[...] = ALPHA * x1 + shared

    n_tok = xb_ref.shape[0]

    def expert_chunk(local, c, lo):
        e = step * per_step + local
        rank = rank_ref[pl.ds(e, 1), lo:]
        gate = gate_ref[pl.ds(e, 1), lo:]
        hit = (lax.broadcasted_iota(I32, (rows, 1), 0) + c * rows) == rank
        onehot = jnp.where(hit, 1.0, 0.0).astype(BF16)
        xs = _dot(onehot, xb_ref[lo:, :]).astype(BF16)
        gu = _dot(xs, wgu_ref[local])
        g = gu[:, :d_e]
        y = _dot((g * jax.nn.sigmoid(g) * gu[:, d_e:]).astype(BF16), wd_ref[local])
        row_gate = jnp.sum(jnp.where(hit, gate, 0.0), axis=1, keepdims=True)
        return onehot, (y * row_gate).astype(BF16)

    for first in range(0, per_step, 2):
        base = w * n_e + step * per_step + first

        def trip(c, carry, first=first, lo=0):
            oh_a, yg_a = expert_chunk(first, c, lo)
            oh_b, yg_b = expert_chunk(first + 1, c, lo)
            o_ref[lo:, :] += lax.dot_general(jnp.concatenate([oh_a, oh_b], axis=0),
                                             jnp.concatenate([yg_a, yg_b], axis=0),
                                             (((0,), (0,)), ((), ())), preferred_element_type=F32)
            return carry

        n_trips = jnp.maximum(nchunk_ref[base], nchunk_ref[base + 1])
        tail_ok = jnp.minimum(tail_ok_ref[base], tail_ok_ref[base + 1]) == 1
        n_full = jnp.where(tail_ok, jnp.minimum(n_trips, 1), n_trips)
        lax.fori_loop(0, n_full, trip, 0)
        lax.fori_loop(n_full, n_trips, functools.partial(trip, lo=n_tok - tail), 0)

    @pl.when(step == pl.num_programs(1) - 1)
    def _():
        o_ref[...] = _layer_norm(o_ref[...], g2_ref[...], b2_ref[...])


def _moe(x1, rank, gates, counts, w_gate, w_up, w_down, ws_gate, ws_up, ws_down, ln2_g, ln2_b):
    N, D = x1.shape
    n_e, _, d_e = w_gate.shape
    d_s = ws_down.shape[0]
    W = MOE_WINDOW
    n_win = N // W
    rows = EXPERT_ROWS
    tail = MIX_ROWS
    tiles = W // tail
    cnt = counts.reshape(n_win, tiles, n_e)
    nchunk = ((cnt[:, -1] + rows - 1) // rows).astype(I32).reshape(n_win * n_e)
    tail_ok = (cnt[:, -2] <= rows).astype(I32).reshape(n_win * n_e)
    wgu = jnp.concatenate([w_gate, w_up], axis=-1).astype(BF16)
    wsgu = jnp.concatenate([ws_gate, ws_up], axis=-1).astype(BF16)
    per_step = MOE_EXPERTS_PER_STEP
    kern = functools.partial(_moe_kernel, rows=rows, d_e=d_e, d_s=d_s, n_e=n_e, per_step=per_step, tail=tail)
    const = lambda shape: pl.BlockSpec(shape, lambda w, e, nc, tk: (0,) * len(shape))
    return pl.pallas_call(
        kern,
        grid_spec=pltpu.PrefetchScalarGridSpec(
            num_scalar_prefetch=2,
            grid=(n_win, n_e // per_step),
            in_specs=[
                pl.BlockSpec((W, D), lambda w, e, nc, tk: (w, 0)),
                pl.BlockSpec((n_e, W), lambda w, e, nc, tk: (0, w)),
                pl.BlockSpec((n_e, W), lambda w, e, nc, tk: (0, w)),
                pl.BlockSpec((per_step, D, 2 * d_e), lambda w, e, nc, tk: (e, 0, 0)),
                pl.BlockSpec((per_step, d_e, D), lambda w, e, nc, tk: (e, 0, 0)),
                const(wsgu.shape), const((d_s, D)), const((1, D)), const((1, D)),
            ],
            out_specs=pl.BlockSpec((W, D), lambda w, e, nc, tk: (w, 0)),
            scratch_shapes=[pltpu.VMEM((W, D), BF16)],
        ),
        out_shape=jax.ShapeDtypeStruct((N, D), F32),
        compiler_params=pltpu.CompilerParams(
            dimension_semantics=("arbitrary", "arbitrary"), vmem_limit_bytes=VMEM_LIMIT),
        name="moe",
    )(nchunk, tail_ok, x1, rank, gates, wgu, w_down.astype(BF16), wsgu, ws_down.astype(BF16),
      ln2_g.reshape(1, D), ln2_b.reshape(1, D))


def kernel(x, positions, w_in, lambda_q1, lambda_k1, lambda_q2, lambda_k2, subln_g, conv_w, conv_b, conv_ln_g, conv_ln_b, w_out, ln1_g, ln1_b, w_router, router_bias, w_exp_gate, w_exp_up, w_exp_down, w_sh_gate, w_sh_up, w_sh_down, ln2_g, ln2_b):
    B, S, D = x.shape
    d_attn = N_HEADS * HEAD_DIM
    d_conv = conv_w.shape[-1]
    for l in range(DEPTH):
        lambda_init = 0.8 - 0.6 * math.exp(-0.3 * l)
        qT, k, vT, h = _proj(x, positions, w_in[l].astype(BF16), d_attn, d_conv)
        attn = _attention(qT, k, vT, lambda_q1[l], lambda_k1[l], lambda_q2[l], lambda_k2[l], subln_g[l],
                          lambda_init)
        x1, rank, gates, counts = _mix(h, attn, x, conv_w[l], conv_b[l], conv_ln_g[l], conv_ln_b[l],
                                       w_out[l].astype(BF16), ln1_g[l], ln1_b[l], w_router[l], router_bias[l])
        out = _moe(x1, rank, gates, counts, w_exp_gate[l], w_exp_up[l], w_exp_down[l],
                   w_sh_gate[l], w_sh_up[l], w_sh_down[l], ln2_g[l], ln2_b[l])
        x = out.reshape(B, S, D)
    return x
```

```python
import functools
import math

import jax
import jax.numpy as jnp
from jax import lax
from jax.experimental import pallas as pl
from jax.experimental.pallas import tpu as pltpu

F32 = jnp.float32
BF16 = jnp.bfloat16
I32 = jnp.int32
U32 = jnp.uint32

N_HEADS = 4
HEAD_DIM = 128
SUB_DIM = 64
ROPE_DIM = 16
ROPE_THETA = 500000.0
CHUNK = 64
CONV_WIDTH = 31
N_EXPERTS = 64
TOP_K = 8
N_GROUPS = 8
GROUP_SIZE = N_EXPERTS // N_GROUPS
TOPK_GROUPS = 4
ROUTED_SCALE = 2.5
LN_EPS = 1e-5
DEPTH = 1
ALPHA = (2.0 * DEPTH) ** 0.25
LOG2_E = math.log2(math.e)

LANES = 128
SUBLANES = 8
VMEM_LIMIT = 56 * 1024 * 1024
MXU_DEPTH = 256

PROJ_ROWS = 512
ATTN_BLOCK = 256
MIX_ROWS = 256
CONV_HALO = 32
CONV_ROW_CHUNK = 64
MOE_WINDOW = 1024
MOE_SPAN = 256
EXPERT_ROWS = 48
MOE_EXPERTS_PER_STEP = 4


def _dot(a, b):
    return jnp.dot(a, b, preferred_element_type=F32)


def _layer_norm(z, g, b):
    mu = jnp.mean(z, axis=-1, keepdims=True)
    zc = z - mu
    var = jnp.mean(zc * zc, axis=-1, keepdims=True)
    return zc * lax.rsqrt(var + LN_EPS) * g + b


def _proj_kernel(x_ref, pos_ref, invf_ref, w_ref, qT_ref, k_ref, vT_ref, h_ref, *, d_attn, d_conv, blk):
    xb = x_ref[...].astype(BF16)
    tm = xb.shape[0]
    ang = pos_ref[...].astype(F32) * invf_ref[...]
    cos = jnp.cos(ang)
    sin = jnp.sin(ang)
    d = lax.broadcasted_iota(I32, (1, LANES), 1) & (SUB_DIM - 1)
    half = ROPE_DIM // 2
    c_mul = jnp.where(d < ROPE_DIM, cos, 1.0)
    s_lo = jnp.where(d < half, -sin, 0.0)
    s_hi = jnp.where((d >= half) & (d < ROPE_DIM), sin, 0.0)

    def rope(t):
        outs = []
        for c in range(t.shape[1] // LANES):
            ts = t[:, c * LANES:(c + 1) * LANES]
            outs.append(ts * c_mul + pltpu.roll(ts, LANES - half, 1) * s_lo + pltpu.roll(ts, half, 1) * s_hi)
        return jnp.concatenate(outs, axis=1)

    q = rope(_dot(xb, w_ref[:, 0:d_attn])) * (SUB_DIM ** -0.5 * LOG2_E)
    for c in range(tm // blk):
        qT_ref[0, c] = q[c * blk:(c + 1) * blk, :].T.astype(BF16)
    k = rope(_dot(xb, w_ref[:, d_attn:2 * d_attn]))
    k_ref[0] = k.astype(BF16)
    v = _dot(xb, w_ref[:, 2 * d_attn:3 * d_attn])
    for c in range(tm // blk):
        vT_ref[0, c] = v[c * blk:(c + 1) * blk, :].T.astype(BF16)
    a = _dot(xb, w_ref[:, 3 * d_attn:3 * d_attn + d_conv])
    gate = _dot(xb, w_ref[:, 3 * d_attn + d_conv:3 * d_attn + 2 * d_conv])
    h_ref[0] = a * jax.nn.sigmoid(gate)


def _proj(x, positions, w_in_bf16, d_attn, d_conv):
    B, S, D = x.shape
    tm, blk = PROJ_ROWS, ATTN_BLOCK
    nb = S // blk
    inv_freq = ROPE_THETA ** (-jnp.arange(0, ROPE_DIM, 2, dtype=F32) / ROPE_DIM)
    half = ROPE_DIM // 2
    sub = jnp.concatenate([inv_freq, inv_freq, jnp.zeros((SUB_DIM - ROPE_DIM,), F32)])
    invf = jnp.tile(sub, LANES // SUB_DIM).reshape(1, LANES)
    d_in = w_in_bf16.shape[1]
    kern = functools.partial(_proj_kernel, d_attn=d_attn, d_conv=d_conv, blk=blk)
    return pl.pallas_call(
        kern,
        grid=(B, S // tm),
        in_specs=[
            pl.BlockSpec((None, tm, D), lambda b, i: (b, i, 0)),
            pl.BlockSpec((None, tm, 1), lambda b, i: (b, i, 0)),
            pl.BlockSpec((1, LANES), lambda b, i: (0, 0)),
            pl.BlockSpec((D, d_in), lambda b, i: (0, 0)),
        ],
        out_specs=[
            pl.BlockSpec((1, tm // blk, d_attn, blk), lambda b, i: (b, i, 0, 0)),
            pl.BlockSpec((1, tm, d_attn), lambda b, i: (b, i, 0)),
            pl.BlockSpec((1, tm // blk, d_attn, blk), lambda b, i: (b, i, 0, 0)),
            pl.BlockSpec((1, tm, d_conv), lambda b, i: (b, i, 0)),
        ],
        out_shape=[
            jax.ShapeDtypeStruct((B, nb, d_attn, blk), BF16),
            jax.ShapeDtypeStruct((B, S, d_attn), BF16),
            jax.ShapeDtypeStruct((B, nb, d_attn, blk), BF16),
            jax.ShapeDtypeStruct((B, S, d_conv), F32),
        ],
        compiler_params=pltpu.CompilerParams(
            dimension_semantics=("parallel", "parallel"), vmem_limit_bytes=VMEM_LIMIT),
        name="proj",
    )(x, positions.reshape(B, S, 1), invf, w_in_bf16)


def _attn_kernel(lq1_ref, lk1_ref, lq2_ref, lk2_ref, g_ref, qT_ref, k_ref, vT_ref, o_ref,
                 s_a, s_b, p_a, p_b, acc_ref, m_ref, *, blk, lambda_init):
    i = pl.program_id(1)
    lam = (jnp.exp(jnp.sum(lq1_ref[...] * lk1_ref[...], axis=-1, keepdims=True))
           - jnp.exp(jnp.sum(lq2_ref[...] * lk2_ref[...], axis=-1, keepdims=True)) + lambda_init)
    row = lax.broadcasted_iota(I32, (HEAD_DIM, 1), 0)
    ones = jnp.ones((2 * SUBLANES, blk), BF16)
    heads = range(N_HEADS)

    def head_cols(h):
        return slice(h * HEAD_DIM, (h + 1) * HEAD_DIM)

    def q_both(h):
        qT = qT_ref[0, 0, head_cols(h), :]
        zero = jnp.zeros_like(qT)
        return jnp.concatenate([jnp.where(row < SUB_DIM, qT, zero), jnp.where(row >= SUB_DIM, qT, zero)], axis=1)

    def scores_into(h, j, s_ref):
        kb = k_ref[0, pl.ds(pl.multiple_of(j * blk, blk), blk), head_cols(h)]
        s_ref[h] = _dot(kb, q_both(h))

    def weighted_values(h, j, p):
        return _dot(jnp.concatenate([vT_ref[0, j, head_cols(h), :], ones], axis=0), p)

    def probs(h, s):
        m = m_ref[h]
        m_new = jnp.maximum(m, jnp.max(s, axis=0, keepdims=True))
        m_ref[h] = m_new
        return jnp.exp2(m - m_new), jnp.exp2(s - m_new).astype(BF16)

    def step(j, s_cur, s_nxt, p_prev, p_cur):
        for h in heads:
            scores_into(h, j + 1, s_nxt)
            pv_prev = weighted_values(h, jnp.maximum(j - 1, 0), p_prev[h])
            alpha, p = probs(h, s_cur[h])
            p_cur[h] = p
            acc_ref[h] = alpha * (acc_ref[h] + pv_prev)

    def finish(s_cur, p_prev):
        kc = lax.broadcasted_iota(I32, (blk, 1), 0) // CHUNK
        qc = (lax.broadcasted_iota(I32, (1, 2 * blk), 1) & (blk - 1)) // CHUNK
        for h in heads:
            pv_prev = weighted_values(h, jnp.maximum(i - 1, 0), p_prev[h])
            alpha, p = probs(h, jnp.where(kc <= qc, s_cur[h], -jnp.inf))
            acc = alpha * (acc_ref[h] + pv_prev) + weighted_values(h, i, p)
            o = acc[:HEAD_DIM] / acc[HEAD_DIM:HEAD_DIM + 1]
            o = o[:, :blk] - lam * o[:, blk:]
            o = o * lax.rsqrt(jnp.mean(o * o, axis=0, keepdims=True) + LN_EPS)
            o = o * g_ref[...] * (1.0 - lambda_init)
            o_ref[0, :, head_cols(h)] = o.T.astype(o_ref.dtype)

    m_ref[...] = jnp.full(m_ref.shape, -jnp.inf, F32)
    acc_ref[...] = jnp.zeros(acc_ref.shape, F32)
    p_b[...] = jnp.zeros(p_b.shape, BF16)
    for h in heads:
        scores_into(h, 0, s_a)

    def pair(jj, c):
        step(2 * jj, s_a, s_b, p_b, p_a)
        step(2 * jj + 1, s_b, s_a, p_a, p_b)
        return c

    lax.fori_loop(0, i // 2, pair, 0)

    @pl.when(i % 2 == 1)
    def _():
        step(i - 1, s_a, s_b, p_b, p_a)
        finish(s_b, p_a)

    @pl.when(i % 2 == 0)
    def _():
        finish(s_a, p_b)


def _attention(qT, k, vT, lq1, lk1, lq2, lk2, sub_g, lambda_init):
    B, nb, d_attn, blk = qT.shape
    S = nb * blk
    kern = functools.partial(_attn_kernel, blk=blk, lambda_init=lambda_init)
    vec = pl.BlockSpec((1, SUB_DIM), lambda b, i: (0, 0))
    return pl.pallas_call(
        kern,
        grid=(B, nb),
        in_specs=[
            vec, vec, vec, vec,
            pl.BlockSpec((HEAD_DIM, 1), lambda b, i: (0, 0)),
            pl.BlockSpec((1, 1, d_attn, blk), lambda b, i: (b, i, 0, 0)),
            pl.BlockSpec((1, S, d_attn), lambda b, i: (b, 0, 0)),
            pl.BlockSpec((1, nb, d_attn, blk), lambda b, i: (b, 0, 0, 0)),
        ],
        out_specs=pl.BlockSpec((1, blk, d_attn), lambda b, i: (b, i, 0)),
        out_shape=jax.ShapeDtypeStruct((B, S, d_attn), BF16),
        scratch_shapes=[
            pltpu.VMEM((N_HEADS, blk, 2 * blk), F32), pltpu.VMEM((N_HEADS, blk, 2 * blk), F32),
            pltpu.VMEM((N_HEADS, blk, 2 * blk), BF16), pltpu.VMEM((N_HEADS, blk, 2 * blk), BF16),
            pltpu.VMEM((N_HEADS, HEAD_DIM + 2 * SUBLANES, 2 * blk), F32),
            pltpu.VMEM((N_HEADS, 1, 2 * blk), F32),
        ],
        compiler_params=pltpu.CompilerParams(
            dimension_semantics=("parallel", "arbitrary"), vmem_limit_bytes=VMEM_LIMIT),
        name="attn",
    )(lq1.reshape(1, SUB_DIM), lk1.reshape(1, SUB_DIM), lq2.reshape(1, SUB_DIM), lk2.reshape(1, SUB_DIM),
      sub_g.reshape(HEAD_DIM, 1), qT, k, vT)


def _route(scores, bias):
    n_e, tm = scores.shape
    neg = -jnp.inf
    biased = scores + bias
    g3 = biased.reshape(N_GROUPS, GROUP_SIZE, tm)
    io3 = lax.broadcasted_iota(I32, g3.shape, 1)
    m1 = jnp.max(g3, axis=1, keepdims=True)
    first = jnp.min(jnp.where(g3 == m1, io3, GROUP_SIZE), axis=1, keepdims=True)
    m2 = jnp.max(jnp.where(io3 == first, neg, g3), axis=1, keepdims=True)
    grp_score = (m1 + m2).reshape(N_GROUPS, tm)
    gio = lax.broadcasted_iota(I32, (N_GROUPS, tm), 0)
    grp_sel = jnp.zeros((N_GROUPS, tm), jnp.bool_)
    cur = grp_score
    for _ in range(TOPK_GROUPS):
        mx = jnp.max(cur, axis=0, keepdims=True)
        f = jnp.min(jnp.where(cur == mx, gio, N_GROUPS), axis=0, keepdims=True)
        hit = gio == f
        grp_sel = grp_sel | hit
        cur = jnp.where(hit, neg, cur)
    grp_f = jnp.where(grp_sel, 1.0, 0.0).astype(F32)
    exp_mask = jnp.broadcast_to(grp_f.reshape(N_GROUPS, 1, tm), (N_GROUPS, GROUP_SIZE, tm)).reshape(n_e, tm) > 0.5
    eio = lax.broadcasted_iota(I32, (n_e, tm), 0)
    cur = jnp.where(exp_mask, biased, neg)
    sel = jnp.zeros((n_e, tm), jnp.bool_)
    for _ in range(TOP_K):
        mx = jnp.max(cur, axis=0, keepdims=True)
        f = jnp.min(jnp.where(cur == mx, eio, n_e), axis=0, keepdims=True)
        hit = eio == f
        sel = sel | hit
        cur = jnp.where(hit, neg, cur)
    denom = jnp.sum(jnp.where(sel, scores, 0.0), axis=0, keepdims=True)
    gates = jnp.where(sel, scores / denom * ROUTED_SCALE, 0.0)
    return sel, gates


def _mix_kernel(h_ref, halo_ref, cw_ref, cb_ref, cg_ref, cbeta_ref, attn_ref, wo_ref, x_ref, g1_ref, b1_ref,
                wrh_ref, wrl_ref, rb_ref,
                x1_ref, rank_ref, gate_ref, cnt_ref,
                buf, shifted, cbuf, carry, *, tm, d_attn, tiles_per_span):
    tile = pl.program_id(0) * pl.num_programs(1) + pl.program_id(1)

    @pl.when(tile % tiles_per_span == 0)
    def _():
        carry[...] = jnp.zeros_like(carry)

    @pl.when(pl.program_id(1) == 0)
    def _():
        buf[0:CONV_HALO, :] = jnp.zeros((CONV_HALO, buf.shape[1]), F32)

    @pl.when(pl.program_id(1) != 0)
    def _():
        buf[0:CONV_HALO, :] = halo_ref[...]

    buf[CONV_HALO:, :] = h_ref[...]
    d_conv = buf.shape[1]
    for s in range(1, SUBLANES):
        shifted[s - 1] = buf[s:s + shifted.shape[1], :]
    off = CONV_HALO - (CONV_WIDTH - 1)
    for c in range(d_conv // LANES):
        cs = slice(c * LANES, (c + 1) * LANES)
        for r in range(tm // CONV_ROW_CHUNK):
            acc = jnp.zeros((CONV_ROW_CHUNK, LANES), F32)
            for j in range(CONV_WIDTH):
                s = (off + j) % SUBLANES
                start = r * CONV_ROW_CHUNK + off + j - s
                src = buf if s == 0 else shifted.at[s - 1]
                acc = acc + cw_ref[j:j + 1, cs] * src[start:start + CONV_ROW_CHUNK, cs]
            cbuf[r * CONV_ROW_CHUNK:(r + 1) * CONV_ROW_CHUNK, cs] = acc + cb_ref[:, cs]
    conv = _layer_norm(cbuf[...], cg_ref[...], cbeta_ref[...])
    conv = conv * jax.nn.sigmoid(conv)

    mix = _dot(attn_ref[...], wo_ref[0:d_attn, :]) + _dot(conv.astype(BF16), wo_ref[d_attn:, :])
    x1 = _layer_norm(ALPHA * x_ref[...] + mix, g1_ref[...], b1_ref[...])
    x1_ref[...] = x1

    xh = x1.astype(BF16)
    xl = (x1 - xh.astype(F32)).astype(BF16)
    logits = _dot(xh, wrh_ref[...]) + _dot(xl, wrh_ref[...]) + _dot(xh, wrl_ref[...])
    scores = jax.nn.sigmoid(logits).T
    sel, gates = _route(scores, rb_ref[...])
    gate_ref[...] = gates

    t_row = lax.broadcasted_iota(I32, (tm, tm), 0)
    t_col = lax.broadcasted_iota(I32, (tm, tm), 1)
    before = jnp.where(t_row < t_col, 1.0, 0.0).astype(BF16)
    sel_f = jnp.where(sel, 1.0, 0.0).astype(F32)
    rank = _dot(sel_f.astype(BF16), before) + carry[...]
    rank_ref[...] = jnp.where(sel, rank, -1.0).astype(I32)
    carry[...] = carry[...] + jnp.sum(sel_f, axis=1, keepdims=True)
    cnt_ref[...] = carry[...].astype(I32)


def _mix(h, attn, x, conv_w, conv_b, cln_g, cln_b, w_out_bf16, ln1_g, ln1_b, w_router, router_bias):
    B, S, D = x.shape
    d_conv = h.shape[-1]
    d_attn = attn.shape[-1]
    tm = MIX_ROWS
    nt = S // tm
    N = B * S
    n_e = w_router.shape[-1]
    wr_hi = w_router.astype(BF16)
    wr_lo = (w_router - wr_hi.astype(F32)).astype(BF16)
    halo_blocks = tm // CONV_HALO
    row = lambda a: a.reshape(1, -1)
    const = lambda shape: pl.BlockSpec(shape, lambda b, i: (0,) * len(shape))
    tiles_per_span = MOE_SPAN // tm
    kern = functools.partial(_mix_kernel, tm=tm, d_attn=d_attn, tiles_per_span=tiles_per_span)
    return pl.pallas_call(
        kern,
        grid=(B, nt),
        in_specs=[
            pl.BlockSpec((None, tm, d_conv), lambda b, i: (b, i, 0)),
            pl.BlockSpec((None, CONV_HALO, d_conv), lambda b, i: (b, jnp.maximum(i * halo_blocks - 1, 0), 0)),
            const(conv_w.shape), const((1, d_conv)), const((1, d_conv)), const((1, d_conv)),
            pl.BlockSpec((None, tm, d_attn), lambda b, i: (b, i, 0)),
            const(w_out_bf16.shape),
            pl.BlockSpec((None, tm, D), lambda b, i: (b, i, 0)),
            const((1, D)), const((1, D)),
            const(wr_hi.shape), const(wr_lo.shape), const((n_e, 1)),
        ],
        out_specs=[
            pl.BlockSpec((tm, D), lambda b, i: (b * nt + i, 0)),
            pl.BlockSpec((n_e, tm), lambda b, i: (0, b * nt + i)),
            pl.BlockSpec((n_e, tm), lambda b, i: (0, b * nt + i)),
            pl.BlockSpec((None, n_e, 1), lambda b, i: (b * nt + i, 0, 0)),
        ],
        out_shape=[
            jax.ShapeDtypeStruct((N, D), F32),
            jax.ShapeDtypeStruct((n_e, N), I32),
            jax.ShapeDtypeStruct((n_e, N), F32),
            jax.ShapeDtypeStruct((B * nt, n_e, 1), I32),
        ],
        scratch_shapes=[
            pltpu.VMEM((tm + CONV_HALO, d_conv), F32),
            pltpu.VMEM((SUBLANES - 1, tm + CONV_HALO - SUBLANES, d_conv), F32),
            pltpu.VMEM((tm, d_conv), F32),
            pltpu.VMEM((n_e, 1), F32),
        ],
        compiler_params=pltpu.CompilerParams(
            dimension_semantics=("arbitrary", "arbitrary"), vmem_limit_bytes=VMEM_LIMIT),
        name="mix",
    )(h, h, conv_w, row(conv_b), row(cln_g), row(cln_b), attn, w_out_bf16, x, row(ln1_g), row(ln1_b),
      wr_hi, wr_lo, router_bias.reshape(n_e, 1))


def _moe_kernel(ntrip_ref, x1_ref, rank_ref, gate_ref, wgu_ref, wd_ref, wsgu_ref, wsd_ref,
                g2_ref, b2_ref, o_ref, xb_ref, *, rows, d_e, d_s, per_step, span):
    w = pl.program_id(0)
    step = pl.program_id(1)

    @pl.when(step == 0)
    def _():
        x1 = x1_ref[...]
        xb = x1.astype(BF16)
        xb_ref[...] = xb
        gu = _dot(xb, wsgu_ref[...])
        g = gu[:, :d_s]
        shared = _dot((g * jax.nn.sigmoid(g) * gu[:, d_s:]).astype(BF16), wsd_ref[...])
        o_ref[...] = ALPHA * x1 + shared

    n_spans = xb_ref.shape[0] // span

    def trip(c, carry):
        onehots, gathered = [], []
        slot_gates = [[None] * n_spans for _ in range(per_step)]
        for s in range(n_spans):
            lo, hi = s * span, (s + 1) * span
            hits = []
            for k in range(per_step):
                e = step * per_step + k
                rank = rank_ref[pl.ds(e, 1), lo:hi]
                hit = (lax.broadcasted_iota(I32, (rows, 1), 0) + c * rows) == rank
                hits.append(hit)
                slot_gates[k][s] = jnp.sum(jnp.where(hit, gate_ref[pl.ds(e, 1), lo:hi], 0.0), axis=1, keepdims=True)
            onehot = jnp.where(jnp.concatenate(hits, axis=0), 1.0, 0.0).astype(BF16)
            onehots.append(onehot)
            gathered.append(_dot(onehot, xb_ref[lo:hi, :]).astype(BF16))
        gated = []
        for k in range(per_step):
            sl = slice(k * rows, (k + 1) * rows)
            xs = jnp.concatenate([gathered[s][sl] for s in range(n_spans)], axis=0)
            gu = _dot(xs, wgu_ref[k])
            g = gu[:, :d_e]
            y = _dot((g * jax.nn.sigmoid(g) * gu[:, d_e:]).astype(BF16), wd_ref[k])
            gated.append((y * jnp.concatenate(slot_gates[k], axis=0)).astype(BF16))
        for s in range(n_spans):
            sl = slice(s * rows, (s + 1) * rows)
            yg = jnp.concatenate([gated[k][sl] for k in range(per_step)], axis=0)
            o_ref[s * span:(s + 1) * span, :] += lax.dot_general(
                onehots[s], yg, (((0,), (0,)), ((), ())), preferred_element_type=F32)
        return carry

    lax.fori_loop(0, ntrip_ref[w * pl.num_programs(1) + step], trip, 0)

    @pl.when(step == pl.num_programs(1) - 1)
    def _():
        o_ref[...] = _layer_norm(o_ref[...], g2_ref[...], b2_ref[...])


def _moe(x1, rank, gates, counts, w_gate, w_up, w_down, ws_gate, ws_up, ws_down, ln2_g, ln2_b):
    N, D = x1.shape
    n_e, _, d_e = w_gate.shape
    d_s = ws_down.shape[0]
    W = MOE_WINDOW
    n_win = N // W
    rows = EXPERT_ROWS
    span = MOE_SPAN
    per_step = MOE_EXPERTS_PER_STEP
    n_steps = n_e // per_step
    per_span = counts.reshape(N // span, span // MIX_ROWS, n_e)[:, -1]
    group = per_span.reshape(n_win, W // span, n_steps, per_step)
    ntrip = jnp.max((group + rows - 1) // rows, axis=(1, 3)).astype(I32).reshape(n_win * n_steps)
    wgu = jnp.concatenate([w_gate, w_up], axis=-1).astype(BF16)
    wsgu = jnp.concatenate([ws_gate, ws_up], axis=-1).astype(BF16)
    kern = functools.partial(_moe_kernel, rows=rows, d_e=d_e, d_s=d_s, per_step=per_step, span=span)
    const = lambda shape: pl.BlockSpec(shape, lambda w, e, nt: (0,) * len(shape))
    return pl.pallas_call(
        kern,
        grid_spec=pltpu.PrefetchScalarGridSpec(
            num_scalar_prefetch=1,
            grid=(n_win, n_steps),
            in_specs=[
                pl.BlockSpec((W, D), lambda w, e, nt: (w, 0)),
                pl.BlockSpec((n_e, W), lambda w, e, nt: (0, w)),
                pl.BlockSpec((n_e, W), lambda w, e, nt: (0, w)),
                pl.BlockSpec((per_step, D, 2 * d_e), lambda w, e, nt: (e, 0, 0)),
                pl.BlockSpec((per_step, d_e, D), lambda w, e, nt: (e, 0, 0)),
                const(wsgu.shape), const((d_s, D)), const((1, D)), const((1, D)),
            ],
            out_specs=pl.BlockSpec((W, D), lambda w, e, nt: (w, 0)),
            scratch_shapes=[pltpu.VMEM((W, D), BF16)],
        ),
        out_shape=jax.ShapeDtypeStruct((N, D), F32),
        compiler_params=pltpu.CompilerParams(
            dimension_semantics=("arbitrary", "arbitrary"), vmem_limit_bytes=VMEM_LIMIT),
        name="moe",
    )(ntrip, x1, rank, gates, wgu, w_down.astype(BF16), wsgu, ws_down.astype(BF16),
      ln2_g.reshape(1, D), ln2_b.reshape(1, D))


def kernel(x, positions, w_in, lambda_q1, lambda_k1, lambda_q2, lambda_k2, subln_g, conv_w, conv_b, conv_ln_g, conv_ln_b, w_out, ln1_g, ln1_b, w_router, router_bias, w_exp_gate, w_exp_up, w_exp_down, w_sh_gate, w_sh_up, w_sh_down, ln2_g, ln2_b):
    B, S, D = x.shape
    d_attn = N_HEADS * HEAD_DIM
    d_conv = conv_w.shape[-1]
    for l in range(DEPTH):
        lambda_init = 0.8 - 0.6 * math.exp(-0.3 * l)
        qT, k, vT, h = _proj(x, positions, w_in[l].astype(BF16), d_attn, d_conv)
        attn = _attention(qT, k, vT, lambda_q1[l], lambda_k1[l], lambda_q2[l], lambda_k2[l], subln_g[l],
                          lambda_init)
        x1, rank, gates, counts = _mix(h, attn, x, conv_w[l], conv_b[l], conv_ln_g[l], conv_ln_b[l],
                                       w_out[l].astype(BF16), ln1_g[l], ln1_b[l], w_router[l], router_bias[l])
        out = _moe(x1, rank, gates, counts, w_exp_gate[l], w_exp_up[l], w_exp_down[l],
                   w_sh_gate[l], w_sh_up[l], w_sh_down[l], ln2_g[l], ln2_b[l])
        x = out.reshape(B, S, D)
    return x
```

```python
import functools
import math

import jax
import jax.numpy as jnp
from jax import lax
from jax.experimental import pallas as pl
from jax.experimental.pallas import tpu as pltpu

F32 = jnp.float32
BF16 = jnp.bfloat16
I32 = jnp.int32
U32 = jnp.uint32

N_HEADS = 4
HEAD_DIM = 128
SUB_DIM = 64
ROPE_DIM = 16
ROPE_THETA = 500000.0
CHUNK = 64
CONV_WIDTH = 31
N_EXPERTS = 64
TOP_K = 8
N_GROUPS = 8
GROUP_SIZE = N_EXPERTS // N_GROUPS
TOPK_GROUPS = 4
ROUTED_SCALE = 2.5
LN_EPS = 1e-5
DEPTH = 1
ALPHA = (2.0 * DEPTH) ** 0.25
LOG2_E = math.log2(math.e)

LANES = 128
SUBLANES = 8
VMEM_LIMIT = 56 * 1024 * 1024
MXU_DEPTH = 256

PROJ_ROWS = 512
ATTN_BLOCK = 256
MIX_ROWS = 256
CONV_HALO = 32
CONV_ROW_CHUNK = 64
MOE_WINDOW = 1024
MOE_SPAN = 256
EXPERT_ROWS = 48
MOE_EXPERTS_PER_STEP = 4


def _dot(a, b):
    return jnp.dot(a, b, preferred_element_type=F32)


def _layer_norm(z, g, b):
    mu = jnp.mean(z, axis=-1, keepdims=True)
    zc = z - mu
    var = jnp.mean(zc * zc, axis=-1, keepdims=True)
    return zc * lax.rsqrt(var + LN_EPS) * g + b


def _proj_kernel(x_ref, pos_ref, invf_ref, w_ref, qT_ref, k_ref, vT_ref, h_ref, *, d_attn, d_conv, blk):
    xb = x_ref[...].astype(BF16)
    tm = xb.shape[0]
    ang = pos_ref[...].astype(F32) * invf_ref[...]
    cos = jnp.cos(ang)
    sin = jnp.sin(ang)
    d = lax.broadcasted_iota(I32, (1, LANES), 1) & (SUB_DIM - 1)
    half = ROPE_DIM // 2
    c_mul = jnp.where(d < ROPE_DIM, cos, 1.0)
    s_lo = jnp.where(d < half, -sin, 0.0)
    s_hi = jnp.where((d >= half) & (d < ROPE_DIM), sin, 0.0)

    def rope(t):
        outs = []
        for c in range(t.shape[1] // LANES):
            ts = t[:, c * LANES:(c + 1) * LANES]
            outs.append(ts * c_mul + pltpu.roll(ts, LANES - half, 1) * s_lo + pltpu.roll(ts, half, 1) * s_hi)
        return jnp.concatenate(outs, axis=1)

    q = rope(_dot(xb, w_ref[:, 0:d_attn])) * (SUB_DIM ** -0.5 * LOG2_E)
    for c in range(tm // blk):
        qT_ref[0, c] = q[c * blk:(c + 1) * blk, :].T.astype(BF16)
    k = rope(_dot(xb, w_ref[:, d_attn:2 * d_attn]))
    k_ref[0] = k.astype(BF16)
    v = _dot(xb, w_ref[:, 2 * d_attn:3 * d_attn])
    for c in range(tm // blk):
        vT_ref[0, c] = v[c * blk:(c + 1) * blk, :].T.astype(BF16)
    a = _dot(xb, w_ref[:, 3 * d_attn:3 * d_attn + d_conv])
    gate = _dot(xb, w_ref[:, 3 * d_attn + d_conv:3 * d_attn + 2 * d_conv])
    h_ref[0] = a * jax.nn.sigmoid(gate)


def _proj(x, positions, w_in_bf16, d_attn, d_conv):
    B, S, D = x.shape
    tm, blk = PROJ_ROWS, ATTN_BLOCK
    nb = S // blk
    inv_freq = ROPE_THETA ** (-jnp.arange(0, ROPE_DIM, 2, dtype=F32) / ROPE_DIM)
    half = ROPE_DIM // 2
    sub = jnp.concatenate([inv_freq, inv_freq, jnp.zeros((SUB_DIM - ROPE_DIM,), F32)])
    invf = jnp.tile(sub, LANES // SUB_DIM).reshape(1, LANES)
    d_in = w_in_bf16.shape[1]
    kern = functools.partial(_proj_kernel, d_attn=d_attn, d_conv=d_conv, blk=blk)
    return pl.pallas_call(
        kern,
        grid=(B, S // tm),
        in_specs=[
            pl.BlockSpec((None, tm, D), lambda b, i: (b, i, 0)),
            pl.BlockSpec((None, tm, 1), lambda b, i: (b, i, 0)),
            pl.BlockSpec((1, LANES), lambda b, i: (0, 0)),
            pl.BlockSpec((D, d_in), lambda b, i: (0, 0)),
        ],
        out_specs=[
            pl.BlockSpec((1, tm // blk, d_attn, blk), lambda b, i: (b, i, 0, 0)),
            pl.BlockSpec((1, tm, d_attn), lambda b, i: (b, i, 0)),
            pl.BlockSpec((1, tm // blk, d_attn, blk), lambda b, i: (b, i, 0, 0)),
            pl.BlockSpec((1, tm, d_conv), lambda b, i: (b, i, 0)),
        ],
        out_shape=[
            jax.ShapeDtypeStruct((B, nb, d_attn, blk), BF16),
            jax.ShapeDtypeStruct((B, S, d_attn), BF16),
            jax.ShapeDtypeStruct((B, nb, d_attn, blk), BF16),
            jax.ShapeDtypeStruct((B, S, d_conv), F32),
        ],
        compiler_params=pltpu.CompilerParams(
            dimension_semantics=("parallel", "parallel"), vmem_limit_bytes=VMEM_LIMIT),
        name="proj",
    )(x, positions.reshape(B, S, 1), invf, w_in_bf16)


def _attn_kernel(lq1_ref, lk1_ref, lq2_ref, lk2_ref, g_ref, qT_ref, k_ref, vT_ref, wg_ref, wu_ref, wd_ref,
                 o_ref, wgu_out, wd_out, s_a, s_b, p_a, p_b, acc_ref, m_ref, *, blk, lambda_init):
    i = pl.program_id(1)
    d_e = wg_ref.shape[-1]
    wgu_out[:, :, :d_e] = wg_ref[...].astype(wgu_out.dtype)
    wgu_out[:, :, d_e:] = wu_ref[...].astype(wgu_out.dtype)
    wd_out[...] = wd_ref[...].astype(wd_out.dtype)
    lam = (jnp.exp(jnp.sum(lq1_ref[...] * lk1_ref[...], axis=-1, keepdims=True))
           - jnp.exp(jnp.sum(lq2_ref[...] * lk2_ref[...], axis=-1, keepdims=True)) + lambda_init)
    row = lax.broadcasted_iota(I32, (HEAD_DIM, 1), 0)
    ones = jnp.ones((2 * SUBLANES, blk), BF16)
    heads = range(N_HEADS)

    def head_cols(h):
        return slice(h * HEAD_DIM, (h + 1) * HEAD_DIM)

    def q_both(h):
        qT = qT_ref[0, 0, head_cols(h), :]
        zero = jnp.zeros_like(qT)
        return jnp.concatenate([jnp.where(row < SUB_DIM, qT, zero), jnp.where(row >= SUB_DIM, qT, zero)], axis=1)

    def scores_into(h, j, s_ref):
        kb = k_ref[0, pl.ds(pl.multiple_of(j * blk, blk), blk), head_cols(h)]
        s_ref[h] = _dot(kb, q_both(h))

    def weighted_values(h, j, p):
        return _dot(jnp.concatenate([vT_ref[0, j, head_cols(h), :], ones], axis=0), p)

    def probs(h, s):
        m = m_ref[h]
        m_new = jnp.maximum(m, jnp.max(s, axis=0, keepdims=True))
        m_ref[h] = m_new
        return jnp.exp2(m - m_new), jnp.exp2(s - m_new).astype(BF16)

    def step(j, s_cur, s_nxt, p_prev, p_cur):
        for h in heads:
            scores_into(h, j + 1, s_nxt)
            pv_prev = weighted_values(h, jnp.maximum(j - 1, 0), p_prev[h])
            alpha, p = probs(h, s_cur[h])
            p_cur[h] = p
            acc_ref[h] = alpha * (acc_ref[h] + pv_prev)

    def finish(s_cur, p_prev):
        kc = lax.broadcasted_iota(I32, (blk, 1), 0) // CHUNK
        qc = (lax.broadcasted_iota(I32, (1, 2 * blk), 1) & (blk - 1)) // CHUNK
        for h in heads:
            pv_prev = weighted_values(h, jnp.maximum(i - 1, 0), p_prev[h])
            alpha, p = probs(h, jnp.where(kc <= qc, s_cur[h], -jnp.inf))
            acc = alpha * (acc_ref[h] + pv_prev) + weighted_values(h, i, p)
            o = acc[:HEAD_DIM] / acc[HEAD_DIM:HEAD_DIM + 1]
            o = o[:, :blk] - lam * o[:, blk:]
            o = o * lax.rsqrt(jnp.mean(o * o, axis=0, keepdims=True) + LN_EPS)
            o = o * g_ref[...] * (1.0 - lambda_init)
            o_ref[0, :, head_cols(h)] = o.T.astype(o_ref.dtype)

    m_ref[...] = jnp.full(m_ref.shape, -jnp.inf, F32)
    acc_ref[...] = jnp.zeros(acc_ref.shape, F32)
    p_b[...] = jnp.zeros(p_b.shape, BF16)
    for h in heads:
        scores_into(h, 0, s_a)

    def pair(jj, c):
        step(2 * jj, s_a, s_b, p_b, p_a)
        step(2 * jj + 1, s_b, s_a, p_a, p_b)
        return c

    lax.fori_loop(0, i // 2, pair, 0)

    @pl.when(i % 2 == 1)
    def _():
        step(i - 1, s_a, s_b, p_b, p_a)
        finish(s_b, p_a)

    @pl.when(i % 2 == 0)
    def _():
        finish(s_a, p_b)


def _attention(qT, k, vT, lq1, lk1, lq2, lk2, sub_g, lambda_init, w_gate, w_up, w_down):
    B, nb, d_attn, blk = qT.shape
    S = nb * blk
    n_e, D, d_e = w_gate.shape
    per_step = n_e // (B * nb)
    assert per_step * B * nb == n_e
    kern = functools.partial(_attn_kernel, blk=blk, lambda_init=lambda_init)
    vec = pl.BlockSpec((1, SUB_DIM), lambda b, i: (0, 0))
    w_slice = lambda shape: pl.BlockSpec((per_step,) + shape, lambda b, i: (b * nb + i, 0, 0))
    return pl.pallas_call(
        kern,
        grid=(B, nb),
        in_specs=[
            vec, vec, vec, vec,
            pl.BlockSpec((HEAD_DIM, 1), lambda b, i: (0, 0)),
            pl.BlockSpec((1, 1, d_attn, blk), lambda b, i: (b, i, 0, 0)),
            pl.BlockSpec((1, S, d_attn), lambda b, i: (b, 0, 0)),
            pl.BlockSpec((1, nb, d_attn, blk), lambda b, i: (b, 0, 0, 0)),
            w_slice((D, d_e)), w_slice((D, d_e)), w_slice((d_e, D)),
        ],
        out_specs=[
            pl.BlockSpec((1, blk, d_attn), lambda b, i: (b, i, 0)),
            w_slice((D, 2 * d_e)), w_slice((d_e, D)),
        ],
        out_shape=[
            jax.ShapeDtypeStruct((B, S, d_attn), BF16),
            jax.ShapeDtypeStruct((n_e, D, 2 * d_e), BF16),
            jax.ShapeDtypeStruct((n_e, d_e, D), BF16),
        ],
        scratch_shapes=[
            pltpu.VMEM((N_HEADS, blk, 2 * blk), F32), pltpu.VMEM((N_HEADS, blk, 2 * blk), F32),
            pltpu.VMEM((N_HEADS, blk, 2 * blk), BF16), pltpu.VMEM((N_HEADS, blk, 2 * blk), BF16),
            pltpu.VMEM((N_HEADS, HEAD_DIM + 2 * SUBLANES, 2 * blk), F32),
            pltpu.VMEM((N_HEADS, 1, 2 * blk), F32),
        ],
        compiler_params=pltpu.CompilerParams(
            dimension_semantics=("parallel", "arbitrary"), vmem_limit_bytes=VMEM_LIMIT),
        name="attn",
    )(lq1.reshape(1, SUB_DIM), lk1.reshape(1, SUB_DIM), lq2.reshape(1, SUB_DIM), lk2.reshape(1, SUB_DIM),
      sub_g.reshape(HEAD_DIM, 1), qT, k, vT, w_gate, w_up, w_down)


def _route(scores, bias):
    n_e, tm = scores.shape
    neg = -jnp.inf
    biased = scores + bias
    g3 = biased.reshape(N_GROUPS, GROUP_SIZE, tm)
    io3 = lax.broadcasted_iota(I32, g3.shape, 1)
    m1 = jnp.max(g3, axis=1, keepdims=True)
    first = jnp.min(jnp.where(g3 == m1, io3, GROUP_SIZE), axis=1, keepdims=True)
    m2 = jnp.max(jnp.where(io3 == first, neg, g3), axis=1, keepdims=True)
    grp_score = (m1 + m2).reshape(N_GROUPS, tm)
    gio = lax.broadcasted_iota(I32, (N_GROUPS, tm), 0)
    grp_sel = jnp.zeros((N_GROUPS, tm), jnp.bool_)
    cur = grp_score
    for _ in range(TOPK_GROUPS):
        mx = jnp.max(cur, axis=0, keepdims=True)
        f = jnp.min(jnp.where(cur == mx, gio, N_GROUPS), axis=0, keepdims=True)
        hit = gio == f
        grp_sel = grp_sel | hit
        cur = jnp.where(hit, neg, cur)
    grp_f = jnp.where(grp_sel, 1.0, 0.0).astype(F32)
    exp_mask = jnp.broadcast_to(grp_f.reshape(N_GROUPS, 1, tm), (N_GROUPS, GROUP_SIZE, tm)).reshape(n_e, tm) > 0.5
    eio = lax.broadcasted_iota(I32, (n_e, tm), 0)
    cur = jnp.where(exp_mask, biased, neg)
    sel = jnp.zeros((n_e, tm), jnp.bool_)
    for _ in range(TOP_K):
        mx = jnp.max(cur, axis=0, keepdims=True)
        f = jnp.min(jnp.where(cur == mx, eio, n_e), axis=0, keepdims=True)
        hit = eio == f
        sel = sel | hit
        cur = jnp.where(hit, neg, cur)
    denom = jnp.sum(jnp.where(sel, scores, 0.0), axis=0, keepdims=True)
    gates = jnp.where(sel, scores / denom * ROUTED_SCALE, 0.0)
    return sel, gates


def _mix_kernel(h_ref, halo_ref, cw_ref, cb_ref, cg_ref, cbeta_ref, attn_ref, wo_ref, x_ref, g1_ref, b1_ref,
                wrh_ref, wrl_ref, rb_ref,
                x1_ref, rank_ref, gate_ref, cnt_ref,
                buf, shifted, cbuf, carry, *, tm, d_attn, tiles_per_span):
    tile = pl.program_id(0) * pl.num_programs(1) + pl.program_id(1)

    @pl.when(tile % tiles_per_span == 0)
    def _():
        carry[...] = jnp.zeros_like(carry)

    @pl.when(pl.program_id(1) == 0)
    def _():
        buf[0:CONV_HALO, :] = jnp.zeros((CONV_HALO, buf.shape[1]), F32)

    @pl.when(pl.program_id(1) != 0)
    def _():
        buf[0:CONV_HALO, :] = halo_ref[...]

    buf[CONV_HALO:, :] = h_ref[...]
    d_conv = buf.shape[1]
    for s in range(1, SUBLANES):
        shifted[s - 1] = buf[s:s + shifted.shape[1], :]
    off = CONV_HALO - (CONV_WIDTH - 1)
    for c in range(d_conv // LANES):
        cs = slice(c * LANES, (c + 1) * LANES)
        for r in range(tm // CONV_ROW_CHUNK):
            acc = jnp.zeros((CONV_ROW_CHUNK, LANES), F32)
            for j in range(CONV_WIDTH):
                s = (off + j) % SUBLANES
                start = r * CONV_ROW_CHUNK + off + j - s
                src = buf if s == 0 else shifted.at[s - 1]
                acc = acc + cw_ref[j:j + 1, cs] * src[start:start + CONV_ROW_CHUNK, cs]
            cbuf[r * CONV_ROW_CHUNK:(r + 1) * CONV_ROW_CHUNK, cs] = acc + cb_ref[:, cs]
    conv = _layer_norm(cbuf[...], cg_ref[...], cbeta_ref[...])
    conv = conv * jax.nn.sigmoid(conv)

    mix = _dot(attn_ref[...], wo_ref[0:d_attn, :]) + _dot(conv.astype(BF16), wo_ref[d_attn:, :])
    x1 = _layer_norm(ALPHA * x_ref[...] + mix, g1_ref[...], b1_ref[...])
    x1_ref[...] = x1

    xh = x1.astype(BF16)
    xl = (x1 - xh.astype(F32)).astype(BF16)
    logits = _dot(xh, wrh_ref[...]) + _dot(xl, wrh_ref[...]) + _dot(xh, wrl_ref[...])
    scores = jax.nn.sigmoid(logits).T
    sel, gates = _route(scores, rb_ref[...])
    gate_ref[...] = gates

    t_row = lax.broadcasted_iota(I32, (tm, tm), 0)
    t_col = lax.broadcasted_iota(I32, (tm, tm), 1)
    before = jnp.where(t_row < t_col, 1.0, 0.0).astype(BF16)
    sel_f = jnp.where(sel, 1.0, 0.0).astype(F32)
    rank = _dot(sel_f.astype(BF16), before) + carry[...]
    rank_ref[...] = jnp.where(sel, rank, -1.0).astype(I32)
    carry[...] = carry[...] + jnp.sum(sel_f, axis=1, keepdims=True)
    cnt_ref[...] = carry[...].astype(I32)


def _mix(h, attn, x, conv_w, conv_b, cln_g, cln_b, w_out_bf16, ln1_g, ln1_b, w_router, router_bias):
    B, S, D = x.shape
    d_conv = h.shape[-1]
    d_attn = attn.shape[-1]
    tm = MIX_ROWS
    nt = S // tm
    N = B * S
    n_e = w_router.shape[-1]
    wr_hi = w_router.astype(BF16)
    wr_lo = (w_router - wr_hi.astype(F32)).astype(BF16)
    halo_blocks = tm // CONV_HALO
    row = lambda a: a.reshape(1, -1)
    const = lambda shape: pl.BlockSpec(shape, lambda b, i: (0,) * len(shape))
    tiles_per_span = MOE_SPAN // tm
    kern = functools.partial(_mix_kernel, tm=tm, d_attn=d_attn, tiles_per_span=tiles_per_span)
    return pl.pallas_call(
        kern,
        grid=(B, nt),
        in_specs=[
            pl.BlockSpec((None, tm, d_conv), lambda b, i: (b, i, 0)),
            pl.BlockSpec((None, CONV_HALO, d_conv), lambda b, i: (b, jnp.maximum(i * halo_blocks - 1, 0), 0)),
            const(conv_w.shape), const((1, d_conv)), const((1, d_conv)), const((1, d_conv)),
            pl.BlockSpec((None, tm, d_attn), lambda b, i: (b, i, 0)),
            const(w_out_bf16.shape),
            pl.BlockSpec((None, tm, D), lambda b, i: (b, i, 0)),
            const((1, D)), const((1, D)),
            const(wr_hi.shape), const(wr_lo.shape), const((n_e, 1)),
        ],
        out_specs=[
            pl.BlockSpec((tm, D), lambda b, i: (b * nt + i, 0)),
            pl.BlockSpec((n_e, tm), lambda b, i: (0, b * nt + i)),
            pl.BlockSpec((n_e, tm), lambda b, i: (0, b * nt + i)),
            pl.BlockSpec((None, n_e, 1), lambda b, i: (b * nt + i, 0, 0)),
        ],
        out_shape=[
            jax.ShapeDtypeStruct((N, D), F32),
            jax.ShapeDtypeStruct((n_e, N), I32),
            jax.ShapeDtypeStruct((n_e, N), F32),
            jax.ShapeDtypeStruct((B * nt, n_e, 1), I32),
        ],
        scratch_shapes=[
            pltpu.VMEM((tm + CONV_HALO, d_conv), F32),
            pltpu.VMEM((SUBLANES - 1, tm + CONV_HALO - SUBLANES, d_conv), F32),
            pltpu.VMEM((tm, d_conv), F32),
            pltpu.VMEM((n_e, 1), F32),
        ],
        compiler_params=pltpu.CompilerParams(
            dimension_semantics=("arbitrary", "arbitrary"), vmem_limit_bytes=VMEM_LIMIT),
        name="mix",
    )(h, h, conv_w, row(conv_b), row(cln_g), row(cln_b), attn, w_out_bf16, x, row(ln1_g), row(ln1_b),
      wr_hi, wr_lo, router_bias.reshape(n_e, 1))


def _moe_kernel(ntrip_ref, x1_ref, rank_ref, gate_ref, wgu_ref, wd_ref, wsgu_ref, wsd_ref,
                g2_ref, b2_ref, o_ref, xb_ref, *, rows, d_e, d_s, per_step, span):
    w = pl.program_id(0)
    step = pl.program_id(1)

    @pl.when(step == 0)
    def _():
        x1 = x1_ref[...]
        xb = x1.astype(BF16)
        xb_ref[...] = xb
        gu = _dot(xb, wsgu_ref[...])
        g = gu[:, :d_s]
        shared = _dot((g * jax.nn.sigmoid(g) * gu[:, d_s:]).astype(BF16), wsd_ref[...])
        o_ref[...] = ALPHA * x1 + shared

    n_spans = xb_ref.shape[0] // span

    def trip(c, carry):
        onehots, gathered = [], []
        slot_gates = [[None] * n_spans for _ in range(per_step)]
        for s in range(n_spans):
            lo, hi = s * span, (s + 1) * span
            hits = []
            for k in range(per_step):
                e = step * per_step + k
                rank = rank_ref[pl.ds(e, 1), lo:hi]
                hit = (lax.broadcasted_iota(I32, (rows, 1), 0) + c * rows) == rank
                hits.append(hit)
                slot_gates[k][s] = jnp.sum(jnp.where(hit, gate_ref[pl.ds(e, 1), lo:hi], 0.0), axis=1, keepdims=True)
            onehot = jnp.where(jnp.concatenate(hits, axis=0), 1.0, 0.0).astype(BF16)
            onehots.append(onehot)
            gathered.append(_dot(onehot, xb_ref[lo:hi, :]).astype(BF16))
        gated = []
        for k in range(per_step):
            sl = slice(k * rows, (k + 1) * rows)
            xs = jnp.concatenate([gathered[s][sl] for s in range(n_spans)], axis=0)
            gu = _dot(xs, wgu_ref[k])
            g = gu[:, :d_e]
            y = _dot((g * jax.nn.sigmoid(g) * gu[:, d_e:]).astype(BF16), wd_ref[k])
            gated.append((y * jnp.concatenate(slot_gates[k], axis=0)).astype(BF16))
        for s in range(n_spans):
            sl = slice(s * rows, (s + 1) * rows)
            yg = jnp.concatenate([gated[k][sl] for k in range(per_step)], axis=0)
            o_ref[s * span:(s + 1) * span, :] += lax.dot_general(
                onehots[s], yg, (((0,), (0,)), ((), ())), preferred_element_type=F32)
        return carry

    lax.fori_loop(0, ntrip_ref[w * pl.num_programs(1) + step], trip, 0)

    @pl.when(step == pl.num_programs(1) - 1)
    def _():
        o_ref[...] = _layer_norm(o_ref[...], g2_ref[...], b2_ref[...])


def _moe(x1, rank, gates, counts, wgu, wd, ws_gate, ws_up, ws_down, ln2_g, ln2_b):
    N, D = x1.shape
    n_e, d_e, _ = wd.shape
    d_s = ws_down.shape[0]
    W = MOE_WINDOW
    n_win = N // W
    rows = EXPERT_ROWS
    span = MOE_SPAN
    per_step = MOE_EXPERTS_PER_STEP
    n_steps = n_e // per_step
    per_span = counts.reshape(N // span, span // MIX_ROWS, n_e)[:, -1]
    group = per_span.reshape(n_win, W // span, n_steps, per_step)
    ntrip = jnp.max((group + rows - 1) // rows, axis=(1, 3)).astype(I32).reshape(n_win * n_steps)
    wsgu = jnp.concatenate([ws_gate, ws_up], axis=-1).astype(BF16)
    kern = functools.partial(_moe_kernel, rows=rows, d_e=d_e, d_s=d_s, per_step=per_step, span=span)
    const = lambda shape: pl.BlockSpec(shape, lambda w, e, nt: (0,) * len(shape))
    return pl.pallas_call(
        kern,
        grid_spec=pltpu.PrefetchScalarGridSpec(
            num_scalar_prefetch=1,
            grid=(n_win, n_steps),
            in_specs=[
                pl.BlockSpec((W, D), lambda w, e, nt: (w, 0)),
                pl.BlockSpec((n_e, W), lambda w, e, nt: (0, w)),
                pl.BlockSpec((n_e, W), lambda w, e, nt: (0, w)),
                pl.BlockSpec((per_step, D, 2 * d_e), lambda w, e, nt: (e, 0, 0)),
                pl.BlockSpec((per_step, d_e, D), lambda w, e, nt: (e, 0, 0)),
                const(wsgu.shape), const((d_s, D)), const((1, D)), const((1, D)),
            ],
            out_specs=pl.BlockSpec((W, D), lambda w, e, nt: (w, 0)),
            scratch_shapes=[pltpu.VMEM((W, D), BF16)],
        ),
        out_shape=jax.ShapeDtypeStruct((N, D), F32),
        compiler_params=pltpu.CompilerParams(
            dimension_semantics=("arbitrary", "arbitrary"), vmem_limit_bytes=VMEM_LIMIT),
        name="moe",
    )(ntrip, x1, rank, gates, wgu, wd, wsgu, ws_down.astype(BF16),
      ln2_g.reshape(1, D), ln2_b.reshape(1, D))


def kernel(x, positions, w_in, lambda_q1, lambda_k1, lambda_q2, lambda_k2, subln_g, conv_w, conv_b, conv_ln_g, conv_ln_b, w_out, ln1_g, ln1_b, w_router, router_bias, w_exp_gate, w_exp_up, w_exp_down, w_sh_gate, w_sh_up, w_sh_down, ln2_g, ln2_b):
    B, S, D = x.shape
    d_attn = N_HEADS * HEAD_DIM
    d_conv = conv_w.shape[-1]
    for l in range(DEPTH):
        lambda_init = 0.8 - 0.6 * math.exp(-0.3 * l)
        qT, k, vT, h = _proj(x, positions, w_in[l].astype(BF16), d_attn, d_conv)
        attn, wgu, wd = _attention(qT, k, vT, lambda_q1[l], lambda_k1[l], lambda_q2[l], lambda_k2[l], subln_g[l],
                                   lambda_init, w_exp_gate[l], w_exp_up[l], w_exp_down[l])
        x1, rank, gates, counts = _mix(h, attn, x, conv_w[l], conv_b[l], conv_ln_g[l], conv_ln_b[l],
                                       w_out[l].astype(BF16), ln1_g[l], ln1_b[l], w_router[l], router_bias[l])
        out = _moe(x1, rank, gates, counts, wgu, wd, w_sh_gate[l], w_sh_up[l], w_sh_down[l], ln2_g[l], ln2_b[l])
        x = out.reshape(B, S, D)
    return x
```

```python
import functools
import math

import jax
import jax.numpy as jnp
from jax import lax
from jax.experimental import pallas as pl
from jax.experimental.pallas import tpu as pltpu

F32 = jnp.float32
BF16 = jnp.bfloat16
I32 = jnp.int32

N_HEADS = 4
HEAD_DIM = 128
SUB_DIM = 64
ROPE_DIM = 16
ROPE_THETA = 500000.0
CHUNK = 64
CONV_WIDTH = 31
N_EXPERTS = 64
TOP_K = 8
N_GROUPS = 8
GROUP_SIZE = N_EXPERTS // N_GROUPS
TOPK_GROUPS = 4
ROUTED_SCALE = 2.5
LN_EPS = 1e-5
DEPTH = 1
ALPHA = (2.0 * DEPTH) ** 0.25
LOG2_E = math.log2(math.e)

LANES = 128
SUBLANES = 8
VMEM_LIMIT = 56 * 1024 * 1024

PROJ_ROWS = 512
ATTN_BLOCK = 256
MIX_ROWS = 256
CONV_HALO = 32
CONV_ROW_CHUNK = 64
MOE_WINDOW = 1024
MOE_SPAN = 256
EXPERT_ROWS = 48
MOE_EXPERTS_PER_STEP = 8
MOE_EXPERTS_PER_TRIP = 4


def _dot(a, b):
    return jnp.dot(a, b, preferred_element_type=F32)


def _layer_norm(z, g, b):
    mu = jnp.mean(z, axis=-1, keepdims=True)
    zc = z - mu
    var = jnp.mean(zc * zc, axis=-1, keepdims=True)
    return zc * lax.rsqrt(var + LN_EPS) * g + b


def _proj_kernel(x_ref, pos_ref, invf_ref, w_ref, qT_ref, k_ref, vT_ref, h_ref, *, d_attn, d_conv, blk):
    xb = x_ref[...].astype(BF16)
    tm = xb.shape[0]
    ang = pos_ref[...].astype(F32) * invf_ref[...]
    cos = jnp.cos(ang)
    sin = jnp.sin(ang)
    d = lax.broadcasted_iota(I32, (1, LANES), 1) & (SUB_DIM - 1)
    half = ROPE_DIM // 2
    c_mul = jnp.where(d < ROPE_DIM, cos, 1.0)
    s_lo = jnp.where(d < half, -sin, 0.0)
    s_hi = jnp.where((d >= half) & (d < ROPE_DIM), sin, 0.0)

    def rope(t):
        outs = []
        for c in range(t.shape[1] // LANES):
            ts = t[:, c * LANES:(c + 1) * LANES]
            outs.append(ts * c_mul + pltpu.roll(ts, LANES - half, 1) * s_lo + pltpu.roll(ts, half, 1) * s_hi)
        return jnp.concatenate(outs, axis=1)

    q = rope(_dot(xb, w_ref[:, 0:d_attn])) * (SUB_DIM ** -0.5 * LOG2_E)
    for c in range(tm // blk):
        qT_ref[0, c] = q[c * blk:(c + 1) * blk, :].T.astype(BF16)
    k = rope(_dot(xb, w_ref[:, d_attn:2 * d_attn]))
    k_ref[0] = k.astype(BF16)
    v = _dot(xb, w_ref[:, 2 * d_attn:3 * d_attn])
    for c in range(tm // blk):
        vT_ref[0, c] = v[c * blk:(c + 1) * blk, :].T.astype(BF16)
    a = _dot(xb, w_ref[:, 3 * d_attn:3 * d_attn + d_conv])
    gate = _dot(xb, w_ref[:, 3 * d_attn + d_conv:3 * d_attn + 2 * d_conv])
    h_ref[0] = a * jax.nn.sigmoid(gate)


def _proj(x, positions, w_in_bf16, d_attn, d_conv):
    B, S, D = x.shape
    tm, blk = PROJ_ROWS, ATTN_BLOCK
    nb = S // blk
    inv_freq = ROPE_THETA ** (-jnp.arange(0, ROPE_DIM, 2, dtype=F32) / ROPE_DIM)
    sub = jnp.concatenate([inv_freq, inv_freq, jnp.zeros((SUB_DIM - ROPE_DIM,), F32)])
    invf = jnp.tile(sub, LANES // SUB_DIM).reshape(1, LANES)
    d_in = w_in_bf16.shape[1]
    kern = functools.partial(_proj_kernel, d_attn=d_attn, d_conv=d_conv, blk=blk)
    return pl.pallas_call(
        kern,
        grid=(B, S // tm),
        in_specs=[
            pl.BlockSpec((None, tm, D), lambda b, i: (b, i, 0)),
            pl.BlockSpec((None, tm, 1), lambda b, i: (b, i, 0)),
            pl.BlockSpec((1, LANES), lambda b, i: (0, 0)),
            pl.BlockSpec((D, d_in), lambda b, i: (0, 0)),
        ],
        out_specs=[
            pl.BlockSpec((1, tm // blk, d_attn, blk), lambda b, i: (b, i, 0, 0)),
            pl.BlockSpec((1, tm, d_attn), lambda b, i: (b, i, 0)),
            pl.BlockSpec((1, tm // blk, d_attn, blk), lambda b, i: (b, i, 0, 0)),
            pl.BlockSpec((1, tm, d_conv), lambda b, i: (b, i, 0)),
        ],
        out_shape=[
            jax.ShapeDtypeStruct((B, nb, d_attn, blk), BF16),
            jax.ShapeDtypeStruct((B, S, d_attn), BF16),
            jax.ShapeDtypeStruct((B, nb, d_attn, blk), BF16),
            jax.ShapeDtypeStruct((B, S, d_conv), F32),
        ],
        compiler_params=pltpu.CompilerParams(
            dimension_semantics=("parallel", "parallel"), vmem_limit_bytes=VMEM_LIMIT),
        name="proj",
    )(x, positions.reshape(B, S, 1), invf, w_in_bf16)


def _attn_kernel(lq1_ref, lk1_ref, lq2_ref, lk2_ref, g_ref, qT_ref, k_ref, vT_ref, wg_ref, wu_ref, wd_ref,
                 o_ref, wgu_out, wd_out, s_a, s_b, p_a, p_b, acc_ref, m_ref, *, blk, lambda_init):
    i = pl.program_id(1)
    d_e = wg_ref.shape[-1]
    wgu_out[:, :, :d_e] = wg_ref[...].astype(wgu_out.dtype)
    wgu_out[:, :, d_e:] = wu_ref[...].astype(wgu_out.dtype)
    wd_out[...] = wd_ref[...].astype(wd_out.dtype)
    lam = (jnp.exp(jnp.sum(lq1_ref[...] * lk1_ref[...], axis=-1, keepdims=True))
           - jnp.exp(jnp.sum(lq2_ref[...] * lk2_ref[...], axis=-1, keepdims=True)) + lambda_init)
    row = lax.broadcasted_iota(I32, (HEAD_DIM, 1), 0)
    ones = jnp.ones((2 * SUBLANES, blk), BF16)
    heads = range(N_HEADS)

    def head_cols(h):
        return slice(h * HEAD_DIM, (h + 1) * HEAD_DIM)

    def q_both(h):
        qT = qT_ref[0, 0, head_cols(h), :]
        zero = jnp.zeros_like(qT)
        return jnp.concatenate([jnp.where(row < SUB_DIM, qT, zero), jnp.where(row >= SUB_DIM, qT, zero)], axis=1)

    def scores_into(h, j, s_ref):
        kb = k_ref[0, pl.ds(pl.multiple_of(j * blk, blk), blk), head_cols(h)]
        s_ref[h] = _dot(kb, q_both(h))

    def weighted_values(h, j, p):
        return _dot(jnp.concatenate([vT_ref[0, j, head_cols(h), :], ones], axis=0), p)

    def probs(h, s):
        m = m_ref[h]
        m_new = jnp.maximum(m, jnp.max(s, axis=0, keepdims=True))
        m_ref[h] = m_new
        return jnp.exp2(m - m_new), jnp.exp2(s - m_new).astype(BF16)

    def step(j, s_cur, s_nxt, p_prev, p_cur):
        for h in heads:
            scores_into(h, j + 1, s_nxt)
            pv_prev = weighted_values(h, jnp.maximum(j - 1, 0), p_prev[h])
            alpha, p = probs(h, s_cur[h])
            p_cur[h] = p
            acc_ref[h] = alpha * (acc_ref[h] + pv_prev)

    def finish(s_cur, p_prev):
        kc = lax.broadcasted_iota(I32, (blk, 1), 0) // CHUNK
        qc = (lax.broadcasted_iota(I32, (1, 2 * blk), 1) & (blk - 1)) // CHUNK
        for h in heads:
            pv_prev = weighted_values(h, jnp.maximum(i - 1, 0), p_prev[h])
            alpha, p = probs(h, jnp.where(kc <= qc, s_cur[h], -jnp.inf))
            acc = alpha * (acc_ref[h] + pv_prev) + weighted_values(h, i, p)
            o = acc[:HEAD_DIM] / acc[HEAD_DIM:HEAD_DIM + 1]
            o = o[:, :blk] - lam * o[:, blk:]
            o = o * lax.rsqrt(jnp.mean(o * o, axis=0, keepdims=True) + LN_EPS)
            o = o * g_ref[...] * (1.0 - lambda_init)
            o_ref[0, :, head_cols(h)] = o.T.astype(o_ref.dtype)

    m_ref[...] = jnp.full(m_ref.shape, -jnp.inf, F32)
    acc_ref[...] = jnp.zeros(acc_ref.shape, F32)
    p_b[...] = jnp.zeros(p_b.shape, BF16)
    for h in heads:
        scores_into(h, 0, s_a)

    def pair(jj, c):
        step(2 * jj, s_a, s_b, p_b, p_a)
        step(2 * jj + 1, s_b, s_a, p_a, p_b)
        return c

    lax.fori_loop(0, i // 2, pair, 0)

    @pl.when(i % 2 == 1)
    def _():
        step(i - 1, s_a, s_b, p_b, p_a)
        finish(s_b, p_a)

    @pl.when(i % 2 == 0)
    def _():
        finish(s_a, p_b)


def _attention(qT, k, vT, lq1, lk1, lq2, lk2, sub_g, lambda_init, w_gate, w_up, w_down):
    B, nb, d_attn, blk = qT.shape
    S = nb * blk
    n_e, D, d_e = w_gate.shape
    per_step = n_e // (B * nb)
    assert per_step * B * nb == n_e
    kern = functools.partial(_attn_kernel, blk=blk, lambda_init=lambda_init)
    vec = pl.BlockSpec((1, SUB_DIM), lambda b, i: (0, 0))
    w_slice = lambda shape: pl.BlockSpec((per_step,) + shape, lambda b, i: (b * nb + i, 0, 0))
    return pl.pallas_call(
        kern,
        grid=(B, nb),
        in_specs=[
            vec, vec, vec, vec,
            pl.BlockSpec((HEAD_DIM, 1), lambda b, i: (0, 0)),
            pl.BlockSpec((1, 1, d_attn, blk), lambda b, i: (b, i, 0, 0)),
            pl.BlockSpec((1, S, d_attn), lambda b, i: (b, 0, 0)),
            pl.BlockSpec((1, nb, d_attn, blk), lambda b, i: (b, 0, 0, 0)),
            w_slice((D, d_e)), w_slice((D, d_e)), w_slice((d_e, D)),
        ],
        out_specs=[
            pl.BlockSpec((1, blk, d_attn), lambda b, i: (b, i, 0)),
            w_slice((D, 2 * d_e)), w_slice((d_e, D)),
        ],
        out_shape=[
            jax.ShapeDtypeStruct((B, S, d_attn), BF16),
            jax.ShapeDtypeStruct((n_e, D, 2 * d_e), BF16),
            jax.ShapeDtypeStruct((n_e, d_e, D), BF16),
        ],
        scratch_shapes=[
            pltpu.VMEM((N_HEADS, blk, 2 * blk), F32), pltpu.VMEM((N_HEADS, blk, 2 * blk), F32),
            pltpu.VMEM((N_HEADS, blk, 2 * blk), BF16), pltpu.VMEM((N_HEADS, blk, 2 * blk), BF16),
            pltpu.VMEM((N_HEADS, HEAD_DIM + 2 * SUBLANES, 2 * blk), F32),
            pltpu.VMEM((N_HEADS, 1, 2 * blk), F32),
        ],
        compiler_params=pltpu.CompilerParams(
            dimension_semantics=("parallel", "arbitrary"), vmem_limit_bytes=VMEM_LIMIT),
        name="attn",
    )(lq1.reshape(1, SUB_DIM), lk1.reshape(1, SUB_DIM), lq2.reshape(1, SUB_DIM), lk2.reshape(1, SUB_DIM),
      sub_g.reshape(HEAD_DIM, 1), qT, k, vT, w_gate, w_up, w_down)


def _route(scores, bias):
    n_e, tm = scores.shape
    neg = -jnp.inf
    biased = scores + bias
    g3 = biased.reshape(N_GROUPS, GROUP_SIZE, tm)
    io3 = lax.broadcasted_iota(I32, g3.shape, 1)
    m1 = jnp.max(g3, axis=1, keepdims=True)
    first = jnp.min(jnp.where(g3 == m1, io3, GROUP_SIZE), axis=1, keepdims=True)
    m2 = jnp.max(jnp.where(io3 == first, neg, g3), axis=1, keepdims=True)
    grp_score = (m1 + m2).reshape(N_GROUPS, tm)
    gio = lax.broadcasted_iota(I32, (N_GROUPS, tm), 0)
    grp_sel = jnp.zeros((N_GROUPS, tm), jnp.bool_)
    cur = grp_score
    for _ in range(TOPK_GROUPS):
        mx = jnp.max(cur, axis=0, keepdims=True)
        f = jnp.min(jnp.where(cur == mx, gio, N_GROUPS), axis=0, keepdims=True)
        hit = gio == f
        grp_sel = grp_sel | hit
        cur = jnp.where(hit, neg, cur)
    grp_f = jnp.where(grp_sel, 1.0, 0.0).astype(F32)
    exp_mask = jnp.broadcast_to(grp_f.reshape(N_GROUPS, 1, tm), (N_GROUPS, GROUP_SIZE, tm)).reshape(n_e, tm) > 0.5
    eio = lax.broadcasted_iota(I32, (n_e, tm), 0)
    cur = jnp.where(exp_mask, biased, neg)
    sel = jnp.zeros((n_e, tm), jnp.bool_)
    for _ in range(TOP_K):
        mx = jnp.max(cur, axis=0, keepdims=True)
        f = jnp.min(jnp.where(cur == mx, eio, n_e), axis=0, keepdims=True)
        hit = eio == f
        sel = sel | hit
        cur = jnp.where(hit, neg, cur)
    denom = jnp.sum(jnp.where(sel, scores, 0.0), axis=0, keepdims=True)
    gates = jnp.where(sel, scores / denom * ROUTED_SCALE, 0.0)
    return sel, gates


def _mix_kernel(h_ref, halo_ref, cw_ref, cb_ref, cg_ref, cbeta_ref, attn_ref, wo_ref, x_ref, g1_ref, b1_ref,
                wrh_ref, wrl_ref, rb_ref,
                x1_ref, rank_ref, gate_ref, cnt_ref,
                buf, shifted, cbuf, carry, *, tm, d_attn, tiles_per_span):
    tile = pl.program_id(0) * pl.num_programs(1) + pl.program_id(1)

    @pl.when(tile % tiles_per_span == 0)
    def _():
        carry[...] = jnp.zeros_like(carry)

    @pl.when(pl.program_id(1) == 0)
    def _():
        buf[0:CONV_HALO, :] = jnp.zeros((CONV_HALO, buf.shape[1]), F32)

    @pl.when(pl.program_id(1) != 0)
    def _():
        buf[0:CONV_HALO, :] = halo_ref[...]

    buf[CONV_HALO:, :] = h_ref[...]
    d_conv = buf.shape[1]
    for s in range(1, SUBLANES):
        shifted[s - 1] = buf[s:s + shifted.shape[1], :]
    off = CONV_HALO - (CONV_WIDTH - 1)
    for c in range(d_conv // LANES):
        cs = slice(c * LANES, (c + 1) * LANES)
        for r in range(tm // CONV_ROW_CHUNK):
            acc = jnp.zeros((CONV_ROW_CHUNK, LANES), F32)
            for j in range(CONV_WIDTH):
                s = (off + j) % SUBLANES
                start = r * CONV_ROW_CHUNK + off + j - s
                src = buf if s == 0 else shifted.at[s - 1]
                acc = acc + cw_ref[j:j + 1, cs] * src[start:start + CONV_ROW_CHUNK, cs]
            cbuf[r * CONV_ROW_CHUNK:(r + 1) * CONV_ROW_CHUNK, cs] = acc + cb_ref[:, cs]
    conv = _layer_norm(cbuf[...], cg_ref[...], cbeta_ref[...])
    conv = conv * jax.nn.sigmoid(conv)

    mix = _dot(attn_ref[...], wo_ref[0:d_attn, :]) + _dot(conv.astype(BF16), wo_ref[d_attn:, :])
    x1 = _layer_norm(ALPHA * x_ref[...] + mix, g1_ref[...], b1_ref[...])
    x1_ref[...] = x1

    xh = x1.astype(BF16)
    xl = (x1 - xh.astype(F32)).astype(BF16)
    logits = _dot(xh, wrh_ref[...]) + _dot(xl, wrh_ref[...]) + _dot(xh, wrl_ref[...])
    scores = jax.nn.sigmoid(logits).T
    sel, gates = _route(scores, rb_ref[...])
    gate_ref[...] = gates

    t_row = lax.broadcasted_iota(I32, (tm, tm), 0)
    t_col = lax.broadcasted_iota(I32, (tm, tm), 1)
    before = jnp.where(t_row < t_col, 1.0, 0.0).astype(BF16)
    sel_f = jnp.where(sel, 1.0, 0.0).astype(F32)
    rank = _dot(sel_f.astype(BF16), before) + carry[...]
    rank_ref[...] = jnp.where(sel, rank, -1.0).astype(I32)
    carry[...] = carry[...] + jnp.sum(sel_f, axis=1, keepdims=True)
    cnt_ref[...] = carry[...].astype(I32)


def _mix(h, attn, x, conv_w, conv_b, cln_g, cln_b, w_out_bf16, ln1_g, ln1_b, w_router, router_bias):
    B, S, D = x.shape
    d_conv = h.shape[-1]
    d_attn = attn.shape[-1]
    tm = MIX_ROWS
    nt = S // tm
    N = B * S
    n_e = w_router.shape[-1]
    wr_hi = w_router.astype(BF16)
    wr_lo = (w_router - wr_hi.astype(F32)).astype(BF16)
    halo_blocks = tm // CONV_HALO
    row = lambda a: a.reshape(1, -1)
    const = lambda shape: pl.BlockSpec(shape, lambda b, i: (0,) * len(shape))
    tiles_per_span = MOE_SPAN // tm
    kern = functools.partial(_mix_kernel, tm=tm, d_attn=d_attn, tiles_per_span=tiles_per_span)
    return pl.pallas_call(
        kern,
        grid=(B, nt),
        in_specs=[
            pl.BlockSpec((None, tm, d_conv), lambda b, i: (b, i, 0)),
            pl.BlockSpec((None, CONV_HALO, d_conv), lambda b, i: (b, jnp.maximum(i * halo_blocks - 1, 0), 0)),
            const(conv_w.shape), const((1, d_conv)), const((1, d_conv)), const((1, d_conv)),
            pl.BlockSpec((None, tm, d_attn), lambda b, i: (b, i, 0)),
            const(w_out_bf16.shape),
            pl.BlockSpec((None, tm, D), lambda b, i: (b, i, 0)),
            const((1, D)), const((1, D)),
            const(wr_hi.shape), const(wr_lo.shape), const((n_e, 1)),
        ],
        out_specs=[
            pl.BlockSpec((tm, D), lambda b, i: (b * nt + i, 0)),
            pl.BlockSpec((n_e, tm), lambda b, i: (0, b * nt + i)),
            pl.BlockSpec((n_e, tm), lambda b, i: (0, b * nt + i)),
            pl.BlockSpec((None, n_e, 1), lambda b, i: (b * nt + i, 0, 0)),
        ],
        out_shape=[
            jax.ShapeDtypeStruct((N, D), F32),
            jax.ShapeDtypeStruct((n_e, N), I32),
            jax.ShapeDtypeStruct((n_e, N), F32),
            jax.ShapeDtypeStruct((B * nt, n_e, 1), I32),
        ],
        scratch_shapes=[
            pltpu.VMEM((tm + CONV_HALO, d_conv), F32),
            pltpu.VMEM((SUBLANES - 1, tm + CONV_HALO - SUBLANES, d_conv), F32),
            pltpu.VMEM((tm, d_conv), F32),
            pltpu.VMEM((n_e, 1), F32),
        ],
        compiler_params=pltpu.CompilerParams(
            dimension_semantics=("arbitrary", "arbitrary"), vmem_limit_bytes=VMEM_LIMIT),
        name="mix",
    )(h, h, conv_w, row(conv_b), row(cln_g), row(cln_b), attn, w_out_bf16, x, row(ln1_g), row(ln1_b),
      wr_hi, wr_lo, router_bias.reshape(n_e, 1))


def _moe_kernel(ntrip_ref, x1_ref, rank_ref, gate_ref, wgu_ref, wd_ref, wsgu_ref, wsd_ref,
                g2_ref, b2_ref, o_ref, xb_ref, *, rows, d_e, d_s, per_step, per_trip, span):
    w = pl.program_id(0)
    step = pl.program_id(1)

    @pl.when(step == 0)
    def _():
        x1 = x1_ref[...]
        xb = x1.astype(BF16)
        xb_ref[...] = xb
        gu = _dot(xb, wsgu_ref[...])
        g = gu[:, :d_s]
        shared = _dot((g * jax.nn.sigmoid(g) * gu[:, d_s:]).astype(BF16), wsd_ref[...])
        o_ref[...] = ALPHA * x1 + shared

    n_spans = xb_ref.shape[0] // span

    def trip(c, carry, first):
        onehots, gathered = [], []
        slot_gates = [[None] * n_spans for _ in range(per_trip)]
        for s in range(n_spans):
            lo, hi = s * span, (s + 1) * span
            hits = []
            for k in range(per_trip):
                e = step * per_step + first + k
                rank = rank_ref[pl.ds(e, 1), lo:hi]
                hit = (lax.broadcasted_iota(I32, (rows, 1), 0) + c * rows) == rank
                hits.append(hit)
                slot_gates[k][s] = jnp.sum(jnp.where(hit, gate_ref[pl.ds(e, 1), lo:hi], 0.0), axis=1, keepdims=True)
            onehot = jnp.where(jnp.concatenate(hits, axis=0), 1.0, 0.0).astype(BF16)
            onehots.append(onehot)
            gathered.append(_dot(onehot, xb_ref[lo:hi, :]).astype(BF16))
        gated = []
        for k in range(per_trip):
            sl = slice(k * rows, (k + 1) * rows)
            xs = jnp.concatenate([gathered[s][sl] for s in range(n_spans)], axis=0)
            gu = _dot(xs, wgu_ref[first + k])
            g = gu[:, :d_e]
            y = _dot((g * jax.nn.sigmoid(g) * gu[:, d_e:]).astype(BF16), wd_ref[first + k])
            gated.append((y * jnp.concatenate(slot_gates[k], axis=0)).astype(BF16))
        for s in range(n_spans):
            sl = slice(s * rows, (s + 1) * rows)
            yg = jnp.concatenate([gated[k][sl] for k in range(per_trip)], axis=0)
            o_ref[s * span:(s + 1) * span, :] += lax.dot_general(
                onehots[s], yg, (((0,), (0,)), ((), ())), preferred_element_type=F32)
        return carry

    trips_per_step = per_step // per_trip
    for t in range(trips_per_step):
        n_trips = ntrip_ref[(w * pl.num_programs(1) + step) * trips_per_step + t]
        lax.fori_loop(0, n_trips, functools.partial(trip, first=t * per_trip), 0)

    @pl.when(step == pl.num_programs(1) - 1)
    def _():
        o_ref[...] = _layer_norm(o_ref[...], g2_ref[...], b2_ref[...])


def _moe(x1, rank, gates, counts, wgu, wd, ws_gate, ws_up, ws_down, ln2_g, ln2_b):
    N, D = x1.shape
    n_e, d_e, _ = wd.shape
    d_s = ws_down.shape[0]
    W = MOE_WINDOW
    n_win = N // W
    rows = EXPERT_ROWS
    span = MOE_SPAN
    per_step, per_trip = MOE_EXPERTS_PER_STEP, MOE_EXPERTS_PER_TRIP
    n_steps = n_e // per_step
    per_span = counts.reshape(N // span, span // MIX_ROWS, n_e)[:, -1]
    group = per_span.reshape(n_win, W // span, n_e // per_trip, per_trip)
    ntrip = jnp.max((group + rows - 1) // rows, axis=(1, 3)).astype(I32).reshape(-1)
    wsgu = jnp.concatenate([ws_gate, ws_up], axis=-1).astype(BF16)
    kern = functools.partial(_moe_kernel, rows=rows, d_e=d_e, d_s=d_s, per_step=per_step, per_trip=per_trip,
                             span=span)
    const = lambda shape: pl.BlockSpec(shape, lambda w, e, nt: (0,) * len(shape))
    return pl.pallas_call(
        kern,
        grid_spec=pltpu.PrefetchScalarGridSpec(
            num_scalar_prefetch=1,
            grid=(n_win, n_steps),
            in_specs=[
                pl.BlockSpec((W, D), lambda w, e, nt: (w, 0)),
                pl.BlockSpec((n_e, W), lambda w, e, nt: (0, w)),
                pl.BlockSpec((n_e, W), lambda w, e, nt: (0, w)),
                pl.BlockSpec((per_step, D, 2 * d_e), lambda w, e, nt: (e, 0, 0)),
                pl.BlockSpec((per_step, d_e, D), lambda w, e, nt: (e, 0, 0)),
                const(wsgu.shape), const((d_s, D)), const((1, D)), const((1, D)),
            ],
            out_specs=pl.BlockSpec((W, D), lambda w, e, nt: (w, 0)),
            scratch_shapes=[pltpu.VMEM((W, D), BF16)],
        ),
        out_shape=jax.ShapeDtypeStruct((N, D), F32),
        compiler_params=pltpu.CompilerParams(
            dimension_semantics=("arbitrary", "arbitrary"), vmem_limit_bytes=VMEM_LIMIT),
        name="moe",
    )(ntrip, x1, rank, gates, wgu, wd, wsgu, ws_down.astype(BF16),
      ln2_g.reshape(1, D), ln2_b.reshape(1, D))


def kernel(x, positions, w_in, lambda_q1, lambda_k1, lambda_q2, lambda_k2, subln_g, conv_w, conv_b, conv_ln_g, conv_ln_b, w_out, ln1_g, ln1_b, w_router, router_bias, w_exp_gate, w_exp_up, w_exp_down, w_sh_gate, w_sh_up, w_sh_down, ln2_g, ln2_b):
    B, S, D = x.shape
    d_attn = N_HEADS * HEAD_DIM
    d_conv = conv_w.shape[-1]
    for l in range(DEPTH):
        lambda_init = 0.8 - 0.6 * math.exp(-0.3 * l)
        qT, k, vT, h = _proj(x, positions, w_in[l].astype(BF16), d_attn, d_conv)
        attn, wgu, wd = _attention(qT, k, vT, lambda_q1[l], lambda_k1[l], lambda_q2[l], lambda_k2[l], subln_g[l],
                                   lambda_init, w_exp_gate[l], w_exp_up[l], w_exp_down[l])
        x1, rank, gates, counts = _mix(h, attn, x, conv_w[l], conv_b[l], conv_ln_g[l], conv_ln_b[l],
                                       w_out[l].astype(BF16), ln1_g[l], ln1_b[l], w_router[l], router_bias[l])
        out = _moe(x1, rank, gates, counts, wgu, wd, w_sh_gate[l], w_sh_up[l], w_sh_down[l], ln2_g[l], ln2_b[l])
        x = out.reshape(B, S, D)
    return x
```

```python
import functools
import math

import jax
import jax.numpy as jnp
from jax import lax
from jax.experimental import pallas as pl
from jax.experimental.pallas import tpu as pltpu

F32 = jnp.float32
BF16 = jnp.bfloat16
I32 = jnp.int32

N_HEADS = 4
HEAD_DIM = 128
SUB_DIM = 64
ROPE_DIM = 16
ROPE_THETA = 500000.0
CHUNK = 64
CONV_WIDTH = 31
N_EXPERTS = 64
TOP_K = 8
N_GROUPS = 8
GROUP_SIZE = N_EXPERTS // N_GROUPS
TOPK_GROUPS = 4
ROUTED_SCALE = 2.5
LN_EPS = 1e-5
DEPTH = 1
ALPHA = (2.0 * DEPTH) ** 0.25
LOG2_E = math.log2(math.e)

LANES = 128
SUBLANES = 8
VMEM_LIMIT = 56 * 1024 * 1024

PROJ_ROWS = 512
ATTN_BLOCK = 256
MIX_ROWS = 256
CONV_HALO = 32
CONV_ROW_CHUNK = 64
MOE_WINDOW = 1024
MOE_SPAN = 256
EXPERT_ROWS = 48
EXPERT_ROWS_WIDE = 64
MOE_EXPERTS_PER_STEP = 8
MOE_EXPERTS_PER_TRIP = 4


def _dot(a, b):
    return jnp.dot(a, b, preferred_element_type=F32)


def _layer_norm(z, g, b):
    mu = jnp.mean(z, axis=-1, keepdims=True)
    zc = z - mu
    var = jnp.mean(zc * zc, axis=-1, keepdims=True)
    return zc * lax.rsqrt(var + LN_EPS) * g + b


def _proj_kernel(x_ref, pos_ref, invf_ref, w_ref, qT_ref, k_ref, vT_ref, h_ref, *, d_attn, d_conv, blk):
    xb = x_ref[...].astype(BF16)
    tm = xb.shape[0]
    ang = pos_ref[...].astype(F32) * invf_ref[...]
    cos = jnp.cos(ang)
    sin = jnp.sin(ang)
    d = lax.broadcasted_iota(I32, (1, LANES), 1) & (SUB_DIM - 1)
    half = ROPE_DIM // 2
    c_mul = jnp.where(d < ROPE_DIM, cos, 1.0)
    s_lo = jnp.where(d < half, -sin, 0.0)
    s_hi = jnp.where((d >= half) & (d < ROPE_DIM), sin, 0.0)

    def rope(t):
        outs = []
        for c in range(t.shape[1] // LANES):
            ts = t[:, c * LANES:(c + 1) * LANES]
            outs.append(ts * c_mul + pltpu.roll(ts, LANES - half, 1) * s_lo + pltpu.roll(ts, half, 1) * s_hi)
        return jnp.concatenate(outs, axis=1)

    q = rope(_dot(xb, w_ref[:, 0:d_attn])) * (SUB_DIM ** -0.5 * LOG2_E)
    for c in range(tm // blk):
        qT_ref[0, c] = q[c * blk:(c + 1) * blk, :].T.astype(BF16)
    k = rope(_dot(xb, w_ref[:, d_attn:2 * d_attn]))
    k_ref[0] = k.astype(BF16)
    v = _dot(xb, w_ref[:, 2 * d_attn:3 * d_attn])
    for c in range(tm // blk):
        vT_ref[0, c] = v[c * blk:(c + 1) * blk, :].T.astype(BF16)
    a = _dot(xb, w_ref[:, 3 * d_attn:3 * d_attn + d_conv])
    gate = _dot(xb, w_ref[:, 3 * d_attn + d_conv:3 * d_attn + 2 * d_conv])
    h_ref[0] = a * jax.nn.sigmoid(gate)


def _proj(x, positions, w_in_bf16, d_attn, d_conv):
    B, S, D = x.shape
    tm, blk = PROJ_ROWS, ATTN_BLOCK
    nb = S // blk
    inv_freq = ROPE_THETA ** (-jnp.arange(0, ROPE_DIM, 2, dtype=F32) / ROPE_DIM)
    sub = jnp.concatenate([inv_freq, inv_freq, jnp.zeros((SUB_DIM - ROPE_DIM,), F32)])
    invf = jnp.tile(sub, LANES // SUB_DIM).reshape(1, LANES)
    d_in = w_in_bf16.shape[1]
    kern = functools.partial(_proj_kernel, d_attn=d_attn, d_conv=d_conv, blk=blk)
    return pl.pallas_call(
        kern,
        grid=(B, S // tm),
        in_specs=[
            pl.BlockSpec((None, tm, D), lambda b, i: (b, i, 0)),
            pl.BlockSpec((None, tm, 1), lambda b, i: (b, i, 0)),
            pl.BlockSpec((1, LANES), lambda b, i: (0, 0)),
            pl.BlockSpec((D, d_in), lambda b, i: (0, 0)),
        ],
        out_specs=[
            pl.BlockSpec((1, tm // blk, d_attn, blk), lambda b, i: (b, i, 0, 0)),
            pl.BlockSpec((1, tm, d_attn), lambda b, i: (b, i, 0)),
            pl.BlockSpec((1, tm // blk, d_attn, blk), lambda b, i: (b, i, 0, 0)),
            pl.BlockSpec((1, tm, d_conv), lambda b, i: (b, i, 0)),
        ],
        out_shape=[
            jax.ShapeDtypeStruct((B, nb, d_attn, blk), BF16),
            jax.ShapeDtypeStruct((B, S, d_attn), BF16),
            jax.ShapeDtypeStruct((B, nb, d_attn, blk), BF16),
            jax.ShapeDtypeStruct((B, S, d_conv), F32),
        ],
        compiler_params=pltpu.CompilerParams(
            dimension_semantics=("parallel", "parallel"), vmem_limit_bytes=VMEM_LIMIT),
        name="proj",
    )(x, positions.reshape(B, S, 1), invf, w_in_bf16)


def _attn_kernel(lq1_ref, lk1_ref, lq2_ref, lk2_ref, g_ref, qT_ref, k_ref, vT_ref, wg_ref, wu_ref, wd_ref,
                 o_ref, wgu_out, wd_out, s_a, s_b, p_a, p_b, acc_ref, m_ref, *, blk, lambda_init):
    i = pl.program_id(1)
    d_e = wg_ref.shape[-1]
    wgu_out[:, :, :d_e] = wg_ref[...].astype(wgu_out.dtype)
    wgu_out[:, :, d_e:] = wu_ref[...].astype(wgu_out.dtype)
    wd_out[...] = wd_ref[...].astype(wd_out.dtype)
    lam = (jnp.exp(jnp.sum(lq1_ref[...] * lk1_ref[...], axis=-1, keepdims=True))
           - jnp.exp(jnp.sum(lq2_ref[...] * lk2_ref[...], axis=-1, keepdims=True)) + lambda_init)
    row = lax.broadcasted_iota(I32, (HEAD_DIM, 1), 0)
    ones = jnp.ones((2 * SUBLANES, blk), BF16)
    heads = range(N_HEADS)

    def head_cols(h):
        return slice(h * HEAD_DIM, (h + 1) * HEAD_DIM)

    def q_both(h):
        qT = qT_ref[0, 0, head_cols(h), :]
        zero = jnp.zeros_like(qT)
        return jnp.concatenate([jnp.where(row < SUB_DIM, qT, zero), jnp.where(row >= SUB_DIM, qT, zero)], axis=1)

    def scores_into(h, j, s_ref):
        kb = k_ref[0, pl.ds(pl.multiple_of(j * blk, blk), blk), head_cols(h)]
        s_ref[h] = _dot(kb, q_both(h))

    def weighted_values(h, j, p):
        return _dot(jnp.concatenate([vT_ref[0, j, head_cols(h), :], ones], axis=0), p)

    def probs(h, s):
        m = m_ref[h]
        m_new = jnp.maximum(m, jnp.max(s, axis=0, keepdims=True))
        m_ref[h] = m_new
        return jnp.exp2(m - m_new), jnp.exp2(s - m_new).astype(BF16)

    def step(j, s_cur, s_nxt, p_prev, p_cur):
        for h in heads:
            scores_into(h, j + 1, s_nxt)
            pv_prev = weighted_values(h, jnp.maximum(j - 1, 0), p_prev[h])
            alpha, p = probs(h, s_cur[h])
            p_cur[h] = p
            acc_ref[h] = alpha * (acc_ref[h] + pv_prev)

    def finish(s_cur, p_prev):
        kc = lax.broadcasted_iota(I32, (blk, 1), 0) // CHUNK
        qc = (lax.broadcasted_iota(I32, (1, 2 * blk), 1) & (blk - 1)) // CHUNK
        for h in heads:
            pv_prev = weighted_values(h, jnp.maximum(i - 1, 0), p_prev[h])
            alpha, p = probs(h, jnp.where(kc <= qc, s_cur[h], -jnp.inf))
            acc = alpha * (acc_ref[h] + pv_prev) + weighted_values(h, i, p)
            o = acc[:HEAD_DIM] / acc[HEAD_DIM:HEAD_DIM + 1]
            o = o[:, :blk] - lam * o[:, blk:]
            o = o * lax.rsqrt(jnp.mean(o * o, axis=0, keepdims=True) + LN_EPS)
            o = o * g_ref[...] * (1.0 - lambda_init)
            o_ref[0, :, head_cols(h)] = o.T.astype(o_ref.dtype)

    m_ref[...] = jnp.full(m_ref.shape, -jnp.inf, F32)
    acc_ref[...] = jnp.zeros(acc_ref.shape, F32)
    p_b[...] = jnp.zeros(p_b.shape, BF16)
    for h in heads:
        scores_into(h, 0, s_a)

    def pair(jj, c):
        step(2 * jj, s_a, s_b, p_b, p_a)
        step(2 * jj + 1, s_b, s_a, p_a, p_b)
        return c

    lax.fori_loop(0, i // 2, pair, 0)

    @pl.when(i % 2 == 1)
    def _():
        step(i - 1, s_a, s_b, p_b, p_a)
        finish(s_b, p_a)

    @pl.when(i % 2 == 0)
    def _():
        finish(s_a, p_b)


def _attention(qT, k, vT, lq1, lk1, lq2, lk2, sub_g, lambda_init, w_gate, w_up, w_down):
    B, nb, d_attn, blk = qT.shape
    S = nb * blk
    n_e, D, d_e = w_gate.shape
    per_step = n_e // (B * nb)
    assert per_step * B * nb == n_e
    kern = functools.partial(_attn_kernel, blk=blk, lambda_init=lambda_init)
    vec = pl.BlockSpec((1, SUB_DIM), lambda b, i: (0, 0))
    w_slice = lambda shape: pl.BlockSpec((per_step,) + shape, lambda b, i: (b * nb + i, 0, 0))
    return pl.pallas_call(
        kern,
        grid=(B, nb),
        in_specs=[
            vec, vec, vec, vec,
            pl.BlockSpec((HEAD_DIM, 1), lambda b, i: (0, 0)),
            pl.BlockSpec((1, 1, d_attn, blk), lambda b, i: (b, i, 0, 0)),
            pl.BlockSpec((1, S, d_attn), lambda b, i: (b, 0, 0)),
            pl.BlockSpec((1, nb, d_attn, blk), lambda b, i: (b, 0, 0, 0)),
            w_slice((D, d_e)), w_slice((D, d_e)), w_slice((d_e, D)),
        ],
        out_specs=[
            pl.BlockSpec((1, blk, d_attn), lambda b, i: (b, i, 0)),
            w_slice((D, 2 * d_e)), w_slice((d_e, D)),
        ],
        out_shape=[
            jax.ShapeDtypeStruct((B, S, d_attn), BF16),
            jax.ShapeDtypeStruct((n_e, D, 2 * d_e), BF16),
            jax.ShapeDtypeStruct((n_e, d_e, D), BF16),
        ],
        scratch_shapes=[
            pltpu.VMEM((N_HEADS, blk, 2 * blk), F32), pltpu.VMEM((N_HEADS, blk, 2 * blk), F32),
            pltpu.VMEM((N_HEADS, blk, 2 * blk), BF16), pltpu.VMEM((N_HEADS, blk, 2 * blk), BF16),
            pltpu.VMEM((N_HEADS, HEAD_DIM + 2 * SUBLANES, 2 * blk), F32),
            pltpu.VMEM((N_HEADS, 1, 2 * blk), F32),
        ],
        compiler_params=pltpu.CompilerParams(
            dimension_semantics=("parallel", "arbitrary"), vmem_limit_bytes=VMEM_LIMIT),
        name="attn",
    )(lq1.reshape(1, SUB_DIM), lk1.reshape(1, SUB_DIM), lq2.reshape(1, SUB_DIM), lk2.reshape(1, SUB_DIM),
      sub_g.reshape(HEAD_DIM, 1), qT, k, vT, w_gate, w_up, w_down)


def _route(scores, bias):
    n_e, tm = scores.shape
    neg = -jnp.inf
    biased = scores + bias
    g3 = biased.reshape(N_GROUPS, GROUP_SIZE, tm)
    io3 = lax.broadcasted_iota(I32, g3.shape, 1)
    m1 = jnp.max(g3, axis=1, keepdims=True)
    first = jnp.min(jnp.where(g3 == m1, io3, GROUP_SIZE), axis=1, keepdims=True)
    m2 = jnp.max(jnp.where(io3 == first, neg, g3), axis=1, keepdims=True)
    grp_score = (m1 + m2).reshape(N_GROUPS, tm)
    gio = lax.broadcasted_iota(I32, (N_GROUPS, tm), 0)
    grp_sel = jnp.zeros((N_GROUPS, tm), jnp.bool_)
    cur = grp_score
    for _ in range(TOPK_GROUPS):
        mx = jnp.max(cur, axis=0, keepdims=True)
        f = jnp.min(jnp.where(cur == mx, gio, N_GROUPS), axis=0, keepdims=True)
        hit = gio == f
        grp_sel = grp_sel | hit
        cur = jnp.where(hit, neg, cur)
    grp_f = jnp.where(grp_sel, 1.0, 0.0).astype(F32)
    exp_mask = jnp.broadcast_to(grp_f.reshape(N_GROUPS, 1, tm), (N_GROUPS, GROUP_SIZE, tm)).reshape(n_e, tm) > 0.5
    eio = lax.broadcasted_iota(I32, (n_e, tm), 0)
    cur = jnp.where(exp_mask, biased, neg)
    sel = jnp.zeros((n_e, tm), jnp.bool_)
    for _ in range(TOP_K):
        mx = jnp.max(cur, axis=0, keepdims=True)
        f = jnp.min(jnp.where(cur == mx, eio, n_e), axis=0, keepdims=True)
        hit = eio == f
        sel = sel | hit
        cur = jnp.where(hit, neg, cur)
    denom = jnp.sum(jnp.where(sel, scores, 0.0), axis=0, keepdims=True)
    gates = jnp.where(sel, scores / denom * ROUTED_SCALE, 0.0)
    return sel, gates


def _mix_kernel(h_ref, halo_ref, cw_ref, cb_ref, cg_ref, cbeta_ref, attn_ref, wo_ref, x_ref, g1_ref, b1_ref,
                wrh_ref, wrl_ref, rb_ref,
                x1_ref, rank_ref, gate_ref, cnt_ref,
                buf, shifted, cbuf, carry, *, tm, d_attn, tiles_per_span):
    tile = pl.program_id(0) * pl.num_programs(1) + pl.program_id(1)

    @pl.when(tile % tiles_per_span == 0)
    def _():
        carry[...] = jnp.zeros_like(carry)

    @pl.when(pl.program_id(1) == 0)
    def _():
        buf[0:CONV_HALO, :] = jnp.zeros((CONV_HALO, buf.shape[1]), F32)

    @pl.when(pl.program_id(1) != 0)
    def _():
        buf[0:CONV_HALO, :] = halo_ref[...]

    buf[CONV_HALO:, :] = h_ref[...]
    d_conv = buf.shape[1]
    for s in range(1, SUBLANES):
        shifted[s - 1] = buf[s:s + shifted.shape[1], :]
    off = CONV_HALO - (CONV_WIDTH - 1)
    for c in range(d_conv // LANES):
        cs = slice(c * LANES, (c + 1) * LANES)
        for r in range(tm // CONV_ROW_CHUNK):
            acc = jnp.zeros((CONV_ROW_CHUNK, LANES), F32)
            for j in range(CONV_WIDTH):
                s = (off + j) % SUBLANES
                start = r * CONV_ROW_CHUNK + off + j - s
                src = buf if s == 0 else shifted.at[s - 1]
                acc = acc + cw_ref[j:j + 1, cs] * src[start:start + CONV_ROW_CHUNK, cs]
            cbuf[r * CONV_ROW_CHUNK:(r + 1) * CONV_ROW_CHUNK, cs] = acc + cb_ref[:, cs]
    conv = _layer_norm(cbuf[...], cg_ref[...], cbeta_ref[...])
    conv = conv * jax.nn.sigmoid(conv)

    mix = _dot(attn_ref[...], wo_ref[0:d_attn, :]) + _dot(conv.astype(BF16), wo_ref[d_attn:, :])
    x1 = _layer_norm(ALPHA * x_ref[...] + mix, g1_ref[...], b1_ref[...])
    x1_ref[...] = x1

    xh = x1.astype(BF16)
    xl = (x1 - xh.astype(F32)).astype(BF16)
    logits = _dot(xh, wrh_ref[...]) + _dot(xl, wrh_ref[...]) + _dot(xh, wrl_ref[...])
    scores = jax.nn.sigmoid(logits).T
    sel, gates = _route(scores, rb_ref[...])
    gate_ref[...] = gates

    t_row = lax.broadcasted_iota(I32, (tm, tm), 0)
    t_col = lax.broadcasted_iota(I32, (tm, tm), 1)
    before = jnp.where(t_row < t_col, 1.0, 0.0).astype(BF16)
    sel_f = jnp.where(sel, 1.0, 0.0).astype(F32)
    rank = _dot(sel_f.astype(BF16), before) + carry[...]
    rank_ref[...] = jnp.where(sel, rank, -1.0).astype(I32)
    carry[...] = carry[...] + jnp.sum(sel_f, axis=1, keepdims=True)
    cnt_ref[...] = carry[...].astype(I32)


def _mix(h, attn, x, conv_w, conv_b, cln_g, cln_b, w_out_bf16, ln1_g, ln1_b, w_router, router_bias):
    B, S, D = x.shape
    d_conv = h.shape[-1]
    d_attn = attn.shape[-1]
    tm = MIX_ROWS
    nt = S // tm
    N = B * S
    n_e = w_router.shape[-1]
    wr_hi = w_router.astype(BF16)
    wr_lo = (w_router - wr_hi.astype(F32)).astype(BF16)
    halo_blocks = tm // CONV_HALO
    row = lambda a: a.reshape(1, -1)
    const = lambda shape: pl.BlockSpec(shape, lambda b, i: (0,) * len(shape))
    tiles_per_span = MOE_SPAN // tm
    kern = functools.partial(_mix_kernel, tm=tm, d_attn=d_attn, tiles_per_span=tiles_per_span)
    return pl.pallas_call(
        kern,
        grid=(B, nt),
        in_specs=[
            pl.BlockSpec((None, tm, d_conv), lambda b, i: (b, i, 0)),
            pl.BlockSpec((None, CONV_HALO, d_conv), lambda b, i: (b, jnp.maximum(i * halo_blocks - 1, 0), 0)),
            const(conv_w.shape), const((1, d_conv)), const((1, d_conv)), const((1, d_conv)),
            pl.BlockSpec((None, tm, d_attn), lambda b, i: (b, i, 0)),
            const(w_out_bf16.shape),
            pl.BlockSpec((None, tm, D), lambda b, i: (b, i, 0)),
            const((1, D)), const((1, D)),
            const(wr_hi.shape), const(wr_lo.shape), const((n_e, 1)),
        ],
        out_specs=[
            pl.BlockSpec((tm, D), lambda b, i: (b * nt + i, 0)),
            pl.BlockSpec((n_e, tm), lambda b, i: (0, b * nt + i)),
            pl.BlockSpec((n_e, tm), lambda b, i: (0, b * nt + i)),
            pl.BlockSpec((None, n_e, 1), lambda b, i: (b * nt + i, 0, 0)),
        ],
        out_shape=[
            jax.ShapeDtypeStruct((N, D), F32),
            jax.ShapeDtypeStruct((n_e, N), I32),
            jax.ShapeDtypeStruct((n_e, N), F32),
            jax.ShapeDtypeStruct((B * nt, n_e, 1), I32),
        ],
        scratch_shapes=[
            pltpu.VMEM((tm + CONV_HALO, d_conv), F32),
            pltpu.VMEM((SUBLANES - 1, tm + CONV_HALO - SUBLANES, d_conv), F32),
            pltpu.VMEM((tm, d_conv), F32),
            pltpu.VMEM((n_e, 1), F32),
        ],
        compiler_params=pltpu.CompilerParams(
            dimension_semantics=("arbitrary", "arbitrary"), vmem_limit_bytes=VMEM_LIMIT),
        name="mix",
    )(h, h, conv_w, row(conv_b), row(cln_g), row(cln_b), attn, w_out_bf16, x, row(ln1_g), row(ln1_b),
      wr_hi, wr_lo, router_bias.reshape(n_e, 1))


def _moe_kernel(ntrip_ref, ntrip_wide_ref, x1_ref, rank_ref, gate_ref, wgu_ref, wd_ref, wsgu_ref, wsd_ref,
                g2_ref, b2_ref, o_ref, xb_ref, *, rows, rows_wide, d_e, d_s, per_step, per_trip, span):
    w = pl.program_id(0)
    step = pl.program_id(1)

    @pl.when(step == 0)
    def _():
        x1 = x1_ref[...]
        xb = x1.astype(BF16)
        xb_ref[...] = xb
        gu = _dot(xb, wsgu_ref[...])
        g = gu[:, :d_s]
        shared = _dot((g * jax.nn.sigmoid(g) * gu[:, d_s:]).astype(BF16), wsd_ref[...])
        o_ref[...] = ALPHA * x1 + shared

    n_spans = xb_ref.shape[0] // span

    def trip(c, carry, first, rows):
        onehots, gathered = [], []
        slot_gates = [[None] * n_spans for _ in range(per_trip)]
        for s in range(n_spans):
            lo, hi = s * span, (s + 1) * span
            hits = []
            for k in range(per_trip):
                e = step * per_step + first + k
                rank = rank_ref[pl.ds(e, 1), lo:hi]
                hit = (lax.broadcasted_iota(I32, (rows, 1), 0) + c * rows) == rank
                hits.append(hit)
                slot_gates[k][s] = jnp.sum(jnp.where(hit, gate_ref[pl.ds(e, 1), lo:hi], 0.0), axis=1, keepdims=True)
            onehot = jnp.where(jnp.concatenate(hits, axis=0), 1.0, 0.0).astype(BF16)
            onehots.append(onehot)
            gathered.append(_dot(onehot, xb_ref[lo:hi, :]).astype(BF16))
        gated = []
        for k in range(per_trip):
            sl = slice(k * rows, (k + 1) * rows)
            xs = jnp.concatenate([gathered[s][sl] for s in range(n_spans)], axis=0)
            gu = _dot(xs, wgu_ref[first + k])
            g = gu[:, :d_e]
            y = _dot((g * jax.nn.sigmoid(g) * gu[:, d_e:]).astype(BF16), wd_ref[first + k])
            gated.append((y * jnp.concatenate(slot_gates[k], axis=0)).astype(BF16))
        for s in range(n_spans):
            sl = slice(s * rows, (s + 1) * rows)
            yg = jnp.concatenate([gated[k][sl] for k in range(per_trip)], axis=0)
            o_ref[s * span:(s + 1) * span, :] += lax.dot_general(
                onehots[s], yg, (((0,), (0,)), ((), ())), preferred_element_type=F32)
        return carry

    trips_per_step = per_step // per_trip
    for t in range(trips_per_step):
        q = (w * pl.num_programs(1) + step) * trips_per_step + t
        lax.fori_loop(0, ntrip_ref[q], functools.partial(trip, first=t * per_trip, rows=rows), 0)
        lax.fori_loop(0, ntrip_wide_ref[q], functools.partial(trip, first=t * per_trip, rows=rows_wide), 0)

    @pl.when(step == pl.num_programs(1) - 1)
    def _():
        o_ref[...] = _layer_norm(o_ref[...], g2_ref[...], b2_ref[...])


def _moe(x1, rank, gates, counts, wgu, wd, ws_gate, ws_up, ws_down, ln2_g, ln2_b):
    N, D = x1.shape
    n_e, d_e, _ = wd.shape
    d_s = ws_down.shape[0]
    W = MOE_WINDOW
    n_win = N // W
    rows, rows_wide = EXPERT_ROWS, EXPERT_ROWS_WIDE
    span = MOE_SPAN
    per_step, per_trip = MOE_EXPERTS_PER_STEP, MOE_EXPERTS_PER_TRIP
    n_steps = n_e // per_step
    per_span = counts.reshape(N // span, span // MIX_ROWS, n_e)[:, -1]
    fullest = jnp.max(per_span.reshape(n_win, W // span, n_e // per_trip, per_trip), axis=(1, 3)).reshape(-1)
    narrow = fullest <= rows
    ntrip = jnp.where(narrow, jnp.minimum(fullest, 1), 0).astype(I32)
    ntrip_wide = jnp.where(narrow, 0, (fullest + rows_wide - 1) // rows_wide).astype(I32)
    wsgu = jnp.concatenate([ws_gate, ws_up], axis=-1).astype(BF16)
    kern = functools.partial(_moe_kernel, rows=rows, rows_wide=rows_wide, d_e=d_e, d_s=d_s, per_step=per_step,
                             per_trip=per_trip, span=span)
    const = lambda shape: pl.BlockSpec(shape, lambda w, e, nt, ntw: (0,) * len(shape))
    return pl.pallas_call(
        kern,
        grid_spec=pltpu.PrefetchScalarGridSpec(
            num_scalar_prefetch=2,
            grid=(n_win, n_steps),
            in_specs=[
                pl.BlockSpec((W, D), lambda w, e, nt, ntw: (w, 0)),
                pl.BlockSpec((n_e, W), lambda w, e, nt, ntw: (0, w)),
                pl.BlockSpec((n_e, W), lambda w, e, nt, ntw: (0, w)),
                pl.BlockSpec((per_step, D, 2 * d_e), lambda w, e, nt, ntw: (e, 0, 0)),
                pl.BlockSpec((per_step, d_e, D), lambda w, e, nt, ntw: (e, 0, 0)),
                const(wsgu.shape), const((d_s, D)), const((1, D)), const((1, D)),
            ],
            out_specs=pl.BlockSpec((W, D), lambda w, e, nt, ntw: (w, 0)),
            scratch_shapes=[pltpu.VMEM((W, D), BF16)],
        ),
        out_shape=jax.ShapeDtypeStruct((N, D), F32),
        compiler_params=pltpu.CompilerParams(
            dimension_semantics=("arbitrary", "arbitrary"), vmem_limit_bytes=VMEM_LIMIT),
        name="moe",
    )(ntrip, ntrip_wide, x1, rank, gates, wgu, wd, wsgu, ws_down.astype(BF16),
      ln2_g.reshape(1, D), ln2_b.reshape(1, D))


def kernel(x, positions, w_in, lambda_q1, lambda_k1, lambda_q2, lambda_k2, subln_g, conv_w, conv_b, conv_ln_g, conv_ln_b, w_out, ln1_g, ln1_b, w_router, router_bias, w_exp_gate, w_exp_up, w_exp_down, w_sh_gate, w_sh_up, w_sh_down, ln2_g, ln2_b):
    B, S, D = x.shape
    d_attn = N_HEADS * HEAD_DIM
    d_conv = conv_w.shape[-1]
    for l in range(DEPTH):
        lambda_init = 0.8 - 0.6 * math.exp(-0.3 * l)
        qT, k, vT, h = _proj(x, positions, w_in[l].astype(BF16), d_attn, d_conv)
        attn, wgu, wd = _attention(qT, k, vT, lambda_q1[l], lambda_k1[l], lambda_q2[l], lambda_k2[l], subln_g[l],
                                   lambda_init, w_exp_gate[l], w_exp_up[l], w_exp_down[l])
        x1, rank, gates, counts = _mix(h, attn, x, conv_w[l], conv_b[l], conv_ln_g[l], conv_ln_b[l],
                                       w_out[l].astype(BF16), ln1_g[l], ln1_b[l], w_router[l], router_bias[l])
        out = _moe(x1, rank, gates, counts, wgu, wd, w_sh_gate[l], w_sh_up[l], w_sh_down[l], ln2_g[l], ln2_b[l])
        x = out.reshape(B, S, D)
    return x
```

```python
import functools
import math

import jax
import jax.numpy as jnp
from jax import lax
from jax.experimental import pallas as pl
from jax.experimental.pallas import tpu as pltpu

F32 = jnp.float32
BF16 = jnp.bfloat16
I32 = jnp.int32

N_HEADS = 4
HEAD_DIM = 128
SUB_DIM = 64
ROPE_DIM = 16
ROPE_THETA = 500000.0
CHUNK = 64
CONV_WIDTH = 31
N_EXPERTS = 64
TOP_K = 8
N_GROUPS = 8
GROUP_SIZE = N_EXPERTS // N_GROUPS
TOPK_GROUPS = 4
ROUTED_SCALE = 2.5
LN_EPS = 1e-5
DEPTH = 1
ALPHA = (2.0 * DEPTH) ** 0.25
LOG2_E = math.log2(math.e)

LANES = 128
SUBLANES = 8
VMEM_LIMIT = 56 * 1024 * 1024

PROJ_ROWS = 512
ATTN_BLOCK = 256
MIX_ROWS = 256
CONV_HALO = 32
CONV_ROW_CHUNK = 64
MOE_WINDOW = 1024
MOE_SPAN = 256
EXPERT_ROW_TIERS = (40, 48, 56, 64)
MOE_EXPERTS_PER_STEP = 8
MOE_EXPERTS_PER_TRIP = 4


def _dot(a, b):
    return jnp.dot(a, b, preferred_element_type=F32)


def _layer_norm(z, g, b):
    mu = jnp.mean(z, axis=-1, keepdims=True)
    zc = z - mu
    var = jnp.mean(zc * zc, axis=-1, keepdims=True)
    return zc * lax.rsqrt(var + LN_EPS) * g + b


def _proj_kernel(x_ref, pos_ref, invf_ref, w_ref, qT_ref, k_ref, vT_ref, h_ref, *, d_attn, d_conv, blk):
    xb = x_ref[...].astype(BF16)
    tm = xb.shape[0]
    ang = pos_ref[...].astype(F32) * invf_ref[...]
    cos = jnp.cos(ang)
    sin = jnp.sin(ang)
    d = lax.broadcasted_iota(I32, (1, LANES), 1) & (SUB_DIM - 1)
    half = ROPE_DIM // 2
    c_mul = jnp.where(d < ROPE_DIM, cos, 1.0)
    s_lo = jnp.where(d < half, -sin, 0.0)
    s_hi = jnp.where((d >= half) & (d < ROPE_DIM), sin, 0.0)

    def rope(t):
        outs = []
        for c in range(t.shape[1] // LANES):
            ts = t[:, c * LANES:(c + 1) * LANES]
            outs.append(ts * c_mul + pltpu.roll(ts, LANES - half, 1) * s_lo + pltpu.roll(ts, half, 1) * s_hi)
        return jnp.concatenate(outs, axis=1)

    q = rope(_dot(xb, w_ref[:, 0:d_attn])) * (SUB_DIM ** -0.5 * LOG2_E)
    for c in range(tm // blk):
        qT_ref[0, c] = q[c * blk:(c + 1) * blk, :].T.astype(BF16)
    k = rope(_dot(xb, w_ref[:, d_attn:2 * d_attn]))
    k_ref[0] = k.astype(BF16)
    v = _dot(xb, w_ref[:, 2 * d_attn:3 * d_attn])
    for c in range(tm // blk):
        vT_ref[0, c] = v[c * blk:(c + 1) * blk, :].T.astype(BF16)
    a = _dot(xb, w_ref[:, 3 * d_attn:3 * d_attn + d_conv])
    gate = _dot(xb, w_ref[:, 3 * d_attn + d_conv:3 * d_attn + 2 * d_conv])
    h_ref[0] = a * jax.nn.sigmoid(gate)


def _proj(x, positions, w_in_bf16, d_attn, d_conv):
    B, S, D = x.shape
    tm, blk = PROJ_ROWS, ATTN_BLOCK
    nb = S // blk
    inv_freq = ROPE_THETA ** (-jnp.arange(0, ROPE_DIM, 2, dtype=F32) / ROPE_DIM)
    sub = jnp.concatenate([inv_freq, inv_freq, jnp.zeros((SUB_DIM - ROPE_DIM,), F32)])
    invf = jnp.tile(sub, LANES // SUB_DIM).reshape(1, LANES)
    d_in = w_in_bf16.shape[1]
    kern = functools.partial(_proj_kernel, d_attn=d_attn, d_conv=d_conv, blk=blk)
    return pl.pallas_call(
        kern,
        grid=(B, S // tm),
        in_specs=[
            pl.BlockSpec((None, tm, D), lambda b, i: (b, i, 0)),
            pl.BlockSpec((None, tm, 1), lambda b, i: (b, i, 0)),
            pl.BlockSpec((1, LANES), lambda b, i: (0, 0)),
            pl.BlockSpec((D, d_in), lambda b, i: (0, 0)),
        ],
        out_specs=[
            pl.BlockSpec((1, tm // blk, d_attn, blk), lambda b, i: (b, i, 0, 0)),
            pl.BlockSpec((1, tm, d_attn), lambda b, i: (b, i, 0)),
            pl.BlockSpec((1, tm // blk, d_attn, blk), lambda b, i: (b, i, 0, 0)),
            pl.BlockSpec((1, tm, d_conv), lambda b, i: (b, i, 0)),
        ],
        out_shape=[
            jax.ShapeDtypeStruct((B, nb, d_attn, blk), BF16),
            jax.ShapeDtypeStruct((B, S, d_attn), BF16),
            jax.ShapeDtypeStruct((B, nb, d_attn, blk), BF16),
            jax.ShapeDtypeStruct((B, S, d_conv), F32),
        ],
        compiler_params=pltpu.CompilerParams(
            dimension_semantics=("parallel", "parallel"), vmem_limit_bytes=VMEM_LIMIT),
        name="proj",
    )(x, positions.reshape(B, S, 1), invf, w_in_bf16)


def _attn_kernel(lq1_ref, lk1_ref, lq2_ref, lk2_ref, g_ref, qT_ref, k_ref, vT_ref, wg_ref, wu_ref, wd_ref,
                 o_ref, wgu_out, wd_out, s_a, s_b, p_a, p_b, acc_ref, m_ref, *, blk, lambda_init):
    i = pl.program_id(1)
    d_e = wg_ref.shape[-1]
    wgu_out[:, :, :d_e] = wg_ref[...].astype(wgu_out.dtype)
    wgu_out[:, :, d_e:] = wu_ref[...].astype(wgu_out.dtype)
    wd_out[...] = wd_ref[...].astype(wd_out.dtype)
    lam = (jnp.exp(jnp.sum(lq1_ref[...] * lk1_ref[...], axis=-1, keepdims=True))
           - jnp.exp(jnp.sum(lq2_ref[...] * lk2_ref[...], axis=-1, keepdims=True)) + lambda_init)
    row = lax.broadcasted_iota(I32, (HEAD_DIM, 1), 0)
    ones = jnp.ones((2 * SUBLANES, blk), BF16)
    heads = range(N_HEADS)

    def head_cols(h):
        return slice(h * HEAD_DIM, (h + 1) * HEAD_DIM)

    def q_both(h):
        qT = qT_ref[0, 0, head_cols(h), :]
        zero = jnp.zeros_like(qT)
        return jnp.concatenate([jnp.where(row < SUB_DIM, qT, zero), jnp.where(row >= SUB_DIM, qT, zero)], axis=1)

    def scores_into(h, j, s_ref):
        kb = k_ref[0, pl.ds(pl.multiple_of(j * blk, blk), blk), head_cols(h)]
        s_ref[h] = _dot(kb, q_both(h))

    def weighted_values(h, j, p):
        return _dot(jnp.concatenate([vT_ref[0, j, head_cols(h), :], ones], axis=0), p)

    def probs(h, s):
        m = m_ref[h]
        m_new = jnp.maximum(m, jnp.max(s, axis=0, keepdims=True))
        m_ref[h] = m_new
        return jnp.exp2(m - m_new), jnp.exp2(s - m_new).astype(BF16)

    def step(j, s_cur, s_nxt, p_prev, p_cur):
        for h in heads:
            scores_into(h, j + 1, s_nxt)
            pv_prev = weighted_values(h, jnp.maximum(j - 1, 0), p_prev[h])
            alpha, p = probs(h, s_cur[h])
            p_cur[h] = p
            acc_ref[h] = alpha * (acc_ref[h] + pv_prev)

    def finish(s_cur, p_prev):
        kc = lax.broadcasted_iota(I32, (blk, 1), 0) // CHUNK
        qc = (lax.broadcasted_iota(I32, (1, 2 * blk), 1) & (blk - 1)) // CHUNK
        for h in heads:
            pv_prev = weighted_values(h, jnp.maximum(i - 1, 0), p_prev[h])
            alpha, p = probs(h, jnp.where(kc <= qc, s_cur[h], -jnp.inf))
            acc = alpha * (acc_ref[h] + pv_prev) + weighted_values(h, i, p)
            o = acc[:HEAD_DIM] / acc[HEAD_DIM:HEAD_DIM + 1]
            o = o[:, :blk] - lam * o[:, blk:]
            o = o * lax.rsqrt(jnp.mean(o * o, axis=0, keepdims=True) + LN_EPS)
            o = o * g_ref[...] * (1.0 - lambda_init)
            o_ref[0, :, head_cols(h)] = o.T.astype(o_ref.dtype)

    m_ref[...] = jnp.full(m_ref.shape, -jnp.inf, F32)
    acc_ref[...] = jnp.zeros(acc_ref.shape, F32)
    p_b[...] = jnp.zeros(p_b.shape, BF16)
    for h in heads:
        scores_into(h, 0, s_a)

    def pair(jj, c):
        step(2 * jj, s_a, s_b, p_b, p_a)
        step(2 * jj + 1, s_b, s_a, p_a, p_b)
        return c

    lax.fori_loop(0, i // 2, pair, 0)

    @pl.when(i % 2 == 1)
    def _():
        step(i - 1, s_a, s_b, p_b, p_a)
        finish(s_b, p_a)

    @pl.when(i % 2 == 0)
    def _():
        finish(s_a, p_b)


def _attention(qT, k, vT, lq1, lk1, lq2, lk2, sub_g, lambda_init, w_gate, w_up, w_down):
    B, nb, d_attn, blk = qT.shape
    S = nb * blk
    n_e, D, d_e = w_gate.shape
    per_step = n_e // (B * nb)
    assert per_step * B * nb == n_e
    kern = functools.partial(_attn_kernel, blk=blk, lambda_init=lambda_init)
    vec = pl.BlockSpec((1, SUB_DIM), lambda b, i: (0, 0))
    w_slice = lambda shape: pl.BlockSpec((per_step,) + shape, lambda b, i: (b * nb + i, 0, 0))
    return pl.pallas_call(
        kern,
        grid=(B, nb),
        in_specs=[
            vec, vec, vec, vec,
            pl.BlockSpec((HEAD_DIM, 1), lambda b, i: (0, 0)),
            pl.BlockSpec((1, 1, d_attn, blk), lambda b, i: (b, i, 0, 0)),
            pl.BlockSpec((1, S, d_attn), lambda b, i: (b, 0, 0)),
            pl.BlockSpec((1, nb, d_attn, blk), lambda b, i: (b, 0, 0, 0)),
            w_slice((D, d_e)), w_slice((D, d_e)), w_slice((d_e, D)),
        ],
        out_specs=[
            pl.BlockSpec((1, blk, d_attn), lambda b, i: (b, i, 0)),
            w_slice((D, 2 * d_e)), w_slice((d_e, D)),
        ],
        out_shape=[
            jax.ShapeDtypeStruct((B, S, d_attn), BF16),
            jax.ShapeDtypeStruct((n_e, D, 2 * d_e), BF16),
            jax.ShapeDtypeStruct((n_e, d_e, D), BF16),
        ],
        scratch_shapes=[
            pltpu.VMEM((N_HEADS, blk, 2 * blk), F32), pltpu.VMEM((N_HEADS, blk, 2 * blk), F32),
            pltpu.VMEM((N_HEADS, blk, 2 * blk), BF16), pltpu.VMEM((N_HEADS, blk, 2 * blk), BF16),
            pltpu.VMEM((N_HEADS, HEAD_DIM + 2 * SUBLANES, 2 * blk), F32),
            pltpu.VMEM((N_HEADS, 1, 2 * blk), F32),
        ],
        compiler_params=pltpu.CompilerParams(
            dimension_semantics=("parallel", "arbitrary"), vmem_limit_bytes=VMEM_LIMIT),
        name="attn",
    )(lq1.reshape(1, SUB_DIM), lk1.reshape(1, SUB_DIM), lq2.reshape(1, SUB_DIM), lk2.reshape(1, SUB_DIM),
      sub_g.reshape(HEAD_DIM, 1), qT, k, vT, w_gate, w_up, w_down)


def _route(scores, bias):
    n_e, tm = scores.shape
    neg = -jnp.inf
    biased = scores + bias
    g3 = biased.reshape(N_GROUPS, GROUP_SIZE, tm)
    io3 = lax.broadcasted_iota(I32, g3.shape, 1)
    m1 = jnp.max(g3, axis=1, keepdims=True)
    first = jnp.min(jnp.where(g3 == m1, io3, GROUP_SIZE), axis=1, keepdims=True)
    m2 = jnp.max(jnp.where(io3 == first, neg, g3), axis=1, keepdims=True)
    grp_score = (m1 + m2).reshape(N_GROUPS, tm)
    gio = lax.broadcasted_iota(I32, (N_GROUPS, tm), 0)
    grp_sel = jnp.zeros((N_GROUPS, tm), jnp.bool_)
    cur = grp_score
    for _ in range(TOPK_GROUPS):
        mx = jnp.max(cur, axis=0, keepdims=True)
        f = jnp.min(jnp.where(cur == mx, gio, N_GROUPS), axis=0, keepdims=True)
        hit = gio == f
        grp_sel = grp_sel | hit
        cur = jnp.where(hit, neg, cur)
    grp_f = jnp.where(grp_sel, 1.0, 0.0).astype(F32)
    exp_mask = jnp.broadcast_to(grp_f.reshape(N_GROUPS, 1, tm), (N_GROUPS, GROUP_SIZE, tm)).reshape(n_e, tm) > 0.5
    eio = lax.broadcasted_iota(I32, (n_e, tm), 0)
    cur = jnp.where(exp_mask, biased, neg)
    sel = jnp.zeros((n_e, tm), jnp.bool_)
    for _ in range(TOP_K):
        mx = jnp.max(cur, axis=0, keepdims=True)
        f = jnp.min(jnp.where(cur == mx, eio, n_e), axis=0, keepdims=True)
        hit = eio == f
        sel = sel | hit
        cur = jnp.where(hit, neg, cur)
    denom = jnp.sum(jnp.where(sel, scores, 0.0), axis=0, keepdims=True)
    gates = jnp.where(sel, scores / denom * ROUTED_SCALE, 0.0)
    return sel, gates


def _mix_kernel(h_ref, halo_ref, cw_ref, cb_ref, cg_ref, cbeta_ref, attn_ref, wo_ref, x_ref, g1_ref, b1_ref,
                wrh_ref, wrl_ref, rb_ref,
                x1_ref, rank_ref, gate_ref, cnt_ref,
                buf, shifted, cbuf, carry, *, tm, d_attn, tiles_per_span):
    tile = pl.program_id(0) * pl.num_programs(1) + pl.program_id(1)

    @pl.when(tile % tiles_per_span == 0)
    def _():
        carry[...] = jnp.zeros_like(carry)

    @pl.when(pl.program_id(1) == 0)
    def _():
        buf[0:CONV_HALO, :] = jnp.zeros((CONV_HALO, buf.shape[1]), F32)

    @pl.when(pl.program_id(1) != 0)
    def _():
        buf[0:CONV_HALO, :] = halo_ref[...]

    buf[CONV_HALO:, :] = h_ref[...]
    d_conv = buf.shape[1]
    for s in range(1, SUBLANES):
        shifted[s - 1] = buf[s:s + shifted.shape[1], :]
    off = CONV_HALO - (CONV_WIDTH - 1)
    for c in range(d_conv // LANES):
        cs = slice(c * LANES, (c + 1) * LANES)
        for r in range(tm // CONV_ROW_CHUNK):
            acc = jnp.zeros((CONV_ROW_CHUNK, LANES), F32)
            for j in range(CONV_WIDTH):
                s = (off + j) % SUBLANES
                start = r * CONV_ROW_CHUNK + off + j - s
                src = buf if s == 0 else shifted.at[s - 1]
                acc = acc + cw_ref[j:j + 1, cs] * src[start:start + CONV_ROW_CHUNK, cs]
            cbuf[r * CONV_ROW_CHUNK:(r + 1) * CONV_ROW_CHUNK, cs] = acc + cb_ref[:, cs]
    conv = _layer_norm(cbuf[...], cg_ref[...], cbeta_ref[...])
    conv = conv * jax.nn.sigmoid(conv)

    mix = _dot(attn_ref[...], wo_ref[0:d_attn, :]) + _dot(conv.astype(BF16), wo_ref[d_attn:, :])
    x1 = _layer_norm(ALPHA * x_ref[...] + mix, g1_ref[...], b1_ref[...])
    x1_ref[...] = x1

    xh = x1.astype(BF16)
    xl = (x1 - xh.astype(F32)).astype(BF16)
    logits = _dot(xh, wrh_ref[...]) + _dot(xl, wrh_ref[...]) + _dot(xh, wrl_ref[...])
    scores = jax.nn.sigmoid(logits).T
    sel, gates = _route(scores, rb_ref[...])
    gate_ref[...] = gates

    t_row = lax.broadcasted_iota(I32, (tm, tm), 0)
    t_col = lax.broadcasted_iota(I32, (tm, tm), 1)
    before = jnp.where(t_row < t_col, 1.0, 0.0).astype(BF16)
    sel_f = jnp.where(sel, 1.0, 0.0).astype(F32)
    rank = _dot(sel_f.astype(BF16), before) + carry[...]
    rank_ref[...] = jnp.where(sel, rank, -1.0).astype(I32)
    carry[...] = carry[...] + jnp.sum(sel_f, axis=1, keepdims=True)
    cnt_ref[...] = carry[...].astype(I32)


def _mix(h, attn, x, conv_w, conv_b, cln_g, cln_b, w_out_bf16, ln1_g, ln1_b, w_router, router_bias):
    B, S, D = x.shape
    d_conv = h.shape[-1]
    d_attn = attn.shape[-1]
    tm = MIX_ROWS
    nt = S // tm
    N = B * S
    n_e = w_router.shape[-1]
    wr_hi = w_router.astype(BF16)
    wr_lo = (w_router - wr_hi.astype(F32)).astype(BF16)
    halo_blocks = tm // CONV_HALO
    row = lambda a: a.reshape(1, -1)
    const = lambda shape: pl.BlockSpec(shape, lambda b, i: (0,) * len(shape))
    tiles_per_span = MOE_SPAN // tm
    kern = functools.partial(_mix_kernel, tm=tm, d_attn=d_attn, tiles_per_span=tiles_per_span)
    return pl.pallas_call(
        kern,
        grid=(B, nt),
        in_specs=[
            pl.BlockSpec((None, tm, d_conv), lambda b, i: (b, i, 0)),
            pl.BlockSpec((None, CONV_HALO, d_conv), lambda b, i: (b, jnp.maximum(i * halo_blocks - 1, 0), 0)),
            const(conv_w.shape), const((1, d_conv)), const((1, d_conv)), const((1, d_conv)),
            pl.BlockSpec((None, tm, d_attn), lambda b, i: (b, i, 0)),
            const(w_out_bf16.shape),
            pl.BlockSpec((None, tm, D), lambda b, i: (b, i, 0)),
            const((1, D)), const((1, D)),
            const(wr_hi.shape), const(wr_lo.shape), const((n_e, 1)),
        ],
        out_specs=[
            pl.BlockSpec((tm, D), lambda b, i: (b * nt + i, 0)),
            pl.BlockSpec((n_e, tm), lambda b, i: (0, b * nt + i)),
            pl.BlockSpec((n_e, tm), lambda b, i: (0, b * nt + i)),
            pl.BlockSpec((None, n_e, 1), lambda b, i: (b * nt + i, 0, 0)),
        ],
        out_shape=[
            jax.ShapeDtypeStruct((N, D), F32),
            jax.ShapeDtypeStruct((n_e, N), I32),
            jax.ShapeDtypeStruct((n_e, N), F32),
            jax.ShapeDtypeStruct((B * nt, n_e, 1), I32),
        ],
        scratch_shapes=[
            pltpu.VMEM((tm + CONV_HALO, d_conv), F32),
            pltpu.VMEM((SUBLANES - 1, tm + CONV_HALO - SUBLANES, d_conv), F32),
            pltpu.VMEM((tm, d_conv), F32),
            pltpu.VMEM((n_e, 1), F32),
        ],
        compiler_params=pltpu.CompilerParams(
            dimension_semantics=("arbitrary", "arbitrary"), vmem_limit_bytes=VMEM_LIMIT),
        name="mix",
    )(h, h, conv_w, row(conv_b), row(cln_g), row(cln_b), attn, w_out_bf16, x, row(ln1_g), row(ln1_b),
      wr_hi, wr_lo, router_bias.reshape(n_e, 1))


def _moe_kernel(ntrip_ref, x1_ref, rank_ref, gate_ref, wgu_ref, wd_ref, wsgu_ref, wsd_ref,
                g2_ref, b2_ref, o_ref, xb_ref, *, tiers, d_e, d_s, per_step, per_trip, span):
    w = pl.program_id(0)
    step = pl.program_id(1)

    @pl.when(step == 0)
    def _():
        x1 = x1_ref[...]
        xb = x1.astype(BF16)
        xb_ref[...] = xb
        gu = _dot(xb, wsgu_ref[...])
        g = gu[:, :d_s]
        shared = _dot((g * jax.nn.sigmoid(g) * gu[:, d_s:]).astype(BF16), wsd_ref[...])
        o_ref[...] = ALPHA * x1 + shared

    n_spans = xb_ref.shape[0] // span

    def trip(c, carry, first, rows):
        onehots, gathered = [], []
        slot_gates = [[None] * n_spans for _ in range(per_trip)]
        for s in range(n_spans):
            lo, hi = s * span, (s + 1) * span
            hits = []
            for k in range(per_trip):
                e = step * per_step + first + k
                rank = rank_ref[pl.ds(e, 1), lo:hi]
                hit = (lax.broadcasted_iota(I32, (rows, 1), 0) + c * rows) == rank
                hits.append(hit)
                slot_gates[k][s] = jnp.sum(jnp.where(hit, gate_ref[pl.ds(e, 1), lo:hi], 0.0), axis=1, keepdims=True)
            onehot = jnp.where(jnp.concatenate(hits, axis=0), 1.0, 0.0).astype(BF16)
            onehots.append(onehot)
            gathered.append(_dot(onehot, xb_ref[lo:hi, :]))
        gated = []
        for k in range(per_trip):
            sl = slice(k * rows, (k + 1) * rows)
            xs = jnp.concatenate([gathered[s][sl] for s in range(n_spans)], axis=0).astype(BF16)
            gu = _dot(xs, wgu_ref[first + k])
            g = gu[:, :d_e]
            y = _dot((g * jax.nn.sigmoid(g) * gu[:, d_e:]).astype(BF16), wd_ref[first + k])
            gated.append(y * jnp.concatenate(slot_gates[k], axis=0))
        for s in range(n_spans):
            sl = slice(s * rows, (s + 1) * rows)
            yg = jnp.concatenate([gated[k][sl] for k in range(per_trip)], axis=0).astype(BF16)
            o_ref[s * span:(s + 1) * span, :] += lax.dot_general(
                onehots[s], yg, (((0,), (0,)), ((), ())), preferred_element_type=F32)
        return carry

    trips_per_step = per_step // per_trip
    for t in range(trips_per_step):
        q = (w * pl.num_programs(1) + step) * trips_per_step + t
        for tier, rows in enumerate(tiers):
            lax.fori_loop(0, ntrip_ref[q * len(tiers) + tier],
                          functools.partial(trip, first=t * per_trip, rows=rows), 0)

    @pl.when(step == pl.num_programs(1) - 1)
    def _():
        o_ref[...] = _layer_norm(o_ref[...], g2_ref[...], b2_ref[...])


def _moe(x1, rank, gates, counts, wgu, wd, ws_gate, ws_up, ws_down, ln2_g, ln2_b):
    N, D = x1.shape
    n_e, d_e, _ = wd.shape
    d_s = ws_down.shape[0]
    W = MOE_WINDOW
    n_win = N // W
    tiers = EXPERT_ROW_TIERS
    span = MOE_SPAN
    per_step, per_trip = MOE_EXPERTS_PER_STEP, MOE_EXPERTS_PER_TRIP
    n_steps = n_e // per_step
    per_span = counts.reshape(N // span, span // MIX_ROWS, n_e)[:, -1]
    fullest = jnp.max(per_span.reshape(n_win, W // span, n_e // per_trip, per_trip), axis=(1, 3)).reshape(-1)
    lower = jnp.asarray((0,) + tiers[:-1], I32)
    fits = (fullest[:, None] > lower[None, :]) & (fullest[:, None] <= jnp.asarray(tiers, I32)[None, :])
    ntrip = fits.astype(I32).at[:, -1].set(
        jnp.where(fullest > tiers[-2], (fullest + tiers[-1] - 1) // tiers[-1], 0).astype(I32)).reshape(-1)
    wsgu = jnp.concatenate([ws_gate, ws_up], axis=-1).astype(BF16)
    kern = functools.partial(_moe_kernel, tiers=tiers, d_e=d_e, d_s=d_s, per_step=per_step,
                             per_trip=per_trip, span=span)
    const = lambda shape: pl.BlockSpec(shape, lambda w, e, nt: (0,) * len(shape))
    return pl.pallas_call(
        kern,
        grid_spec=pltpu.PrefetchScalarGridSpec(
            num_scalar_prefetch=1,
            grid=(n_win, n_steps),
            in_specs=[
                pl.BlockSpec((W, D), lambda w, e, nt: (w, 0)),
                pl.BlockSpec((n_e, W), lambda w, e, nt: (0, w)),
                pl.BlockSpec((n_e, W), lambda w, e, nt: (0, w)),
                pl.BlockSpec((per_step, D, 2 * d_e), lambda w, e, nt: (e, 0, 0)),
                pl.BlockSpec((per_step, d_e, D), lambda w, e, nt: (e, 0, 0)),
                const(wsgu.shape), const((d_s, D)), const((1, D)), const((1, D)),
            ],
            out_specs=pl.BlockSpec((W, D), lambda w, e, nt: (w, 0)),
            scratch_shapes=[pltpu.VMEM((W, D), BF16)],
        ),
        out_shape=jax.ShapeDtypeStruct((N, D), F32),
        compiler_params=pltpu.CompilerParams(
            dimension_semantics=("arbitrary", "arbitrary"), vmem_limit_bytes=VMEM_LIMIT),
        name="moe",
    )(ntrip, x1, rank, gates, wgu, wd, wsgu, ws_down.astype(BF16),
      ln2_g.reshape(1, D), ln2_b.reshape(1, D))


def kernel(x, positions, w_in, lambda_q1, lambda_k1, lambda_q2, lambda_k2, subln_g, conv_w, conv_b, conv_ln_g, conv_ln_b, w_out, ln1_g, ln1_b, w_router, router_bias, w_exp_gate, w_exp_up, w_exp_down, w_sh_gate, w_sh_up, w_sh_down, ln2_g, ln2_b):
    B, S, D = x.shape
    d_attn = N_HEADS * HEAD_DIM
    d_conv = conv_w.shape[-1]
    for l in range(DEPTH):
        lambda_init = 0.8 - 0.6 * math.exp(-0.3 * l)
        qT, k, vT, h = _proj(x, positions, w_in[l].astype(BF16), d_attn, d_conv)
        attn, wgu, wd = _attention(qT, k, vT, lambda_q1[l], lambda_k1[l], lambda_q2[l], lambda_k2[l], subln_g[l],
                                   lambda_init, w_exp_gate[l], w_exp_up[l], w_exp_down[l])
        x1, rank, gates, counts = _mix(h, attn, x, conv_w[l], conv_b[l], conv_ln_g[l], conv_ln_b[l],
                                       w_out[l].astype(BF16), ln1_g[l], ln1_b[l], w_router[l], router_bias[l])
        out = _moe(x1, rank, gates, counts, wgu, wd, w_sh_gate[l], w_sh_up[l], w_sh_down[l], ln2_g[l], ln2_b[l])
        x = out.reshape(B, S, D)
    return x
```

```python
import functools
import math

import jax
import jax.numpy as jnp
from jax import lax
from jax.experimental import pallas as pl
from jax.experimental.pallas import tpu as pltpu

F32 = jnp.float32
BF16 = jnp.bfloat16
I32 = jnp.int32

N_HEADS = 4
HEAD_DIM = 128
SUB_DIM = 64
ROPE_DIM = 16
ROPE_THETA = 500000.0
CHUNK = 64
CONV_WIDTH = 31
N_EXPERTS = 64
TOP_K = 8
N_GROUPS = 8
GROUP_SIZE = N_EXPERTS // N_GROUPS
TOPK_GROUPS = 4
ROUTED_SCALE = 2.5
LN_EPS = 1e-5
DEPTH = 1
ALPHA = (2.0 * DEPTH) ** 0.25
LOG2_E = math.log2(math.e)

LANES = 128
SUBLANES = 8
VMEM_LIMIT = 56 * 1024 * 1024

PROJ_ROWS = 512
ATTN_BLOCK = 256
MIX_ROWS = 256
CONV_HALO = 32
CONV_ROW_CHUNK = 64
MOE_WINDOW = 1024
MOE_SPAN = 256
EXPERT_ROW_TIERS = (48, 56, 64)
MOE_EXPERTS_PER_STEP = 8
MOE_EXPERTS_PER_TRIP = 4


def _dot(a, b):
    return jnp.dot(a, b, preferred_element_type=F32)


def _layer_norm(z, g, b):
    mu = jnp.mean(z, axis=-1, keepdims=True)
    zc = z - mu
    var = jnp.mean(zc * zc, axis=-1, keepdims=True)
    return zc * lax.rsqrt(var + LN_EPS) * g + b


def _proj_kernel(x_ref, pos_ref, invf_ref, w_ref, qT_ref, k_ref, vT_ref, h_ref, *, d_attn, d_conv, blk):
    xb = x_ref[...].astype(BF16)
    tm = xb.shape[0]
    ang = pos_ref[...].astype(F32) * invf_ref[...]
    cos = jnp.cos(ang)
    sin = jnp.sin(ang)
    d = lax.broadcasted_iota(I32, (1, LANES), 1) & (SUB_DIM - 1)
    half = ROPE_DIM // 2
    c_mul = jnp.where(d < ROPE_DIM, cos, 1.0)
    s_lo = jnp.where(d < half, -sin, 0.0)
    s_hi = jnp.where((d >= half) & (d < ROPE_DIM), sin, 0.0)

    def rope(t):
        outs = []
        for c in range(t.shape[1] // LANES):
            ts = t[:, c * LANES:(c + 1) * LANES]
            outs.append(ts * c_mul + pltpu.roll(ts, LANES - half, 1) * s_lo + pltpu.roll(ts, half, 1) * s_hi)
        return jnp.concatenate(outs, axis=1)

    q = rope(_dot(xb, w_ref[:, 0:d_attn])) * (SUB_DIM ** -0.5 * LOG2_E)
    for c in range(tm // blk):
        qT_ref[0, c] = q[c * blk:(c + 1) * blk, :].T.astype(BF16)
    k = rope(_dot(xb, w_ref[:, d_attn:2 * d_attn]))
    k_ref[0] = k.astype(BF16)
    v = _dot(xb, w_ref[:, 2 * d_attn:3 * d_attn])
    for c in range(tm // blk):
        vT_ref[0, c] = v[c * blk:(c + 1) * blk, :].T.astype(BF16)
    a = _dot(xb, w_ref[:, 3 * d_attn:3 * d_attn + d_conv])
    gate = _dot(xb, w_ref[:, 3 * d_attn + d_conv:3 * d_attn + 2 * d_conv])
    h_ref[0] = a * jax.nn.sigmoid(gate)


def _proj(x, positions, w_in_bf16, d_attn, d_conv):
    B, S, D = x.shape
    tm, blk = PROJ_ROWS, ATTN_BLOCK
    nb = S // blk
    inv_freq = ROPE_THETA ** (-jnp.arange(0, ROPE_DIM, 2, dtype=F32) / ROPE_DIM)
    sub = jnp.concatenate([inv_freq, inv_freq, jnp.zeros((SUB_DIM - ROPE_DIM,), F32)])
    invf = jnp.tile(sub, LANES // SUB_DIM).reshape(1, LANES)
    d_in = w_in_bf16.shape[1]
    kern = functools.partial(_proj_kernel, d_attn=d_attn, d_conv=d_conv, blk=blk)
    return pl.pallas_call(
        kern,
        grid=(B, S // tm),
        in_specs=[
            pl.BlockSpec((None, tm, D), lambda b, i: (b, i, 0)),
            pl.BlockSpec((None, tm, 1), lambda b, i: (b, i, 0)),
            pl.BlockSpec((1, LANES), lambda b, i: (0, 0)),
            pl.BlockSpec((D, d_in), lambda b, i: (0, 0)),
        ],
        out_specs=[
            pl.BlockSpec((1, tm // blk, d_attn, blk), lambda b, i: (b, i, 0, 0)),
            pl.BlockSpec((1, tm, d_attn), lambda b, i: (b, i, 0)),
            pl.BlockSpec((1, tm // blk, d_attn, blk), lambda b, i: (b, i, 0, 0)),
            pl.BlockSpec((1, tm, d_conv), lambda b, i: (b, i, 0)),
        ],
        out_shape=[
            jax.ShapeDtypeStruct((B, nb, d_attn, blk), BF16),
            jax.ShapeDtypeStruct((B, S, d_attn), BF16),
            jax.ShapeDtypeStruct((B, nb, d_attn, blk), BF16),
            jax.ShapeDtypeStruct((B, S, d_conv), F32),
        ],
        compiler_params=pltpu.CompilerParams(
            dimension_semantics=("parallel", "parallel"), vmem_limit_bytes=VMEM_LIMIT),
        name="proj",
    )(x, positions.reshape(B, S, 1), invf, w_in_bf16)


def _attn_kernel(lq1_ref, lk1_ref, lq2_ref, lk2_ref, g_ref, qT_ref, k_ref, vT_ref, wg_ref, wu_ref, wd_ref,
                 o_ref, wgu_out, wd_out, s_a, s_b, p_a, p_b, acc_ref, m_ref, *, blk, lambda_init):
    i = pl.program_id(1)
    d_e = wg_ref.shape[-1]
    wgu_out[:, :, :d_e] = wg_ref[...].astype(wgu_out.dtype)
    wgu_out[:, :, d_e:] = wu_ref[...].astype(wgu_out.dtype)
    wd_out[...] = wd_ref[...].astype(wd_out.dtype)
    lam = (jnp.exp(jnp.sum(lq1_ref[...] * lk1_ref[...], axis=-1, keepdims=True))
           - jnp.exp(jnp.sum(lq2_ref[...] * lk2_ref[...], axis=-1, keepdims=True)) + lambda_init)
    row = lax.broadcasted_iota(I32, (HEAD_DIM, 1), 0)
    ones = jnp.ones((2 * SUBLANES, blk), BF16)
    heads = range(N_HEADS)

    def head_cols(h):
        return slice(h * HEAD_DIM, (h + 1) * HEAD_DIM)

    def q_both(h):
        qT = qT_ref[0, 0, head_cols(h), :]
        zero = jnp.zeros_like(qT)
        return jnp.concatenate([jnp.where(row < SUB_DIM, qT, zero), jnp.where(row >= SUB_DIM, qT, zero)], axis=1)

    def scores_into(h, j, s_ref):
        kb = k_ref[0, pl.ds(pl.multiple_of(j * blk, blk), blk), head_cols(h)]
        s_ref[h] = _dot(kb, q_both(h))

    def weighted_values(h, j, p):
        return _dot(jnp.concatenate([vT_ref[0, j, head_cols(h), :], ones], axis=0), p)

    def probs(h, s):
        m = m_ref[h]
        m_new = jnp.maximum(m, jnp.max(s, axis=0, keepdims=True))
        m_ref[h] = m_new
        return jnp.exp2(m - m_new), jnp.exp2(s - m_new).astype(BF16)

    def step(j, s_cur, s_nxt, p_prev, p_cur):
        for h in heads:
            scores_into(h, j + 1, s_nxt)
            pv_prev = weighted_values(h, jnp.maximum(j - 1, 0), p_prev[h])
            alpha, p = probs(h, s_cur[h])
            p_cur[h] = p
            acc_ref[h] = alpha * (acc_ref[h] + pv_prev)

    def finish(s_cur, p_prev):
        kc = lax.broadcasted_iota(I32, (blk, 1), 0) // CHUNK
        qc = (lax.broadcasted_iota(I32, (1, 2 * blk), 1) & (blk - 1)) // CHUNK
        for h in heads:
            pv_prev = weighted_values(h, jnp.maximum(i - 1, 0), p_prev[h])
            alpha, p = probs(h, jnp.where(kc <= qc, s_cur[h], -jnp.inf))
            acc = alpha * (acc_ref[h] + pv_prev) + weighted_values(h, i, p)
            o = acc[:HEAD_DIM] / acc[HEAD_DIM:HEAD_DIM + 1]
            o = o[:, :blk] - lam * o[:, blk:]
            o = o * lax.rsqrt(jnp.mean(o * o, axis=0, keepdims=True) + LN_EPS)
            o = o * g_ref[...] * (1.0 - lambda_init)
            o_ref[0, :, head_cols(h)] = o.T.astype(o_ref.dtype)

    m_ref[...] = jnp.full(m_ref.shape, -jnp.inf, F32)
    acc_ref[...] = jnp.zeros(acc_ref.shape, F32)
    p_b[...] = jnp.zeros(p_b.shape, BF16)
    for h in heads:
        scores_into(h, 0, s_a)

    def pair(jj, c):
        step(2 * jj, s_a, s_b, p_b, p_a)
        step(2 * jj + 1, s_b, s_a, p_a, p_b)
        return c

    lax.fori_loop(0, i // 2, pair, 0)

    @pl.when(i % 2 == 1)
    def _():
        step(i - 1, s_a, s_b, p_b, p_a)
        finish(s_b, p_a)

    @pl.when(i % 2 == 0)
    def _():
        finish(s_a, p_b)


def _attention(qT, k, vT, lq1, lk1, lq2, lk2, sub_g, lambda_init, w_gate, w_up, w_down):
    B, nb, d_attn, blk = qT.shape
    S = nb * blk
    n_e, D, d_e = w_gate.shape
    per_step = n_e // (B * nb)
    assert per_step * B * nb == n_e
    kern = functools.partial(_attn_kernel, blk=blk, lambda_init=lambda_init)
    vec = pl.BlockSpec((1, SUB_DIM), lambda b, i: (0, 0))
    w_slice = lambda shape: pl.BlockSpec((per_step,) + shape, lambda b, i: (b * nb + i, 0, 0))
    return pl.pallas_call(
        kern,
        grid=(B, nb),
        in_specs=[
            vec, vec, vec, vec,
            pl.BlockSpec((HEAD_DIM, 1), lambda b, i: (0, 0)),
            pl.BlockSpec((1, 1, d_attn, blk), lambda b, i: (b, i, 0, 0)),
            pl.BlockSpec((1, S, d_attn), lambda b, i: (b, 0, 0)),
            pl.BlockSpec((1, nb, d_attn, blk), lambda b, i: (b, 0, 0, 0)),
            w_slice((D, d_e)), w_slice((D, d_e)), w_slice((d_e, D)),
        ],
        out_specs=[
            pl.BlockSpec((1, blk, d_attn), lambda b, i: (b, i, 0)),
            w_slice((D, 2 * d_e)), w_slice((d_e, D)),
        ],
        out_shape=[
            jax.ShapeDtypeStruct((B, S, d_attn), BF16),
            jax.ShapeDtypeStruct((n_e, D, 2 * d_e), BF16),
            jax.ShapeDtypeStruct((n_e, d_e, D), BF16),
        ],
        scratch_shapes=[
            pltpu.VMEM((N_HEADS, blk, 2 * blk), F32), pltpu.VMEM((N_HEADS, blk, 2 * blk), F32),
            pltpu.VMEM((N_HEADS, blk, 2 * blk), BF16), pltpu.VMEM((N_HEADS, blk, 2 * blk), BF16),
            pltpu.VMEM((N_HEADS, HEAD_DIM + 2 * SUBLANES, 2 * blk), F32),
            pltpu.VMEM((N_HEADS, 1, 2 * blk), F32),
        ],
        compiler_params=pltpu.CompilerParams(
            dimension_semantics=("parallel", "arbitrary"), vmem_limit_bytes=VMEM_LIMIT),
        name="attn",
    )(lq1.reshape(1, SUB_DIM), lk1.reshape(1, SUB_DIM), lq2.reshape(1, SUB_DIM), lk2.reshape(1, SUB_DIM),
      sub_g.reshape(HEAD_DIM, 1), qT, k, vT, w_gate, w_up, w_down)


def _route(scores, bias):
    n_e, tm = scores.shape
    neg = -jnp.inf
    biased = scores + bias
    g3 = biased.reshape(N_GROUPS, GROUP_SIZE, tm)
    io3 = lax.broadcasted_iota(I32, g3.shape, 1)
    m1 = jnp.max(g3, axis=1, keepdims=True)
    first = jnp.min(jnp.where(g3 == m1, io3, GROUP_SIZE), axis=1, keepdims=True)
    m2 = jnp.max(jnp.where(io3 == first, neg, g3), axis=1, keepdims=True)
    grp_score = (m1 + m2).reshape(N_GROUPS, tm)
    gio = lax.broadcasted_iota(I32, (N_GROUPS, tm), 0)
    grp_sel = jnp.zeros((N_GROUPS, tm), jnp.bool_)
    cur = grp_score
    for _ in range(TOPK_GROUPS):
        mx = jnp.max(cur, axis=0, keepdims=True)
        f = jnp.min(jnp.where(cur == mx, gio, N_GROUPS), axis=0, keepdims=True)
        hit = gio == f
        grp_sel = grp_sel | hit
        cur = jnp.where(hit, neg, cur)
    grp_f = jnp.where(grp_sel, 1.0, 0.0).astype(F32)
    exp_mask = jnp.broadcast_to(grp_f.reshape(N_GROUPS, 1, tm), (N_GROUPS, GROUP_SIZE, tm)).reshape(n_e, tm) > 0.5
    eio = lax.broadcasted_iota(I32, (n_e, tm), 0)
    cur = jnp.where(exp_mask, biased, neg)
    sel = jnp.zeros((n_e, tm), jnp.bool_)
    for _ in range(TOP_K):
        mx = jnp.max(cur, axis=0, keepdims=True)
        f = jnp.min(jnp.where(cur == mx, eio, n_e), axis=0, keepdims=True)
        hit = eio == f
        sel = sel | hit
        cur = jnp.where(hit, neg, cur)
    denom = jnp.sum(jnp.where(sel, scores, 0.0), axis=0, keepdims=True)
    gates = jnp.where(sel, scores / denom * ROUTED_SCALE, 0.0)
    return sel, gates


def _mix_kernel(h_ref, halo_ref, cw_ref, cb_ref, cg_ref, cbeta_ref, attn_ref, wo_ref, x_ref, g1_ref, b1_ref,
                wrh_ref, wrl_ref, rb_ref,
                x1_ref, rank_ref, gate_ref, cnt_ref,
                buf, shifted, cbuf, carry, *, tm, d_attn, tiles_per_span):
    tile = pl.program_id(0) * pl.num_programs(1) + pl.program_id(1)

    @pl.when(tile % tiles_per_span == 0)
    def _():
        carry[...] = jnp.zeros_like(carry)

    @pl.when(pl.program_id(1) == 0)
    def _():
        buf[0:CONV_HALO, :] = jnp.zeros((CONV_HALO, buf.shape[1]), F32)

    @pl.when(pl.program_id(1) != 0)
    def _():
        buf[0:CONV_HALO, :] = halo_ref[...]

    buf[CONV_HALO:, :] = h_ref[...]
    d_conv = buf.shape[1]
    for s in range(1, SUBLANES):
        shifted[s - 1] = buf[s:s + shifted.shape[1], :]
    off = CONV_HALO - (CONV_WIDTH - 1)
    for c in range(d_conv // LANES):
        cs = slice(c * LANES, (c + 1) * LANES)
        for r in range(tm // CONV_ROW_CHUNK):
            acc = jnp.zeros((CONV_ROW_CHUNK, LANES), F32)
            for j in range(CONV_WIDTH):
                s = (off + j) % SUBLANES
                start = r * CONV_ROW_CHUNK + off + j - s
                src = buf if s == 0 else shifted.at[s - 1]
                acc = acc + cw_ref[j:j + 1, cs] * src[start:start + CONV_ROW_CHUNK, cs]
            cbuf[r * CONV_ROW_CHUNK:(r + 1) * CONV_ROW_CHUNK, cs] = acc + cb_ref[:, cs]
    conv = _layer_norm(cbuf[...], cg_ref[...], cbeta_ref[...])
    conv = conv * jax.nn.sigmoid(conv)

    mix = _dot(attn_ref[...], wo_ref[0:d_attn, :]) + _dot(conv.astype(BF16), wo_ref[d_attn:, :])
    x1 = _layer_norm(ALPHA * x_ref[...] + mix, g1_ref[...], b1_ref[...])
    x1_ref[...] = x1

    xh = x1.astype(BF16)
    xl = (x1 - xh.astype(F32)).astype(BF16)
    logits = _dot(xh, wrh_ref[...]) + _dot(xl, wrh_ref[...]) + _dot(xh, wrl_ref[...])
    scores = jax.nn.sigmoid(logits).T
    sel, gates = _route(scores, rb_ref[...])
    gate_ref[...] = gates

    t_row = lax.broadcasted_iota(I32, (tm, tm), 0)
    t_col = lax.broadcasted_iota(I32, (tm, tm), 1)
    before = jnp.where(t_row < t_col, 1.0, 0.0).astype(BF16)
    sel_f = jnp.where(sel, 1.0, 0.0).astype(F32)
    rank = _dot(sel_f.astype(BF16), before) + carry[...]
    rank_ref[...] = jnp.where(sel, rank, -1.0).astype(I32)
    carry[...] = carry[...] + jnp.sum(sel_f, axis=1, keepdims=True)
    cnt_ref[...] = carry[...].astype(I32)


def _mix(h, attn, x, conv_w, conv_b, cln_g, cln_b, w_out_bf16, ln1_g, ln1_b, w_router, router_bias):
    B, S, D = x.shape
    d_conv = h.shape[-1]
    d_attn = attn.shape[-1]
    tm = MIX_ROWS
    nt = S // tm
    N = B * S
    n_e = w_router.shape[-1]
    wr_hi = w_router.astype(BF16)
    wr_lo = (w_router - wr_hi.astype(F32)).astype(BF16)
    halo_blocks = tm // CONV_HALO
    row = lambda a: a.reshape(1, -1)
    const = lambda shape: pl.BlockSpec(shape, lambda b, i: (0,) * len(shape))
    tiles_per_span = MOE_SPAN // tm
    kern = functools.partial(_mix_kernel, tm=tm, d_attn=d_attn, tiles_per_span=tiles_per_span)
    return pl.pallas_call(
        kern,
        grid=(B, nt),
        in_specs=[
            pl.BlockSpec((None, tm, d_conv), lambda b, i: (b, i, 0)),
            pl.BlockSpec((None, CONV_HALO, d_conv), lambda b, i: (b, jnp.maximum(i * halo_blocks - 1, 0), 0)),
            const(conv_w.shape), const((1, d_conv)), const((1, d_conv)), const((1, d_conv)),
            pl.BlockSpec((None, tm, d_attn), lambda b, i: (b, i, 0)),
            const(w_out_bf16.shape),
            pl.BlockSpec((None, tm, D), lambda b, i: (b, i, 0)),
            const((1, D)), const((1, D)),
            const(wr_hi.shape), const(wr_lo.shape), const((n_e, 1)),
        ],
        out_specs=[
            pl.BlockSpec((tm, D), lambda b, i: (b * nt + i, 0)),
            pl.BlockSpec((n_e, tm), lambda b, i: (0, b * nt + i)),
            pl.BlockSpec((n_e, tm), lambda b, i: (0, b * nt + i)),
            pl.BlockSpec((None, n_e, 1), lambda b, i: (b * nt + i, 0, 0)),
        ],
        out_shape=[
            jax.ShapeDtypeStruct((N, D), F32),
            jax.ShapeDtypeStruct((n_e, N), I32),
            jax.ShapeDtypeStruct((n_e, N), F32),
            jax.ShapeDtypeStruct((B * nt, n_e, 1), I32),
        ],
        scratch_shapes=[
            pltpu.VMEM((tm + CONV_HALO, d_conv), F32),
            pltpu.VMEM((SUBLANES - 1, tm + CONV_HALO - SUBLANES, d_conv), F32),
            pltpu.VMEM((tm, d_conv), F32),
            pltpu.VMEM((n_e, 1), F32),
        ],
        compiler_params=pltpu.CompilerParams(
            dimension_semantics=("arbitrary", "arbitrary"), vmem_limit_bytes=VMEM_LIMIT),
        name="mix",
    )(h, h, conv_w, row(conv_b), row(cln_g), row(cln_b), attn, w_out_bf16, x, row(ln1_g), row(ln1_b),
      wr_hi, wr_lo, router_bias.reshape(n_e, 1))


def _moe_kernel(ntrip_ref, x1_ref, rank_ref, gate_ref, wgu_ref, wd_ref, wsgu_ref, wsd_ref,
                g2_ref, b2_ref, o_ref, xb_ref, *, tiers, d_e, d_s, per_step, per_trip, span):
    w = pl.program_id(0)
    step = pl.program_id(1)

    @pl.when(step == 0)
    def _():
        x1 = x1_ref[...]
        xb = x1.astype(BF16)
        xb_ref[...] = xb
        gu = _dot(xb, wsgu_ref[...])
        g = gu[:, :d_s]
        shared = _dot((g * jax.nn.sigmoid(g) * gu[:, d_s:]).astype(BF16), wsd_ref[...])
        o_ref[...] = ALPHA * x1 + shared

    n_spans = xb_ref.shape[0] // span

    def trip(c, carry, first, rows):
        onehots, gathered = [], []
        slot_gates = [[None] * n_spans for _ in range(per_trip)]
        for s in range(n_spans):
            lo, hi = s * span, (s + 1) * span
            hits = []
            for k in range(per_trip):
                e = step * per_step + first + k
                rank = rank_ref[pl.ds(e, 1), lo:hi]
                hit = (lax.broadcasted_iota(I32, (rows, 1), 0) + c * rows) == rank
                hits.append(hit)
                slot_gates[k][s] = jnp.sum(jnp.where(hit, gate_ref[pl.ds(e, 1), lo:hi], 0.0), axis=1, keepdims=True)
            onehot = jnp.where(jnp.concatenate(hits, axis=0), 1.0, 0.0).astype(BF16)
            onehots.append(onehot)
            gathered.append(_dot(onehot, xb_ref[lo:hi, :]))
        gated = []
        for k in range(per_trip):
            sl = slice(k * rows, (k + 1) * rows)
            xs = jnp.concatenate([gathered[s][sl] for s in range(n_spans)], axis=0).astype(BF16)
            gu = _dot(xs, wgu_ref[first + k])
            g = gu[:, :d_e]
            y = _dot((g * jax.nn.sigmoid(g) * gu[:, d_e:]).astype(BF16), wd_ref[first + k])
            gated.append(y * jnp.concatenate(slot_gates[k], axis=0))
        for s in range(n_spans):
            sl = slice(s * rows, (s + 1) * rows)
            yg = jnp.concatenate([gated[k][sl] for k in range(per_trip)], axis=0).astype(BF16)
            o_ref[s * span:(s + 1) * span, :] += lax.dot_general(
                onehots[s], yg, (((0,), (0,)), ((), ())), preferred_element_type=F32)
        return carry

    quads_per_step = per_step // per_trip

    def quad(t, carry):
        q = (w * pl.num_programs(1) + step) * quads_per_step + t
        for tier, rows in enumerate(tiers):
            lax.fori_loop(0, ntrip_ref[q * len(tiers) + tier],
                          functools.partial(trip, first=t * per_trip, rows=rows), 0)
        return carry

    lax.fori_loop(0, quads_per_step, quad, 0)

    @pl.when(step == pl.num_programs(1) - 1)
    def _():
        o_ref[...] = _layer_norm(o_ref[...], g2_ref[...], b2_ref[...])


def _moe(x1, rank, gates, counts, wgu, wd, ws_gate, ws_up, ws_down, ln2_g, ln2_b):
    N, D = x1.shape
    n_e, d_e, _ = wd.shape
    d_s = ws_down.shape[0]
    W = MOE_WINDOW
    n_win = N // W
    tiers = EXPERT_ROW_TIERS
    span = MOE_SPAN
    per_step, per_trip = MOE_EXPERTS_PER_STEP, MOE_EXPERTS_PER_TRIP
    n_steps = n_e // per_step
    per_span = counts.reshape(N // span, span // MIX_ROWS, n_e)[:, -1]
    fullest = jnp.max(per_span.reshape(n_win, W // span, n_e // per_trip, per_trip), axis=(1, 3)).reshape(-1)
    lower = jnp.asarray((0,) + tiers[:-1], I32)
    fits = (fullest[:, None] > lower[None, :]) & (fullest[:, None] <= jnp.asarray(tiers, I32)[None, :])
    ntrip = fits.astype(I32).at[:, -1].set(
        jnp.where(fullest > tiers[-2], (fullest + tiers[-1] - 1) // tiers[-1], 0).astype(I32)).reshape(-1)
    wsgu = jnp.concatenate([ws_gate, ws_up], axis=-1).astype(BF16)
    kern = functools.partial(_moe_kernel, tiers=tiers, d_e=d_e, d_s=d_s, per_step=per_step,
                             per_trip=per_trip, span=span)
    const = lambda shape: pl.BlockSpec(shape, lambda w, e, nt: (0,) * len(shape))
    return pl.pallas_call(
        kern,
        grid_spec=pltpu.PrefetchScalarGridSpec(
            num_scalar_prefetch=1,
            grid=(n_win, n_steps),
            in_specs=[
                pl.BlockSpec((W, D), lambda w, e, nt: (w, 0)),
                pl.BlockSpec((n_e, W), lambda w, e, nt: (0, w)),
                pl.BlockSpec((n_e, W), lambda w, e, nt: (0, w)),
                pl.BlockSpec((per_step, D, 2 * d_e), lambda w, e, nt: (e, 0, 0)),
                pl.BlockSpec((per_step, d_e, D), lambda w, e, nt: (e, 0, 0)),
                const(wsgu.shape), const((d_s, D)), const((1, D)), const((1, D)),
            ],
            out_specs=pl.BlockSpec((W, D), lambda w, e, nt: (w, 0)),
            scratch_shapes=[pltpu.VMEM((W, D), BF16)],
        ),
        out_shape=jax.ShapeDtypeStruct((N, D), F32),
        compiler_params=pltpu.CompilerParams(
            dimension_semantics=("arbitrary", "arbitrary"), vmem_limit_bytes=VMEM_LIMIT),
        name="moe",
    )(ntrip, x1, rank, gates, wgu, wd, wsgu, ws_down.astype(BF16),
      ln2_g.reshape(1, D), ln2_b.reshape(1, D))


def kernel(x, positions, w_in, lambda_q1, lambda_k1, lambda_q2, lambda_k2, subln_g, conv_w, conv_b, conv_ln_g, conv_ln_b, w_out, ln1_g, ln1_b, w_router, router_bias, w_exp_gate, w_exp_up, w_exp_down, w_sh_gate, w_sh_up, w_sh_down, ln2_g, ln2_b):
    B, S, D = x.shape
    d_attn = N_HEADS * HEAD_DIM
    d_conv = conv_w.shape[-1]
    for l in range(DEPTH):
        lambda_init = 0.8 - 0.6 * math.exp(-0.3 * l)
        qT, k, vT, h = _proj(x, positions, w_in[l].astype(BF16), d_attn, d_conv)
        attn, wgu, wd = _attention(qT, k, vT, lambda_q1[l], lambda_k1[l], lambda_q2[l], lambda_k2[l], subln_g[l],
                                   lambda_init, w_exp_gate[l], w_exp_up[l], w_exp_down[l])
        x1, rank, gates, counts = _mix(h, attn, x, conv_w[l], conv_b[l], conv_ln_g[l], conv_ln_b[l],
                                       w_out[l].astype(BF16), ln1_g[l], ln1_b[l], w_router[l], router_bias[l])
        out = _moe(x1, rank, gates, counts, wgu, wd, w_sh_gate[l], w_sh_up[l], w_sh_down[l], ln2_g[l], ln2_b[l])
        x = out.reshape(B, S, D)
    return x
```

```python
import functools
import math

import jax
import jax.numpy as jnp
from jax import lax
from jax.experimental import pallas as pl
from jax.experimental.pallas import tpu as pltpu

F32 = jnp.float32
BF16 = jnp.bfloat16
I32 = jnp.int32

N_HEADS = 4
HEAD_DIM = 128
SUB_DIM = 64
ROPE_DIM = 16
ROPE_THETA = 500000.0
CHUNK = 64
CONV_WIDTH = 31
N_EXPERTS = 64
TOP_K = 8
N_GROUPS = 8
GROUP_SIZE = N_EXPERTS // N_GROUPS
TOPK_GROUPS = 4
ROUTED_SCALE = 2.5
LN_EPS = 1e-5
DEPTH = 1
ALPHA = (2.0 * DEPTH) ** 0.25
LOG2_E = math.log2(math.e)

LANES = 128
SUBLANES = 8
VMEM_LIMIT = 56 * 1024 * 1024

PROJ_ROWS = 512
ATTN_BLOCK = 256
MIX_ROWS = 256
CONV_HALO = 32
CONV_ROW_CHUNK = 64
MOE_WINDOW = 1024
MOE_SPAN = 256
EXPERT_ROW_TIERS = (48, 56, 64)
MOE_EXPERTS_PER_STEP = 8
MOE_EXPERTS_PER_TRIP = 4


def _dot(a, b):
    return jnp.dot(a, b, preferred_element_type=F32)


def _layer_norm(z, g, b):
    mu = jnp.mean(z, axis=-1, keepdims=True)
    zc = z - mu
    var = jnp.mean(zc * zc, axis=-1, keepdims=True)
    return zc * lax.rsqrt(var + LN_EPS) * g + b


def _proj_kernel(x_ref, pos_ref, invf_ref, w_ref, qT_ref, k_ref, vT_ref, h_ref, *, d_attn, d_conv, blk):
    xb = x_ref[...].astype(BF16)
    tm = xb.shape[0]
    ang = pos_ref[...].astype(F32) * invf_ref[...]
    cos = jnp.cos(ang)
    sin = jnp.sin(ang)
    d = lax.broadcasted_iota(I32, (1, LANES), 1) & (SUB_DIM - 1)
    half = ROPE_DIM // 2
    c_mul = jnp.where(d < ROPE_DIM, cos, 1.0)
    s_lo = jnp.where(d < half, -sin, 0.0)
    s_hi = jnp.where((d >= half) & (d < ROPE_DIM), sin, 0.0)

    def rope(t):
        outs = []
        for c in range(t.shape[1] // LANES):
            ts = t[:, c * LANES:(c + 1) * LANES]
            outs.append(ts * c_mul + pltpu.roll(ts, LANES - half, 1) * s_lo + pltpu.roll(ts, half, 1) * s_hi)
        return jnp.concatenate(outs, axis=1)

    q = rope(_dot(xb, w_ref[:, 0:d_attn])) * (SUB_DIM ** -0.5 * LOG2_E)
    for c in range(tm // blk):
        qT_ref[0, c] = q[c * blk:(c + 1) * blk, :].T.astype(BF16)
    k = rope(_dot(xb, w_ref[:, d_attn:2 * d_attn]))
    k_ref[0] = k.astype(BF16)
    v = _dot(xb, w_ref[:, 2 * d_attn:3 * d_attn])
    for c in range(tm // blk):
        vT_ref[0, c] = v[c * blk:(c + 1) * blk, :].T.astype(BF16)
    a = _dot(xb, w_ref[:, 3 * d_attn:3 * d_attn + d_conv])
    gate = _dot(xb, w_ref[:, 3 * d_attn + d_conv:3 * d_attn + 2 * d_conv])
    h_ref[0] = a * jax.nn.sigmoid(gate)


def _proj(x, positions, w_in_bf16, d_attn, d_conv):
    B, S, D = x.shape
    tm, blk = PROJ_ROWS, ATTN_BLOCK
    nb = S // blk
    inv_freq = ROPE_THETA ** (-jnp.arange(0, ROPE_DIM, 2, dtype=F32) / ROPE_DIM)
    sub = jnp.concatenate([inv_freq, inv_freq, jnp.zeros((SUB_DIM - ROPE_DIM,), F32)])
    invf = jnp.tile(sub, LANES // SUB_DIM).reshape(1, LANES)
    d_in = w_in_bf16.shape[1]
    kern = functools.partial(_proj_kernel, d_attn=d_attn, d_conv=d_conv, blk=blk)
    return pl.pallas_call(
        kern,
        grid=(B, S // tm),
        in_specs=[
            pl.BlockSpec((None, tm, D), lambda b, i: (b, i, 0)),
            pl.BlockSpec((None, tm, 1), lambda b, i: (b, i, 0)),
            pl.BlockSpec((1, LANES), lambda b, i: (0, 0)),
            pl.BlockSpec((D, d_in), lambda b, i: (0, 0)),
        ],
        out_specs=[
            pl.BlockSpec((1, tm // blk, d_attn, blk), lambda b, i: (b, i, 0, 0)),
            pl.BlockSpec((1, tm, d_attn), lambda b, i: (b, i, 0)),
            pl.BlockSpec((1, tm // blk, d_attn, blk), lambda b, i: (b, i, 0, 0)),
            pl.BlockSpec((1, tm, d_conv), lambda b, i: (b, i, 0)),
        ],
        out_shape=[
            jax.ShapeDtypeStruct((B, nb, d_attn, blk), BF16),
            jax.ShapeDtypeStruct((B, S, d_attn), BF16),
            jax.ShapeDtypeStruct((B, nb, d_attn, blk), BF16),
            jax.ShapeDtypeStruct((B, S, d_conv), F32),
        ],
        compiler_params=pltpu.CompilerParams(
            dimension_semantics=("parallel", "parallel"), vmem_limit_bytes=VMEM_LIMIT),
        name="proj",
    )(x, positions.reshape(B, S, 1), invf, w_in_bf16)


def _attn_kernel(lq1_ref, lk1_ref, lq2_ref, lk2_ref, g_ref, qT_ref, k_ref, vT_ref, wg_ref, wu_ref, wd_ref,
                 o_ref, wgu_out, wd_out, q_both, s_a, s_b, p_a, p_b, acc_ref, m_ref, *, blk, lambda_init):
    i = pl.program_id(1)
    d_e = wg_ref.shape[-1]
    wgu_out[:, :, :d_e] = wg_ref[...].astype(wgu_out.dtype)
    wgu_out[:, :, d_e:] = wu_ref[...].astype(wgu_out.dtype)
    wd_out[...] = wd_ref[...].astype(wd_out.dtype)
    lam = (jnp.exp(jnp.sum(lq1_ref[...] * lk1_ref[...], axis=-1, keepdims=True))
           - jnp.exp(jnp.sum(lq2_ref[...] * lk2_ref[...], axis=-1, keepdims=True)) + lambda_init)
    row = lax.broadcasted_iota(I32, (HEAD_DIM, 1), 0)
    ones = jnp.ones((2 * SUBLANES, blk), BF16)
    heads = range(N_HEADS)

    def head_cols(h):
        return slice(h * HEAD_DIM, (h + 1) * HEAD_DIM)

    for h in heads:
        qT = qT_ref[0, 0, head_cols(h), :]
        zero = jnp.zeros_like(qT)
        q_both[h] = jnp.concatenate([jnp.where(row < SUB_DIM, qT, zero), jnp.where(row >= SUB_DIM, qT, zero)], axis=1)

    def scores_into(h, j, s_ref):
        kb = k_ref[0, pl.ds(pl.multiple_of(j * blk, blk), blk), head_cols(h)]
        s_ref[h] = _dot(kb, q_both[h])

    def weighted_values(h, j, p):
        return _dot(jnp.concatenate([vT_ref[0, j, head_cols(h), :], ones], axis=0), p)

    def probs(h, s):
        m = m_ref[h]
        m_new = jnp.maximum(m, jnp.max(s, axis=0, keepdims=True))
        m_ref[h] = m_new
        return jnp.exp2(m - m_new), jnp.exp2(s - m_new).astype(BF16)

    def step(j, s_cur, s_nxt, p_prev, p_cur):
        for h in heads:
            scores_into(h, j + 1, s_nxt)
            pv_prev = weighted_values(h, jnp.maximum(j - 1, 0), p_prev[h])
            alpha, p = probs(h, s_cur[h])
            p_cur[h] = p
            acc_ref[h] = alpha * (acc_ref[h] + pv_prev)

    def finish(s_cur, p_prev):
        kc = lax.broadcasted_iota(I32, (blk, 1), 0) // CHUNK
        qc = (lax.broadcasted_iota(I32, (1, 2 * blk), 1) & (blk - 1)) // CHUNK
        for h in heads:
            pv_prev = weighted_values(h, jnp.maximum(i - 1, 0), p_prev[h])
            alpha, p = probs(h, jnp.where(kc <= qc, s_cur[h], -jnp.inf))
            acc = alpha * (acc_ref[h] + pv_prev) + weighted_values(h, i, p)
            o = acc[:HEAD_DIM] * (1.0 / acc[HEAD_DIM:HEAD_DIM + 1])
            o = o[:, :blk] - lam * o[:, blk:]
            o = o * lax.rsqrt(jnp.mean(o * o, axis=0, keepdims=True) + LN_EPS)
            o = o * g_ref[...] * (1.0 - lambda_init)
            o_ref[0, :, head_cols(h)] = o.T.astype(o_ref.dtype)

    m_ref[...] = jnp.full(m_ref.shape, -jnp.inf, F32)
    acc_ref[...] = jnp.zeros(acc_ref.shape, F32)
    p_b[...] = jnp.zeros(p_b.shape, BF16)
    for h in heads:
        scores_into(h, 0, s_a)

    def pair(jj, c):
        step(2 * jj, s_a, s_b, p_b, p_a)
        step(2 * jj + 1, s_b, s_a, p_a, p_b)
        return c

    lax.fori_loop(0, i // 2, pair, 0)

    @pl.when(i % 2 == 1)
    def _():
        step(i - 1, s_a, s_b, p_b, p_a)
        finish(s_b, p_a)

    @pl.when(i % 2 == 0)
    def _():
        finish(s_a, p_b)


def _attention(qT, k, vT, lq1, lk1, lq2, lk2, sub_g, lambda_init, w_gate, w_up, w_down):
    B, nb, d_attn, blk = qT.shape
    S = nb * blk
    n_e, D, d_e = w_gate.shape
    per_step = n_e // (B * nb)
    assert per_step * B * nb == n_e
    kern = functools.partial(_attn_kernel, blk=blk, lambda_init=lambda_init)
    vec = pl.BlockSpec((1, SUB_DIM), lambda b, i: (0, 0))
    w_slice = lambda shape: pl.BlockSpec((per_step,) + shape, lambda b, i: (b * nb + i, 0, 0))
    return pl.pallas_call(
        kern,
        grid=(B, nb),
        in_specs=[
            vec, vec, vec, vec,
            pl.BlockSpec((HEAD_DIM, 1), lambda b, i: (0, 0)),
            pl.BlockSpec((1, 1, d_attn, blk), lambda b, i: (b, i, 0, 0)),
            pl.BlockSpec((1, S, d_attn), lambda b, i: (b, 0, 0)),
            pl.BlockSpec((1, nb, d_attn, blk), lambda b, i: (b, 0, 0, 0)),
            w_slice((D, d_e)), w_slice((D, d_e)), w_slice((d_e, D)),
        ],
        out_specs=[
            pl.BlockSpec((1, blk, d_attn), lambda b, i: (b, i, 0)),
            w_slice((D, 2 * d_e)), w_slice((d_e, D)),
        ],
        out_shape=[
            jax.ShapeDtypeStruct((B, S, d_attn), BF16),
            jax.ShapeDtypeStruct((n_e, D, 2 * d_e), BF16),
            jax.ShapeDtypeStruct((n_e, d_e, D), BF16),
        ],
        scratch_shapes=[
            pltpu.VMEM((N_HEADS, HEAD_DIM, 2 * blk), BF16),
            pltpu.VMEM((N_HEADS, blk, 2 * blk), F32), pltpu.VMEM((N_HEADS, blk, 2 * blk), F32),
            pltpu.VMEM((N_HEADS, blk, 2 * blk), BF16), pltpu.VMEM((N_HEADS, blk, 2 * blk), BF16),
            pltpu.VMEM((N_HEADS, HEAD_DIM + 2 * SUBLANES, 2 * blk), F32),
            pltpu.VMEM((N_HEADS, 1, 2 * blk), F32),
        ],
        compiler_params=pltpu.CompilerParams(
            dimension_semantics=("parallel", "arbitrary"), vmem_limit_bytes=VMEM_LIMIT),
        name="attn",
    )(lq1.reshape(1, SUB_DIM), lk1.reshape(1, SUB_DIM), lq2.reshape(1, SUB_DIM), lk2.reshape(1, SUB_DIM),
      sub_g.reshape(HEAD_DIM, 1), qT, k, vT, w_gate, w_up, w_down)


def _route(scores, bias):
    n_e, tm = scores.shape
    neg = -jnp.inf
    biased = scores + bias
    g3 = biased.reshape(N_GROUPS, GROUP_SIZE, tm)
    io3 = lax.broadcasted_iota(I32, g3.shape, 1)
    m1 = jnp.max(g3, axis=1, keepdims=True)
    first = jnp.min(jnp.where(g3 == m1, io3, GROUP_SIZE), axis=1, keepdims=True)
    m2 = jnp.max(jnp.where(io3 == first, neg, g3), axis=1, keepdims=True)
    grp_score = (m1 + m2).reshape(N_GROUPS, tm)
    gio = lax.broadcasted_iota(I32, (N_GROUPS, tm), 0)
    grp_sel = jnp.zeros((N_GROUPS, tm), jnp.bool_)
    cur = grp_score
    for _ in range(TOPK_GROUPS):
        mx = jnp.max(cur, axis=0, keepdims=True)
        f = jnp.min(jnp.where(cur == mx, gio, N_GROUPS), axis=0, keepdims=True)
        hit = gio == f
        grp_sel = grp_sel | hit
        cur = jnp.where(hit, neg, cur)
    grp_f = jnp.where(grp_sel, 1.0, 0.0).astype(F32)
    exp_mask = jnp.broadcast_to(grp_f.reshape(N_GROUPS, 1, tm), (N_GROUPS, GROUP_SIZE, tm)).reshape(n_e, tm) > 0.5
    eio = lax.broadcasted_iota(I32, (n_e, tm), 0)
    cur = jnp.where(exp_mask, biased, neg)
    sel = jnp.zeros((n_e, tm), jnp.bool_)
    for _ in range(TOP_K):
        mx = jnp.max(cur, axis=0, keepdims=True)
        f = jnp.min(jnp.where(cur == mx, eio, n_e), axis=0, keepdims=True)
        hit = eio == f
        sel = sel | hit
        cur = jnp.where(hit, neg, cur)
    denom = jnp.sum(jnp.where(sel, scores, 0.0), axis=0, keepdims=True)
    gates = jnp.where(sel, scores / denom * ROUTED_SCALE, 0.0)
    return sel, gates


def _mix_kernel(h_ref, halo_ref, cw_ref, cb_ref, cg_ref, cbeta_ref, attn_ref, wo_ref, x_ref, g1_ref, b1_ref,
                wrh_ref, wrl_ref, rb_ref,
                x1_ref, rank_ref, gate_ref, cnt_ref,
                buf, shifted, cbuf, carry, *, tm, d_attn, tiles_per_span):
    tile = pl.program_id(0) * pl.num_programs(1) + pl.program_id(1)

    @pl.when(tile % tiles_per_span == 0)
    def _():
        carry[...] = jnp.zeros_like(carry)

    @pl.when(pl.program_id(1) == 0)
    def _():
        buf[0:CONV_HALO, :] = jnp.zeros((CONV_HALO, buf.shape[1]), F32)

    @pl.when(pl.program_id(1) != 0)
    def _():
        buf[0:CONV_HALO, :] = halo_ref[...]

    buf[CONV_HALO:, :] = h_ref[...]
    d_conv = buf.shape[1]
    for s in range(1, SUBLANES):
        shifted[s - 1] = buf[s:s + shifted.shape[1], :]
    off = CONV_HALO - (CONV_WIDTH - 1)
    for c in range(d_conv // LANES):
        cs = slice(c * LANES, (c + 1) * LANES)
        for r in range(tm // CONV_ROW_CHUNK):
            acc = jnp.zeros((CONV_ROW_CHUNK, LANES), F32)
            for j in range(CONV_WIDTH):
                s = (off + j) % SUBLANES
                start = r * CONV_ROW_CHUNK + off + j - s
                src = buf if s == 0 else shifted.at[s - 1]
                acc = acc + cw_ref[j:j + 1, cs] * src[start:start + CONV_ROW_CHUNK, cs]
            cbuf[r * CONV_ROW_CHUNK:(r + 1) * CONV_ROW_CHUNK, cs] = acc + cb_ref[:, cs]
    conv = _layer_norm(cbuf[...], cg_ref[...], cbeta_ref[...])
    conv = conv * jax.nn.sigmoid(conv)

    mix = _dot(attn_ref[...], wo_ref[0:d_attn, :]) + _dot(conv.astype(BF16), wo_ref[d_attn:, :])
    x1 = _layer_norm(ALPHA * x_ref[...] + mix, g1_ref[...], b1_ref[...])
    x1_ref[...] = x1

    xh = x1.astype(BF16)
    xl = (x1 - xh.astype(F32)).astype(BF16)
    logits = _dot(xh, wrh_ref[...]) + _dot(xl, wrh_ref[...]) + _dot(xh, wrl_ref[...])
    scores = jax.nn.sigmoid(logits).T
    sel, gates = _route(scores, rb_ref[...])
    gate_ref[...] = gates

    t_row = lax.broadcasted_iota(I32, (tm, tm), 0)
    t_col = lax.broadcasted_iota(I32, (tm, tm), 1)
    before = jnp.where(t_row < t_col, 1.0, 0.0).astype(BF16)
    sel_f = jnp.where(sel, 1.0, 0.0).astype(F32)
    rank = _dot(sel_f.astype(BF16), before) + carry[...]
    rank_ref[...] = jnp.where(sel, rank, -1.0).astype(I32)
    carry[...] = carry[...] + jnp.sum(sel_f, axis=1, keepdims=True)
    cnt_ref[...] = carry[...].astype(I32)


def _mix(h, attn, x, conv_w, conv_b, cln_g, cln_b, w_out_bf16, ln1_g, ln1_b, w_router, router_bias):
    B, S, D = x.shape
    d_conv = h.shape[-1]
    d_attn = attn.shape[-1]
    tm = MIX_ROWS
    nt = S // tm
    N = B * S
    n_e = w_router.shape[-1]
    wr_hi = w_router.astype(BF16)
    wr_lo = (w_router - wr_hi.astype(F32)).astype(BF16)
    halo_blocks = tm // CONV_HALO
    row = lambda a: a.reshape(1, -1)
    const = lambda shape: pl.BlockSpec(shape, lambda b, i: (0,) * len(shape))
    tiles_per_span = MOE_SPAN // tm
    kern = functools.partial(_mix_kernel, tm=tm, d_attn=d_attn, tiles_per_span=tiles_per_span)
    return pl.pallas_call(
        kern,
        grid=(B, nt),
        in_specs=[
            pl.BlockSpec((None, tm, d_conv), lambda b, i: (b, i, 0)),
            pl.BlockSpec((None, CONV_HALO, d_conv), lambda b, i: (b, jnp.maximum(i * halo_blocks - 1, 0), 0)),
            const(conv_w.shape), const((1, d_conv)), const((1, d_conv)), const((1, d_conv)),
            pl.BlockSpec((None, tm, d_attn), lambda b, i: (b, i, 0)),
            const(w_out_bf16.shape),
            pl.BlockSpec((None, tm, D), lambda b, i: (b, i, 0)),
            const((1, D)), const((1, D)),
            const(wr_hi.shape), const(wr_lo.shape), const((n_e, 1)),
        ],
        out_specs=[
            pl.BlockSpec((tm, D), lambda b, i: (b * nt + i, 0)),
            pl.BlockSpec((n_e, tm), lambda b, i: (0, b * nt + i)),
            pl.BlockSpec((n_e, tm), lambda b, i: (0, b * nt + i)),
            pl.BlockSpec((None, n_e, 1), lambda b, i: (b * nt + i, 0, 0)),
        ],
        out_shape=[
            jax.ShapeDtypeStruct((N, D), F32),
            jax.ShapeDtypeStruct((n_e, N), I32),
            jax.ShapeDtypeStruct((n_e, N), F32),
            jax.ShapeDtypeStruct((B * nt, n_e, 1), I32),
        ],
        scratch_shapes=[
            pltpu.VMEM((tm + CONV_HALO, d_conv), F32),
            pltpu.VMEM((SUBLANES - 1, tm + CONV_HALO - SUBLANES, d_conv), F32),
            pltpu.VMEM((tm, d_conv), F32),
            pltpu.VMEM((n_e, 1), F32),
        ],
        compiler_params=pltpu.CompilerParams(
            dimension_semantics=("arbitrary", "arbitrary"), vmem_limit_bytes=VMEM_LIMIT),
        name="mix",
    )(h, h, conv_w, row(conv_b), row(cln_g), row(cln_b), attn, w_out_bf16, x, row(ln1_g), row(ln1_b),
      wr_hi, wr_lo, router_bias.reshape(n_e, 1))


def _moe_kernel(ntrip_ref, x1_ref, rank_ref, gate_ref, wgu_ref, wd_ref, wsgu_ref, wsd_ref,
                g2_ref, b2_ref, o_ref, xb_ref, *, tiers, d_e, d_s, per_step, per_trip, span):
    w = pl.program_id(0)
    step = pl.program_id(1)

    @pl.when(step == 0)
    def _():
        x1 = x1_ref[...]
        xb = x1.astype(BF16)
        xb_ref[...] = xb
        gu = _dot(xb, wsgu_ref[...])
        g = gu[:, :d_s]
        shared = _dot((g * jax.nn.sigmoid(g) * gu[:, d_s:]).astype(BF16), wsd_ref[...])
        o_ref[...] = ALPHA * x1 + shared

    n_spans = xb_ref.shape[0] // span

    def trip(c, carry, first, rows):
        onehots, gathered = [], []
        slot_gates = [[None] * n_spans for _ in range(per_trip)]
        for s in range(n_spans):
            lo, hi = s * span, (s + 1) * span
            hits = []
            for k in range(per_trip):
                e = step * per_step + first + k
                rank = rank_ref[pl.ds(e, 1), lo:hi]
                hit = (lax.broadcasted_iota(I32, (rows, 1), 0) + c * rows) == rank
                hits.append(hit)
                slot_gates[k][s] = jnp.sum(jnp.where(hit, gate_ref[pl.ds(e, 1), lo:hi], 0.0), axis=1, keepdims=True)
            onehot = jnp.where(jnp.concatenate(hits, axis=0), 1.0, 0.0).astype(BF16)
            onehots.append(onehot)
            gathered.append(_dot(onehot, xb_ref[lo:hi, :]))
        gated = []
        for k in range(per_trip):
            sl = slice(k * rows, (k + 1) * rows)
            xs = jnp.concatenate([gathered[s][sl] for s in range(n_spans)], axis=0).astype(BF16)
            gu = _dot(xs, wgu_ref[first + k])
            g = gu[:, :d_e]
            y = _dot((g * jax.nn.sigmoid(g) * gu[:, d_e:]).astype(BF16), wd_ref[first + k])
            gated.append(y * jnp.concatenate(slot_gates[k], axis=0))
        for s in range(n_spans):
            sl = slice(s * rows, (s + 1) * rows)
            yg = jnp.concatenate([gated[k][sl] for k in range(per_trip)], axis=0).astype(BF16)
            o_ref[s * span:(s + 1) * span, :] += lax.dot_general(
                onehots[s], yg, (((0,), (0,)), ((), ())), preferred_element_type=F32)
        return carry

    quads_per_step = per_step // per_trip

    def quad(t, carry):
        q = (w * pl.num_programs(1) + step) * quads_per_step + t
        for tier, rows in enumerate(tiers):
            lax.fori_loop(0, ntrip_ref[q * len(tiers) + tier],
                          functools.partial(trip, first=t * per_trip, rows=rows), 0)
        return carry

    lax.fori_loop(0, quads_per_step, quad, 0)

    @pl.when(step == pl.num_programs(1) - 1)
    def _():
        o_ref[...] = _layer_norm(o_ref[...], g2_ref[...], b2_ref[...])


def _moe(x1, rank, gates, counts, wgu, wd, ws_gate, ws_up, ws_down, ln2_g, ln2_b):
    N, D = x1.shape
    n_e, d_e, _ = wd.shape
    d_s = ws_down.shape[0]
    W = MOE_WINDOW
    n_win = N // W
    tiers = EXPERT_ROW_TIERS
    span = MOE_SPAN
    per_step, per_trip = MOE_EXPERTS_PER_STEP, MOE_EXPERTS_PER_TRIP
    n_steps = n_e // per_step
    assert len(tiers) >= 2 and all(r % SUBLANES == 0 and (per_trip * r) % (2 * SUBLANES) == 0 for r in tiers)
    assert tiers == tuple(sorted(tiers)) and n_e % per_step == 0 and per_step % per_trip == 0
    per_span = counts.reshape(N // span, span // MIX_ROWS, n_e)[:, -1]
    fullest = jnp.max(per_span.reshape(n_win, W // span, n_e // per_trip, per_trip), axis=(1, 3)).reshape(-1)
    lower = jnp.asarray((0,) + tiers[:-1], I32)
    fits = (fullest[:, None] > lower[None, :]) & (fullest[:, None] <= jnp.asarray(tiers, I32)[None, :])
    ntrip = fits.astype(I32).at[:, -1].set(
        jnp.where(fullest > tiers[-2], (fullest + tiers[-1] - 1) // tiers[-1], 0).astype(I32)).reshape(-1)
    wsgu = jnp.concatenate([ws_gate, ws_up], axis=-1).astype(BF16)
    kern = functools.partial(_moe_kernel, tiers=tiers, d_e=d_e, d_s=d_s, per_step=per_step,
                             per_trip=per_trip, span=span)
    const = lambda shape: pl.BlockSpec(shape, lambda w, e, nt: (0,) * len(shape))
    return pl.pallas_call(
        kern,
        grid_spec=pltpu.PrefetchScalarGridSpec(
            num_scalar_prefetch=1,
            grid=(n_win, n_steps),
            in_specs=[
                pl.BlockSpec((W, D), lambda w, e, nt: (w, 0)),
                pl.BlockSpec((n_e, W), lambda w, e, nt: (0, w)),
                pl.BlockSpec((n_e, W), lambda w, e, nt: (0, w)),
                pl.BlockSpec((per_step, D, 2 * d_e), lambda w, e, nt: (e, 0, 0)),
                pl.BlockSpec((per_step, d_e, D), lambda w, e, nt: (e, 0, 0)),
                const(wsgu.shape), const((d_s, D)), const((1, D)), const((1, D)),
            ],
            out_specs=pl.BlockSpec((W, D), lambda w, e, nt: (w, 0)),
            scratch_shapes=[pltpu.VMEM((W, D), BF16)],
        ),
        out_shape=jax.ShapeDtypeStruct((N, D), F32),
        compiler_params=pltpu.CompilerParams(
            dimension_semantics=("arbitrary", "arbitrary"), vmem_limit_bytes=VMEM_LIMIT),
        name="moe",
    )(ntrip, x1, rank, gates, wgu, wd, wsgu, ws_down.astype(BF16),
      ln2_g.reshape(1, D), ln2_b.reshape(1, D))


def kernel(x, positions, w_in, lambda_q1, lambda_k1, lambda_q2, lambda_k2, subln_g, conv_w, conv_b, conv_ln_g, conv_ln_b, w_out, ln1_g, ln1_b, w_router, router_bias, w_exp_gate, w_exp_up, w_exp_down, w_sh_gate, w_sh_up, w_sh_down, ln2_g, ln2_b):
    B, S, D = x.shape
    d_attn = N_HEADS * HEAD_DIM
    d_conv = conv_w.shape[-1]
    for l in range(DEPTH):
        lambda_init = 0.8 - 0.6 * math.exp(-0.3 * l)
        qT, k, vT, h = _proj(x, positions, w_in[l].astype(BF16), d_attn, d_conv)
        attn, wgu, wd = _attention(qT, k, vT, lambda_q1[l], lambda_k1[l], lambda_q2[l], lambda_k2[l], subln_g[l],
                                   lambda_init, w_exp_gate[l], w_exp_up[l], w_exp_down[l])
        x1, rank, gates, counts = _mix(h, attn, x, conv_w[l], conv_b[l], conv_ln_g[l], conv_ln_b[l],
                                       w_out[l].astype(BF16), ln1_g[l], ln1_b[l], w_router[l], router_bias[l])
        out = _moe(x1, rank, gates, counts, wgu, wd, w_sh_gate[l], w_sh_up[l], w_sh_down[l], ln2_g[l], ln2_b[l])
        x = out.reshape(B, S, D)
    return x
```

```python
import functools
import math

import jax
import jax.numpy as jnp
from jax import lax
from jax.experimental import pallas as pl
from jax.experimental.pallas import tpu as pltpu

F32 = jnp.float32
BF16 = jnp.bfloat16
I32 = jnp.int32

N_HEADS = 4
HEAD_DIM = 128
SUB_DIM = 64
ROPE_DIM = 16
ROPE_THETA = 500000.0
CHUNK = 64
CONV_WIDTH = 31
N_EXPERTS = 64
TOP_K = 8
N_GROUPS = 8
GROUP_SIZE = N_EXPERTS // N_GROUPS
TOPK_GROUPS = 4
ROUTED_SCALE = 2.5
LN_EPS = 1e-5
DEPTH = 1
ALPHA = (2.0 * DEPTH) ** 0.25
LOG2_E = math.log2(math.e)

LANES = 128
SUBLANES = 8
VMEM_LIMIT = 56 * 1024 * 1024

PROJ_ROWS = 512
ATTN_BLOCK = 256
MIX_ROWS = 256
CONV_HALO = 32
CONV_ROW_CHUNK = 64
MOE_WINDOW = 1024
MOE_SPAN = 256
EXPERT_ROW_TIERS = (40, 48, 56, 64)
MOE_EXPERTS_PER_STEP = 4


def _dot(a, b):
    return jnp.dot(a, b, preferred_element_type=F32)


def _layer_norm(z, g, b):
    mu = jnp.mean(z, axis=-1, keepdims=True)
    zc = z - mu
    var = jnp.mean(zc * zc, axis=-1, keepdims=True)
    return zc * lax.rsqrt(var + LN_EPS) * g + b


def _proj_kernel(x_ref, pos_ref, invf_ref, w_ref, qT_ref, k_ref, vT_ref, h_ref, *, d_attn, d_conv, blk):
    xb = x_ref[...].astype(BF16)
    tm = xb.shape[0]
    ang = pos_ref[...].astype(F32) * invf_ref[...]
    cos = jnp.cos(ang)
    sin = jnp.sin(ang)
    d = lax.broadcasted_iota(I32, (1, LANES), 1) & (SUB_DIM - 1)
    half = ROPE_DIM // 2
    c_mul = jnp.where(d < ROPE_DIM, cos, 1.0)
    s_lo = jnp.where(d < half, -sin, 0.0)
    s_hi = jnp.where((d >= half) & (d < ROPE_DIM), sin, 0.0)

    def rope(t):
        outs = []
        for c in range(t.shape[1] // LANES):
            ts = t[:, c * LANES:(c + 1) * LANES]
            outs.append(ts * c_mul + pltpu.roll(ts, LANES - half, 1) * s_lo + pltpu.roll(ts, half, 1) * s_hi)
        return jnp.concatenate(outs, axis=1)

    q = rope(_dot(xb, w_ref[:, 0:d_attn])) * (SUB_DIM ** -0.5 * LOG2_E)
    for c in range(tm // blk):
        qT_ref[0, c] = q[c * blk:(c + 1) * blk, :].T.astype(BF16)
    k = rope(_dot(xb, w_ref[:, d_attn:2 * d_attn]))
    k_ref[0] = k.astype(BF16)
    v = _dot(xb, w_ref[:, 2 * d_attn:3 * d_attn])
    for c in range(tm // blk):
        vT_ref[0, c] = v[c * blk:(c + 1) * blk, :].T.astype(BF16)
    a = _dot(xb, w_ref[:, 3 * d_attn:3 * d_attn + d_conv])
    gate = _dot(xb, w_ref[:, 3 * d_attn + d_conv:3 * d_attn + 2 * d_conv])
    h_ref[0] = a * jax.nn.sigmoid(gate)


def _proj(x, positions, w_in_bf16, d_attn, d_conv):
    B, S, D = x.shape
    tm, blk = PROJ_ROWS, ATTN_BLOCK
    nb = S // blk
    inv_freq = ROPE_THETA ** (-jnp.arange(0, ROPE_DIM, 2, dtype=F32) / ROPE_DIM)
    sub = jnp.concatenate([inv_freq, inv_freq, jnp.zeros((SUB_DIM - ROPE_DIM,), F32)])
    invf = jnp.tile(sub, LANES // SUB_DIM).reshape(1, LANES)
    d_in = w_in_bf16.shape[1]
    kern = functools.partial(_proj_kernel, d_attn=d_attn, d_conv=d_conv, blk=blk)
    return pl.pallas_call(
        kern,
        grid=(B, S // tm),
        in_specs=[
            pl.BlockSpec((None, tm, D), lambda b, i: (b, i, 0)),
            pl.BlockSpec((None, tm, 1), lambda b, i: (b, i, 0)),
            pl.BlockSpec((1, LANES), lambda b, i: (0, 0)),
            pl.BlockSpec((D, d_in), lambda b, i: (0, 0)),
        ],
        out_specs=[
            pl.BlockSpec((1, tm // blk, d_attn, blk), lambda b, i: (b, i, 0, 0)),
            pl.BlockSpec((1, tm, d_attn), lambda b, i: (b, i, 0)),
            pl.BlockSpec((1, tm // blk, d_attn, blk), lambda b, i: (b, i, 0, 0)),
            pl.BlockSpec((1, tm, d_conv), lambda b, i: (b, i, 0)),
        ],
        out_shape=[
            jax.ShapeDtypeStruct((B, nb, d_attn, blk), BF16),
            jax.ShapeDtypeStruct((B, S, d_attn), BF16),
            jax.ShapeDtypeStruct((B, nb, d_attn, blk), BF16),
            jax.ShapeDtypeStruct((B, S, d_conv), F32),
        ],
        compiler_params=pltpu.CompilerParams(
            dimension_semantics=("parallel", "parallel"), vmem_limit_bytes=VMEM_LIMIT),
        name="proj",
    )(x, positions.reshape(B, S, 1), invf, w_in_bf16)


def _attn_kernel(lq1_ref, lk1_ref, lq2_ref, lk2_ref, g_ref, qT_ref, k_ref, vT_ref, wg_ref, wu_ref, wd_ref,
                 o_ref, wgu_out, wd_out, q_both, s_a, s_b, p_a, p_b, acc_ref, m_ref, *, blk, lambda_init):
    i = pl.program_id(1)
    d_e = wg_ref.shape[-1]
    wgu_out[:, :, :d_e] = wg_ref[...].astype(wgu_out.dtype)
    wgu_out[:, :, d_e:] = wu_ref[...].astype(wgu_out.dtype)
    wd_out[...] = wd_ref[...].astype(wd_out.dtype)
    lam = (jnp.exp(jnp.sum(lq1_ref[...] * lk1_ref[...], axis=-1, keepdims=True))
           - jnp.exp(jnp.sum(lq2_ref[...] * lk2_ref[...], axis=-1, keepdims=True)) + lambda_init)
    row = lax.broadcasted_iota(I32, (HEAD_DIM, 1), 0)
    ones = jnp.ones((2 * SUBLANES, blk), BF16)
    heads = range(N_HEADS)

    def head_cols(h):
        return slice(h * HEAD_DIM, (h + 1) * HEAD_DIM)

    for h in heads:
        qT = qT_ref[0, 0, head_cols(h), :]
        zero = jnp.zeros_like(qT)
        q_both[h] = jnp.concatenate([jnp.where(row < SUB_DIM, qT, zero), jnp.where(row >= SUB_DIM, qT, zero)], axis=1)

    def scores_into(h, j, s_ref):
        kb = k_ref[0, pl.ds(pl.multiple_of(j * blk, blk), blk), head_cols(h)]
        s_ref[h] = _dot(kb, q_both[h])

    def weighted_values(h, j, p):
        return _dot(jnp.concatenate([vT_ref[0, j, head_cols(h), :], ones], axis=0), p)

    def probs(h, s):
        m = m_ref[h]
        m_new = jnp.maximum(m, jnp.max(s, axis=0, keepdims=True))
        m_ref[h] = m_new
        return jnp.exp2(m - m_new), jnp.exp2(s - m_new).astype(BF16)

    def step(j, s_cur, s_nxt, p_prev, p_cur):
        for h in heads:
            scores_into(h, j + 1, s_nxt)
            pv_prev = weighted_values(h, jnp.maximum(j - 1, 0), p_prev[h])
            alpha, p = probs(h, s_cur[h])
            p_cur[h] = p
            acc_ref[h] = alpha * (acc_ref[h] + pv_prev)

    def finish(s_cur, p_prev):
        kc = lax.broadcasted_iota(I32, (blk, 1), 0) // CHUNK
        qc = (lax.broadcasted_iota(I32, (1, 2 * blk), 1) & (blk - 1)) // CHUNK
        for h in heads:
            pv_prev = weighted_values(h, jnp.maximum(i - 1, 0), p_prev[h])
            alpha, p = probs(h, jnp.where(kc <= qc, s_cur[h], -jnp.inf))
            acc = alpha * (acc_ref[h] + pv_prev) + weighted_values(h, i, p)
            o = acc[:HEAD_DIM] * (1.0 / acc[HEAD_DIM:HEAD_DIM + 1])
            o = o[:, :blk] - lam * o[:, blk:]
            o = o * lax.rsqrt(jnp.mean(o * o, axis=0, keepdims=True) + LN_EPS)
            o = o * g_ref[...] * (1.0 - lambda_init)
            o_ref[0, :, head_cols(h)] = o.T.astype(o_ref.dtype)

    m_ref[...] = jnp.full(m_ref.shape, -jnp.inf, F32)
    acc_ref[...] = jnp.zeros(acc_ref.shape, F32)
    p_b[...] = jnp.zeros(p_b.shape, BF16)
    for h in heads:
        scores_into(h, 0, s_a)

    def pair(jj, c):
        step(2 * jj, s_a, s_b, p_b, p_a)
        step(2 * jj + 1, s_b, s_a, p_a, p_b)
        return c

    lax.fori_loop(0, i // 2, pair, 0)

    @pl.when(i % 2 == 1)
    def _():
        step(i - 1, s_a, s_b, p_b, p_a)
        finish(s_b, p_a)

    @pl.when(i % 2 == 0)
    def _():
        finish(s_a, p_b)


def _attention(qT, k, vT, lq1, lk1, lq2, lk2, sub_g, lambda_init, w_gate, w_up, w_down):
    B, nb, d_attn, blk = qT.shape
    S = nb * blk
    n_e, D, d_e = w_gate.shape
    per_step = n_e // (B * nb)
    assert per_step * B * nb == n_e
    kern = functools.partial(_attn_kernel, blk=blk, lambda_init=lambda_init)
    vec = pl.BlockSpec((1, SUB_DIM), lambda b, i: (0, 0))
    w_slice = lambda shape: pl.BlockSpec((per_step,) + shape, lambda b, i: (b * nb + i, 0, 0))
    return pl.pallas_call(
        kern,
        grid=(B, nb),
        in_specs=[
            vec, vec, vec, vec,
            pl.BlockSpec((HEAD_DIM, 1), lambda b, i: (0, 0)),
            pl.BlockSpec((1, 1, d_attn, blk), lambda b, i: (b, i, 0, 0)),
            pl.BlockSpec((1, S, d_attn), lambda b, i: (b, 0, 0)),
            pl.BlockSpec((1, nb, d_attn, blk), lambda b, i: (b, 0, 0, 0)),
            w_slice((D, d_e)), w_slice((D, d_e)), w_slice((d_e, D)),
        ],
        out_specs=[
            pl.BlockSpec((1, blk, d_attn), lambda b, i: (b, i, 0)),
            w_slice((D, 2 * d_e)), w_slice((d_e, D)),
        ],
        out_shape=[
            jax.ShapeDtypeStruct((B, S, d_attn), BF16),
            jax.ShapeDtypeStruct((n_e, D, 2 * d_e), BF16),
            jax.ShapeDtypeStruct((n_e, d_e, D), BF16),
        ],
        scratch_shapes=[
            pltpu.VMEM((N_HEADS, HEAD_DIM, 2 * blk), BF16),
            pltpu.VMEM((N_HEADS, blk, 2 * blk), F32), pltpu.VMEM((N_HEADS, blk, 2 * blk), F32),
            pltpu.VMEM((N_HEADS, blk, 2 * blk), BF16), pltpu.VMEM((N_HEADS, blk, 2 * blk), BF16),
            pltpu.VMEM((N_HEADS, HEAD_DIM + 2 * SUBLANES, 2 * blk), F32),
            pltpu.VMEM((N_HEADS, 1, 2 * blk), F32),
        ],
        compiler_params=pltpu.CompilerParams(
            dimension_semantics=("parallel", "arbitrary"), vmem_limit_bytes=VMEM_LIMIT),
        name="attn",
    )(lq1.reshape(1, SUB_DIM), lk1.reshape(1, SUB_DIM), lq2.reshape(1, SUB_DIM), lk2.reshape(1, SUB_DIM),
      sub_g.reshape(HEAD_DIM, 1), qT, k, vT, w_gate, w_up, w_down)


def _route(scores, bias):
    n_e, tm = scores.shape
    neg = -jnp.inf
    biased = scores + bias
    g3 = biased.reshape(N_GROUPS, GROUP_SIZE, tm)
    io3 = lax.broadcasted_iota(I32, g3.shape, 1)
    m1 = jnp.max(g3, axis=1, keepdims=True)
    first = jnp.min(jnp.where(g3 == m1, io3, GROUP_SIZE), axis=1, keepdims=True)
    m2 = jnp.max(jnp.where(io3 == first, neg, g3), axis=1, keepdims=True)
    grp_score = (m1 + m2).reshape(N_GROUPS, tm)
    gio = lax.broadcasted_iota(I32, (N_GROUPS, tm), 0)
    grp_sel = jnp.zeros((N_GROUPS, tm), jnp.bool_)
    cur = grp_score
    for _ in range(TOPK_GROUPS):
        mx = jnp.max(cur, axis=0, keepdims=True)
        f = jnp.min(jnp.where(cur == mx, gio, N_GROUPS), axis=0, keepdims=True)
        hit = gio == f
        grp_sel = grp_sel | hit
        cur = jnp.where(hit, neg, cur)
    grp_f = jnp.where(grp_sel, 1.0, 0.0).astype(F32)
    exp_mask = jnp.broadcast_to(grp_f.reshape(N_GROUPS, 1, tm), (N_GROUPS, GROUP_SIZE, tm)).reshape(n_e, tm) > 0.5
    eio = lax.broadcasted_iota(I32, (n_e, tm), 0)
    cur = jnp.where(exp_mask, biased, neg)
    sel = jnp.zeros((n_e, tm), jnp.bool_)
    for _ in range(TOP_K):
        mx = jnp.max(cur, axis=0, keepdims=True)
        f = jnp.min(jnp.where(cur == mx, eio, n_e), axis=0, keepdims=True)
        hit = eio == f
        sel = sel | hit
        cur = jnp.where(hit, neg, cur)
    denom = jnp.sum(jnp.where(sel, scores, 0.0), axis=0, keepdims=True)
    gates = jnp.where(sel, scores / denom * ROUTED_SCALE, 0.0)
    return sel, gates


def _mix_kernel(h_ref, halo_ref, cw_ref, cb_ref, cg_ref, cbeta_ref, attn_ref, wo_ref, x_ref, g1_ref, b1_ref,
                wrh_ref, wrl_ref, rb_ref,
                x1_ref, rank_ref, gate_ref, cnt_ref,
                buf, shifted, cbuf, carry, *, tm, d_attn, tiles_per_span):
    tile = pl.program_id(0) * pl.num_programs(1) + pl.program_id(1)

    @pl.when(tile % tiles_per_span == 0)
    def _():
        carry[...] = jnp.zeros_like(carry)

    @pl.when(pl.program_id(1) == 0)
    def _():
        buf[0:CONV_HALO, :] = jnp.zeros((CONV_HALO, buf.shape[1]), F32)

    @pl.when(pl.program_id(1) != 0)
    def _():
        buf[0:CONV_HALO, :] = halo_ref[...]

    buf[CONV_HALO:, :] = h_ref[...]
    d_conv = buf.shape[1]
    for s in range(1, SUBLANES):
        shifted[s - 1] = buf[s:s + shifted.shape[1], :]
    off = CONV_HALO - (CONV_WIDTH - 1)
    for c in range(d_conv // LANES):
        cs = slice(c * LANES, (c + 1) * LANES)
        for r in range(tm // CONV_ROW_CHUNK):
            acc = jnp.zeros((CONV_ROW_CHUNK, LANES), F32)
            for j in range(CONV_WIDTH):
                s = (off + j) % SUBLANES
                start = r * CONV_ROW_CHUNK + off + j - s
                src = buf if s == 0 else shifted.at[s - 1]
                acc = acc + cw_ref[j:j + 1, cs] * src[start:start + CONV_ROW_CHUNK, cs]
            cbuf[r * CONV_ROW_CHUNK:(r + 1) * CONV_ROW_CHUNK, cs] = acc + cb_ref[:, cs]
    conv = _layer_norm(cbuf[...], cg_ref[...], cbeta_ref[...])
    conv = conv * jax.nn.sigmoid(conv)

    mix = _dot(attn_ref[...], wo_ref[0:d_attn, :]) + _dot(conv.astype(BF16), wo_ref[d_attn:, :])
    x1 = _layer_norm(ALPHA * x_ref[...] + mix, g1_ref[...], b1_ref[...])
    x1_ref[...] = x1

    xh = x1.astype(BF16)
    xl = (x1 - xh.astype(F32)).astype(BF16)
    logits = _dot(xh, wrh_ref[...]) + _dot(xl, wrh_ref[...]) + _dot(xh, wrl_ref[...])
    scores = jax.nn.sigmoid(logits).T
    sel, gates = _route(scores, rb_ref[...])
    gate_ref[...] = gates

    t_row = lax.broadcasted_iota(I32, (tm, tm), 0)
    t_col = lax.broadcasted_iota(I32, (tm, tm), 1)
    before = jnp.where(t_row < t_col, 1.0, 0.0).astype(BF16)
    sel_f = jnp.where(sel, 1.0, 0.0).astype(F32)
    rank = _dot(sel_f.astype(BF16), before) + carry[...]
    rank_ref[...] = jnp.where(sel, rank, -1.0).astype(I32)
    carry[...] = carry[...] + jnp.sum(sel_f, axis=1, keepdims=True)
    cnt_ref[...] = carry[...].astype(I32)


def _mix(h, attn, x, conv_w, conv_b, cln_g, cln_b, w_out_bf16, ln1_g, ln1_b, w_router, router_bias):
    B, S, D = x.shape
    d_conv = h.shape[-1]
    d_attn = attn.shape[-1]
    tm = MIX_ROWS
    nt = S // tm
    N = B * S
    n_e = w_router.shape[-1]
    wr_hi = w_router.astype(BF16)
    wr_lo = (w_router - wr_hi.astype(F32)).astype(BF16)
    halo_blocks = tm // CONV_HALO
    row = lambda a: a.reshape(1, -1)
    const = lambda shape: pl.BlockSpec(shape, lambda b, i: (0,) * len(shape))
    tiles_per_span = MOE_SPAN // tm
    kern = functools.partial(_mix_kernel, tm=tm, d_attn=d_attn, tiles_per_span=tiles_per_span)
    return pl.pallas_call(
        kern,
        grid=(B, nt),
        in_specs=[
            pl.BlockSpec((None, tm, d_conv), lambda b, i: (b, i, 0)),
            pl.BlockSpec((None, CONV_HALO, d_conv), lambda b, i: (b, jnp.maximum(i * halo_blocks - 1, 0), 0)),
            const(conv_w.shape), const((1, d_conv)), const((1, d_conv)), const((1, d_conv)),
            pl.BlockSpec((None, tm, d_attn), lambda b, i: (b, i, 0)),
            const(w_out_bf16.shape),
            pl.BlockSpec((None, tm, D), lambda b, i: (b, i, 0)),
            const((1, D)), const((1, D)),
            const(wr_hi.shape), const(wr_lo.shape), const((n_e, 1)),
        ],
        out_specs=[
            pl.BlockSpec((tm, D), lambda b, i: (b * nt + i, 0)),
            pl.BlockSpec((n_e, tm), lambda b, i: (0, b * nt + i)),
            pl.BlockSpec((n_e, tm), lambda b, i: (0, b * nt + i)),
            pl.BlockSpec((None, n_e, 1), lambda b, i: (b * nt + i, 0, 0)),
        ],
        out_shape=[
            jax.ShapeDtypeStruct((N, D), F32),
            jax.ShapeDtypeStruct((n_e, N), I32),
            jax.ShapeDtypeStruct((n_e, N), F32),
            jax.ShapeDtypeStruct((B * nt, n_e, 1), I32),
        ],
        scratch_shapes=[
            pltpu.VMEM((tm + CONV_HALO, d_conv), F32),
            pltpu.VMEM((SUBLANES - 1, tm + CONV_HALO - SUBLANES, d_conv), F32),
            pltpu.VMEM((tm, d_conv), F32),
            pltpu.VMEM((n_e, 1), F32),
        ],
        compiler_params=pltpu.CompilerParams(
            dimension_semantics=("arbitrary", "arbitrary"), vmem_limit_bytes=VMEM_LIMIT),
        name="mix",
    )(h, h, conv_w, row(conv_b), row(cln_g), row(cln_b), attn, w_out_bf16, x, row(ln1_g), row(ln1_b),
      wr_hi, wr_lo, router_bias.reshape(n_e, 1))


def _moe_kernel(ntrip_ref, order_ref, x1_ref, rank_ref, gate_ref, *refs, tiers, d_e, d_s, per_step, span):
    wgu_refs, wd_refs = refs[:per_step], refs[per_step:2 * per_step]
    wsgu_ref, wsd_ref, g2_ref, b2_ref, o_ref, xb_ref = refs[2 * per_step:]
    w = pl.program_id(0)
    step = pl.program_id(1)
    q = w * pl.num_programs(1) + step

    @pl.when(step == 0)
    def _():
        x1 = x1_ref[...]
        xb = x1.astype(BF16)
        xb_ref[...] = xb
        gu = _dot(xb, wsgu_ref[...])
        g = gu[:, :d_s]
        shared = _dot((g * jax.nn.sigmoid(g) * gu[:, d_s:]).astype(BF16), wsd_ref[...])
        o_ref[...] = ALPHA * x1 + shared

    n_spans = xb_ref.shape[0] // span
    experts = [order_ref[q * per_step + k] for k in range(per_step)]

    def trip(c, carry, rows):
        onehots, gathered = [], []
        slot_gates = [[None] * n_spans for _ in range(per_step)]
        for s in range(n_spans):
            lo, hi = s * span, (s + 1) * span
            hits = []
            for k, e in enumerate(experts):
                rank = rank_ref[pl.ds(e, 1), lo:hi]
                hit = (lax.broadcasted_iota(I32, (rows, 1), 0) + c * rows) == rank
                hits.append(hit)
                slot_gates[k][s] = jnp.sum(jnp.where(hit, gate_ref[pl.ds(e, 1), lo:hi], 0.0), axis=1, keepdims=True)
            onehot = jnp.where(jnp.concatenate(hits, axis=0), 1.0, 0.0).astype(BF16)
            onehots.append(onehot)
            gathered.append(_dot(onehot, xb_ref[lo:hi, :]))
        gated = []
        for k in range(per_step):
            sl = slice(k * rows, (k + 1) * rows)
            xs = jnp.concatenate([gathered[s][sl] for s in range(n_spans)], axis=0).astype(BF16)
            gu = _dot(xs, wgu_refs[k][0])
            g = gu[:, :d_e]
            y = _dot((g * jax.nn.sigmoid(g) * gu[:, d_e:]).astype(BF16), wd_refs[k][0])
            gated.append(y * jnp.concatenate(slot_gates[k], axis=0))
        for s in range(n_spans):
            sl = slice(s * rows, (s + 1) * rows)
            yg = jnp.concatenate([gated[k][sl] for k in range(per_step)], axis=0).astype(BF16)
            o_ref[s * span:(s + 1) * span, :] += lax.dot_general(
                onehots[s], yg, (((0,), (0,)), ((), ())), preferred_element_type=F32)
        return carry

    for tier, rows in enumerate(tiers):
        lax.fori_loop(0, ntrip_ref[q * len(tiers) + tier], functools.partial(trip, rows=rows), 0)

    @pl.when(step == pl.num_programs(1) - 1)
    def _():
        o_ref[...] = _layer_norm(o_ref[...], g2_ref[...], b2_ref[...])


def _moe(x1, rank, gates, counts, wgu, wd, ws_gate, ws_up, ws_down, ln2_g, ln2_b):
    N, D = x1.shape
    n_e, d_e, _ = wd.shape
    d_s = ws_down.shape[0]
    W = MOE_WINDOW
    n_win = N // W
    tiers = EXPERT_ROW_TIERS
    span = MOE_SPAN
    per_step = MOE_EXPERTS_PER_STEP
    n_steps = n_e // per_step
    assert len(tiers) >= 2 and all(r % SUBLANES == 0 and (per_step * r) % (2 * SUBLANES) == 0 for r in tiers)
    assert tiers == tuple(sorted(tiers)) and n_e % per_step == 0
    per_span = counts.reshape(N // span, span // MIX_ROWS, n_e)[:, -1]
    load = jnp.max(per_span.reshape(n_win, W // span, n_e), axis=1)
    ids = jnp.arange(n_e, dtype=I32)
    before = ((load[:, None, :] < load[:, :, None])
              | ((load[:, None, :] == load[:, :, None]) & (ids[None, None, :] < ids[None, :, None])))
    place = jnp.sum(before, axis=-1).astype(I32)
    at_place = place[:, :, None] == ids[None, None, :]
    order = jnp.sum(jnp.where(at_place, ids[None, :, None], 0), axis=1).astype(I32)
    sorted_load = jnp.sum(jnp.where(at_place, load[:, :, None], 0), axis=1)
    fullest = jnp.max(sorted_load.reshape(n_win, n_steps, per_step), axis=-1).reshape(-1)
    lower = jnp.asarray((0,) + tiers[:-1], I32)
    fits = (fullest[:, None] > lower[None, :]) & (fullest[:, None] <= jnp.asarray(tiers, I32)[None, :])
    ntrip = fits.astype(I32).at[:, -1].set(
        jnp.where(fullest > tiers[-2], (fullest + tiers[-1] - 1) // tiers[-1], 0).astype(I32)).reshape(-1)
    wsgu = jnp.concatenate([ws_gate, ws_up], axis=-1).astype(BF16)
    kern = functools.partial(_moe_kernel, tiers=tiers, d_e=d_e, d_s=d_s, per_step=per_step, span=span)
    const = lambda shape: pl.BlockSpec(shape, lambda w, s, nt, od: (0,) * len(shape))

    def expert_block(shape, k):
        return pl.BlockSpec((1,) + shape, lambda w, s, nt, od: (od[(w * n_steps + s) * per_step + k], 0, 0))

    return pl.pallas_call(
        kern,
        grid_spec=pltpu.PrefetchScalarGridSpec(
            num_scalar_prefetch=2,
            grid=(n_win, n_steps),
            in_specs=[
                pl.BlockSpec((W, D), lambda w, s, nt, od: (w, 0)),
                pl.BlockSpec((n_e, W), lambda w, s, nt, od: (0, w)),
                pl.BlockSpec((n_e, W), lambda w, s, nt, od: (0, w)),
                *[expert_block((D, 2 * d_e), k) for k in range(per_step)],
                *[expert_block((d_e, D), k) for k in range(per_step)],
                const(wsgu.shape), const((d_s, D)), const((1, D)), const((1, D)),
            ],
            out_specs=pl.BlockSpec((W, D), lambda w, s, nt, od: (w, 0)),
            scratch_shapes=[pltpu.VMEM((W, D), BF16)],
        ),
        out_shape=jax.ShapeDtypeStruct((N, D), F32),
        compiler_params=pltpu.CompilerParams(
            dimension_semantics=("arbitrary", "arbitrary"), vmem_limit_bytes=VMEM_LIMIT),
        name="moe",
    )(ntrip, order.reshape(-1), x1, rank, gates, *([wgu] * per_step), *([wd] * per_step),
      wsgu, ws_down.astype(BF16), ln2_g.reshape(1, D), ln2_b.reshape(1, D))


def kernel(x, positions, w_in, lambda_q1, lambda_k1, lambda_q2, lambda_k2, subln_g, conv_w, conv_b, conv_ln_g, conv_ln_b, w_out, ln1_g, ln1_b, w_router, router_bias, w_exp_gate, w_exp_up, w_exp_down, w_sh_gate, w_sh_up, w_sh_down, ln2_g, ln2_b):
    B, S, D = x.shape
    d_attn = N_HEADS * HEAD_DIM
    d_conv = conv_w.shape[-1]
    for l in range(DEPTH):
        lambda_init = 0.8 - 0.6 * math.exp(-0.3 * l)
        qT, k, vT, h = _proj(x, positions, w_in[l].astype(BF16), d_attn, d_conv)
        attn, wgu, wd = _attention(qT, k, vT, lambda_q1[l], lambda_k1[l], lambda_q2[l], lambda_k2[l], subln_g[l],
                                   lambda_init, w_exp_gate[l], w_exp_up[l], w_exp_down[l])
        x1, rank, gates, counts = _mix(h, attn, x, conv_w[l], conv_b[l], conv_ln_g[l], conv_ln_b[l],
                                       w_out[l].astype(BF16), ln1_g[l], ln1_b[l], w_router[l], router_bias[l])
        out = _moe(x1, rank, gates, counts, wgu, wd, w_sh_gate[l], w_sh_up[l], w_sh_down[l], ln2_g[l], ln2_b[l])
        x = out.reshape(B, S, D)
    return x
```

```python
import functools
import math

import jax
import jax.numpy as jnp
from jax import lax
from jax.experimental import pallas as pl
from jax.experimental.pallas import tpu as pltpu

F32 = jnp.float32
BF16 = jnp.bfloat16
I32 = jnp.int32

N_HEADS = 4
HEAD_DIM = 128
SUB_DIM = 64
ROPE_DIM = 16
ROPE_THETA = 500000.0
CHUNK = 64
CONV_WIDTH = 31
N_EXPERTS = 64
TOP_K = 8
N_GROUPS = 8
GROUP_SIZE = N_EXPERTS // N_GROUPS
TOPK_GROUPS = 4
ROUTED_SCALE = 2.5
LN_EPS = 1e-5
DEPTH = 1
ALPHA = (2.0 * DEPTH) ** 0.25
LOG2_E = math.log2(math.e)

LANES = 128
SUBLANES = 8
VMEM_LIMIT = 56 * 1024 * 1024

PROJ_ROWS = 512
ATTN_BLOCK = 256
MIX_ROWS = 256
CONV_HALO = 32
CONV_ROW_CHUNK = 64
MOE_WINDOW = 2048
MOE_SPAN = 256
EXPERT_ROW_TIERS = (48, 56, 64)
MOE_EXPERTS_PER_STEP = 4
MOE_EXPERTS_PER_TRIP = 4


def _dot(a, b):
    return jnp.dot(a, b, preferred_element_type=F32)


def _layer_norm(z, g, b):
    mu = jnp.mean(z, axis=-1, keepdims=True)
    zc = z - mu
    var = jnp.mean(zc * zc, axis=-1, keepdims=True)
    return zc * lax.rsqrt(var + LN_EPS) * g + b


def _proj_kernel(x_ref, pos_ref, invf_ref, w_ref, qT_ref, k_ref, vT_ref, h_ref, *, d_attn, d_conv, blk):
    xb = x_ref[...].astype(BF16)
    tm = xb.shape[0]
    ang = pos_ref[...].astype(F32) * invf_ref[...]
    cos = jnp.cos(ang)
    sin = jnp.sin(ang)
    d = lax.broadcasted_iota(I32, (1, LANES), 1) & (SUB_DIM - 1)
    half = ROPE_DIM // 2
    c_mul = jnp.where(d < ROPE_DIM, cos, 1.0)
    s_lo = jnp.where(d < half, -sin, 0.0)
    s_hi = jnp.where((d >= half) & (d < ROPE_DIM), sin, 0.0)

    def rope(t):
        outs = []
        for c in range(t.shape[1] // LANES):
            ts = t[:, c * LANES:(c + 1) * LANES]
            outs.append(ts * c_mul + pltpu.roll(ts, LANES - half, 1) * s_lo + pltpu.roll(ts, half, 1) * s_hi)
        return jnp.concatenate(outs, axis=1)

    q = rope(_dot(xb, w_ref[:, 0:d_attn])) * (SUB_DIM ** -0.5 * LOG2_E)
    for c in range(tm // blk):
        qT_ref[0, c] = q[c * blk:(c + 1) * blk, :].T.astype(BF16)
    k = rope(_dot(xb, w_ref[:, d_attn:2 * d_attn]))
    k_ref[0] = k.astype(BF16)
    v = _dot(xb, w_ref[:, 2 * d_attn:3 * d_attn])
    for c in range(tm // blk):
        vT_ref[0, c] = v[c * blk:(c + 1) * blk, :].T.astype(BF16)
    a = _dot(xb, w_ref[:, 3 * d_attn:3 * d_attn + d_conv])
    gate = _dot(xb, w_ref[:, 3 * d_attn + d_conv:3 * d_attn + 2 * d_conv])
    h_ref[0] = a * jax.nn.sigmoid(gate)


def _proj(x, positions, w_in_bf16, d_attn, d_conv):
    B, S, D = x.shape
    tm, blk = PROJ_ROWS, ATTN_BLOCK
    nb = S // blk
    inv_freq = ROPE_THETA ** (-jnp.arange(0, ROPE_DIM, 2, dtype=F32) / ROPE_DIM)
    sub = jnp.concatenate([inv_freq, inv_freq, jnp.zeros((SUB_DIM - ROPE_DIM,), F32)])
    invf = jnp.tile(sub, LANES // SUB_DIM).reshape(1, LANES)
    d_in = w_in_bf16.shape[1]
    kern = functools.partial(_proj_kernel, d_attn=d_attn, d_conv=d_conv, blk=blk)
    return pl.pallas_call(
        kern,
        grid=(B, S // tm),
        in_specs=[
            pl.BlockSpec((None, tm, D), lambda b, i: (b, i, 0)),
            pl.BlockSpec((None, tm, 1), lambda b, i: (b, i, 0)),
            pl.BlockSpec((1, LANES), lambda b, i: (0, 0)),
            pl.BlockSpec((D, d_in), lambda b, i: (0, 0)),
        ],
        out_specs=[
            pl.BlockSpec((1, tm // blk, d_attn, blk), lambda b, i: (b, i, 0, 0)),
            pl.BlockSpec((1, tm, d_attn), lambda b, i: (b, i, 0)),
            pl.BlockSpec((1, tm // blk, d_attn, blk), lambda b, i: (b, i, 0, 0)),
            pl.BlockSpec((1, tm, d_conv), lambda b, i: (b, i, 0)),
        ],
        out_shape=[
            jax.ShapeDtypeStruct((B, nb, d_attn, blk), BF16),
            jax.ShapeDtypeStruct((B, S, d_attn), BF16),
            jax.ShapeDtypeStruct((B, nb, d_attn, blk), BF16),
            jax.ShapeDtypeStruct((B, S, d_conv), F32),
        ],
        compiler_params=pltpu.CompilerParams(
            dimension_semantics=("parallel", "parallel"), vmem_limit_bytes=VMEM_LIMIT),
        name="proj",
    )(x, positions.reshape(B, S, 1), invf, w_in_bf16)


def _attn_kernel(lq1_ref, lk1_ref, lq2_ref, lk2_ref, g_ref, qT_ref, k_ref, vT_ref, wg_ref, wu_ref, wd_ref,
                 o_ref, wgu_out, wd_out, q_both, s_a, s_b, p_a, p_b, acc_ref, m_ref, *, blk, lambda_init):
    i = pl.program_id(1)
    d_e = wg_ref.shape[-1]
    wgu_out[:, :, :d_e] = wg_ref[...].astype(wgu_out.dtype)
    wgu_out[:, :, d_e:] = wu_ref[...].astype(wgu_out.dtype)
    wd_out[...] = wd_ref[...].astype(wd_out.dtype)
    lam = (jnp.exp(jnp.sum(lq1_ref[...] * lk1_ref[...], axis=-1, keepdims=True))
           - jnp.exp(jnp.sum(lq2_ref[...] * lk2_ref[...], axis=-1, keepdims=True)) + lambda_init)
    row = lax.broadcasted_iota(I32, (HEAD_DIM, 1), 0)
    ones = jnp.ones((2 * SUBLANES, blk), BF16)
    heads = range(N_HEADS)

    def head_cols(h):
        return slice(h * HEAD_DIM, (h + 1) * HEAD_DIM)

    for h in heads:
        qT = qT_ref[0, 0, head_cols(h), :]
        zero = jnp.zeros_like(qT)
        q_both[h] = jnp.concatenate([jnp.where(row < SUB_DIM, qT, zero), jnp.where(row >= SUB_DIM, qT, zero)], axis=1)

    def scores_into(h, j, s_ref):
        kb = k_ref[0, pl.ds(pl.multiple_of(j * blk, blk), blk), head_cols(h)]
        s_ref[h] = _dot(kb, q_both[h])

    def weighted_values(h, j, p):
        return _dot(jnp.concatenate([vT_ref[0, j, head_cols(h), :], ones], axis=0), p)

    def probs(h, s):
        m = m_ref[h]
        m_new = jnp.maximum(m, jnp.max(s, axis=0, keepdims=True))
        m_ref[h] = m_new
        return jnp.exp2(m - m_new), jnp.exp2(s - m_new).astype(BF16)

    def step(j, s_cur, s_nxt, p_prev, p_cur):
        for h in heads:
            scores_into(h, j + 1, s_nxt)
            pv_prev = weighted_values(h, jnp.maximum(j - 1, 0), p_prev[h])
            alpha, p = probs(h, s_cur[h])
            p_cur[h] = p
            acc_ref[h] = alpha * (acc_ref[h] + pv_prev)

    def finish(s_cur, p_prev):
        kc = lax.broadcasted_iota(I32, (blk, 1), 0) // CHUNK
        qc = (lax.broadcasted_iota(I32, (1, 2 * blk), 1) & (blk - 1)) // CHUNK
        for h in heads:
            pv_prev = weighted_values(h, jnp.maximum(i - 1, 0), p_prev[h])
            alpha, p = probs(h, jnp.where(kc <= qc, s_cur[h], -jnp.inf))
            acc = alpha * (acc_ref[h] + pv_prev) + weighted_values(h, i, p)
            o = acc[:HEAD_DIM] * (1.0 / acc[HEAD_DIM:HEAD_DIM + 1])
            o = o[:, :blk] - lam * o[:, blk:]
            o = o * lax.rsqrt(jnp.mean(o * o, axis=0, keepdims=True) + LN_EPS)
            o = o * g_ref[...] * (1.0 - lambda_init)
            o_ref[0, :, head_cols(h)] = o.T.astype(o_ref.dtype)

    m_ref[...] = jnp.full(m_ref.shape, -jnp.inf, F32)
    acc_ref[...] = jnp.zeros(acc_ref.shape, F32)
    p_b[...] = jnp.zeros(p_b.shape, BF16)
    for h in heads:
        scores_into(h, 0, s_a)

    def pair(jj, c):
        step(2 * jj, s_a, s_b, p_b, p_a)
        step(2 * jj + 1, s_b, s_a, p_a, p_b)
        return c

    lax.fori_loop(0, i // 2, pair, 0)

    @pl.when(i % 2 == 1)
    def _():
        step(i - 1, s_a, s_b, p_b, p_a)
        finish(s_b, p_a)

    @pl.when(i % 2 == 0)
    def _():
        finish(s_a, p_b)


def _attention(qT, k, vT, lq1, lk1, lq2, lk2, sub_g, lambda_init, w_gate, w_up, w_down):
    B, nb, d_attn, blk = qT.shape
    S = nb * blk
    n_e, D, d_e = w_gate.shape
    per_step = n_e // (B * nb)
    assert per_step * B * nb == n_e
    kern = functools.partial(_attn_kernel, blk=blk, lambda_init=lambda_init)
    vec = pl.BlockSpec((1, SUB_DIM), lambda b, i: (0, 0))
    w_slice = lambda shape: pl.BlockSpec((per_step,) + shape, lambda b, i: (b * nb + i, 0, 0))
    return pl.pallas_call(
        kern,
        grid=(B, nb),
        in_specs=[
            vec, vec, vec, vec,
            pl.BlockSpec((HEAD_DIM, 1), lambda b, i: (0, 0)),
            pl.BlockSpec((1, 1, d_attn, blk), lambda b, i: (b, i, 0, 0)),
            pl.BlockSpec((1, S, d_attn), lambda b, i: (b, 0, 0)),
            pl.BlockSpec((1, nb, d_attn, blk), lambda b, i: (b, 0, 0, 0)),
            w_slice((D, d_e)), w_slice((D, d_e)), w_slice((d_e, D)),
        ],
        out_specs=[
            pl.BlockSpec((1, blk, d_attn), lambda b, i: (b, i, 0)),
            w_slice((D, 2 * d_e)), w_slice((d_e, D)),
        ],
        out_shape=[
            jax.ShapeDtypeStruct((B, S, d_attn), BF16),
            jax.ShapeDtypeStruct((n_e, D, 2 * d_e), BF16),
            jax.ShapeDtypeStruct((n_e, d_e, D), BF16),
        ],
        scratch_shapes=[
            pltpu.VMEM((N_HEADS, HEAD_DIM, 2 * blk), BF16),
            pltpu.VMEM((N_HEADS, blk, 2 * blk), F32), pltpu.VMEM((N_HEADS, blk, 2 * blk), F32),
            pltpu.VMEM((N_HEADS, blk, 2 * blk), BF16), pltpu.VMEM((N_HEADS, blk, 2 * blk), BF16),
            pltpu.VMEM((N_HEADS, HEAD_DIM + 2 * SUBLANES, 2 * blk), F32),
            pltpu.VMEM((N_HEADS, 1, 2 * blk), F32),
        ],
        compiler_params=pltpu.CompilerParams(
            dimension_semantics=("parallel", "arbitrary"), vmem_limit_bytes=VMEM_LIMIT),
        name="attn",
    )(lq1.reshape(1, SUB_DIM), lk1.reshape(1, SUB_DIM), lq2.reshape(1, SUB_DIM), lk2.reshape(1, SUB_DIM),
      sub_g.reshape(HEAD_DIM, 1), qT, k, vT, w_gate, w_up, w_down)


def _route(scores, bias):
    n_e, tm = scores.shape
    neg = -jnp.inf
    biased = scores + bias
    g3 = biased.reshape(N_GROUPS, GROUP_SIZE, tm)
    io3 = lax.broadcasted_iota(I32, g3.shape, 1)
    m1 = jnp.max(g3, axis=1, keepdims=True)
    first = jnp.min(jnp.where(g3 == m1, io3, GROUP_SIZE), axis=1, keepdims=True)
    m2 = jnp.max(jnp.where(io3 == first, neg, g3), axis=1, keepdims=True)
    grp_score = (m1 + m2).reshape(N_GROUPS, tm)
    gio = lax.broadcasted_iota(I32, (N_GROUPS, tm), 0)
    grp_sel = jnp.zeros((N_GROUPS, tm), jnp.bool_)
    cur = grp_score
    for _ in range(TOPK_GROUPS):
        mx = jnp.max(cur, axis=0, keepdims=True)
        f = jnp.min(jnp.where(cur == mx, gio, N_GROUPS), axis=0, keepdims=True)
        hit = gio == f
        grp_sel = grp_sel | hit
        cur = jnp.where(hit, neg, cur)
    grp_f = jnp.where(grp_sel, 1.0, 0.0).astype(F32)
    exp_mask = jnp.broadcast_to(grp_f.reshape(N_GROUPS, 1, tm), (N_GROUPS, GROUP_SIZE, tm)).reshape(n_e, tm) > 0.5
    eio = lax.broadcasted_iota(I32, (n_e, tm), 0)
    cur = jnp.where(exp_mask, biased, neg)
    sel = jnp.zeros((n_e, tm), jnp.bool_)
    for _ in range(TOP_K):
        mx = jnp.max(cur, axis=0, keepdims=True)
        f = jnp.min(jnp.where(cur == mx, eio, n_e), axis=0, keepdims=True)
        hit = eio == f
        sel = sel | hit
        cur = jnp.where(hit, neg, cur)
    denom = jnp.sum(jnp.where(sel, scores, 0.0), axis=0, keepdims=True)
    gates = jnp.where(sel, scores / denom * ROUTED_SCALE, 0.0)
    return sel, gates


def _mix_kernel(h_ref, halo_ref, cw_ref, cb_ref, cg_ref, cbeta_ref, attn_ref, wo_ref, x_ref, g1_ref, b1_ref,
                wrh_ref, wrl_ref, rb_ref,
                x1_ref, rank_ref, gate_ref, cnt_ref,
                buf, shifted, cbuf, carry, *, tm, d_attn, tiles_per_span):
    tile = pl.program_id(0) * pl.num_programs(1) + pl.program_id(1)

    @pl.when(tile % tiles_per_span == 0)
    def _():
        carry[...] = jnp.zeros_like(carry)

    @pl.when(pl.program_id(1) == 0)
    def _():
        buf[0:CONV_HALO, :] = jnp.zeros((CONV_HALO, buf.shape[1]), F32)

    @pl.when(pl.program_id(1) != 0)
    def _():
        buf[0:CONV_HALO, :] = halo_ref[...]

    buf[CONV_HALO:, :] = h_ref[...]
    d_conv = buf.shape[1]
    for s in range(1, SUBLANES):
        shifted[s - 1] = buf[s:s + shifted.shape[1], :]
    off = CONV_HALO - (CONV_WIDTH - 1)
    for c in range(d_conv // LANES):
        cs = slice(c * LANES, (c + 1) * LANES)
        for r in range(tm // CONV_ROW_CHUNK):
            acc = jnp.zeros((CONV_ROW_CHUNK, LANES), F32)
            for j in range(CONV_WIDTH):
                s = (off + j) % SUBLANES
                start = r * CONV_ROW_CHUNK + off + j - s
                src = buf if s == 0 else shifted.at[s - 1]
                acc = acc + cw_ref[j:j + 1, cs] * src[start:start + CONV_ROW_CHUNK, cs]
            cbuf[r * CONV_ROW_CHUNK:(r + 1) * CONV_ROW_CHUNK, cs] = acc + cb_ref[:, cs]
    conv = _layer_norm(cbuf[...], cg_ref[...], cbeta_ref[...])
    conv = conv * jax.nn.sigmoid(conv)

    mix = _dot(attn_ref[...], wo_ref[0:d_attn, :]) + _dot(conv.astype(BF16), wo_ref[d_attn:, :])
    x1 = _layer_norm(ALPHA * x_ref[...] + mix, g1_ref[...], b1_ref[...])
    x1_ref[...] = x1

    xh = x1.astype(BF16)
    xl = (x1 - xh.astype(F32)).astype(BF16)
    logits = _dot(xh, wrh_ref[...]) + _dot(xl, wrh_ref[...]) + _dot(xh, wrl_ref[...])
    scores = jax.nn.sigmoid(logits).T
    sel, gates = _route(scores, rb_ref[...])
    gate_ref[...] = gates

    t_row = lax.broadcasted_iota(I32, (tm, tm), 0)
    t_col = lax.broadcasted_iota(I32, (tm, tm), 1)
    before = jnp.where(t_row < t_col, 1.0, 0.0).astype(BF16)
    sel_f = jnp.where(sel, 1.0, 0.0).astype(F32)
    rank = _dot(sel_f.astype(BF16), before) + carry[...]
    rank_ref[...] = jnp.where(sel, rank, -1.0).astype(I32)
    carry[...] = carry[...] + jnp.sum(sel_f, axis=1, keepdims=True)
    cnt_ref[...] = carry[...].astype(I32)


def _mix(h, attn, x, conv_w, conv_b, cln_g, cln_b, w_out_bf16, ln1_g, ln1_b, w_router, router_bias):
    B, S, D = x.shape
    d_conv = h.shape[-1]
    d_attn = attn.shape[-1]
    tm = MIX_ROWS
    nt = S // tm
    N = B * S
    n_e = w_router.shape[-1]
    wr_hi = w_router.astype(BF16)
    wr_lo = (w_router - wr_hi.astype(F32)).astype(BF16)
    halo_blocks = tm // CONV_HALO
    row = lambda a: a.reshape(1, -1)
    const = lambda shape: pl.BlockSpec(shape, lambda b, i: (0,) * len(shape))
    tiles_per_span = MOE_SPAN // tm
    kern = functools.partial(_mix_kernel, tm=tm, d_attn=d_attn, tiles_per_span=tiles_per_span)
    return pl.pallas_call(
        kern,
        grid=(B, nt),
        in_specs=[
            pl.BlockSpec((None, tm, d_conv), lambda b, i: (b, i, 0)),
            pl.BlockSpec((None, CONV_HALO, d_conv), lambda b, i: (b, jnp.maximum(i * halo_blocks - 1, 0), 0)),
            const(conv_w.shape), const((1, d_conv)), const((1, d_conv)), const((1, d_conv)),
            pl.BlockSpec((None, tm, d_attn), lambda b, i: (b, i, 0)),
            const(w_out_bf16.shape),
            pl.BlockSpec((None, tm, D), lambda b, i: (b, i, 0)),
            const((1, D)), const((1, D)),
            const(wr_hi.shape), const(wr_lo.shape), const((n_e, 1)),
        ],
        out_specs=[
            pl.BlockSpec((tm, D), lambda b, i: (b * nt + i, 0)),
            pl.BlockSpec((n_e, tm), lambda b, i: (0, b * nt + i)),
            pl.BlockSpec((n_e, tm), lambda b, i: (0, b * nt + i)),
            pl.BlockSpec((None, n_e, 1), lambda b, i: (b * nt + i, 0, 0)),
        ],
        out_shape=[
            jax.ShapeDtypeStruct((N, D), F32),
            jax.ShapeDtypeStruct((n_e, N), I32),
            jax.ShapeDtypeStruct((n_e, N), F32),
            jax.ShapeDtypeStruct((B * nt, n_e, 1), I32),
        ],
        scratch_shapes=[
            pltpu.VMEM((tm + CONV_HALO, d_conv), F32),
            pltpu.VMEM((SUBLANES - 1, tm + CONV_HALO - SUBLANES, d_conv), F32),
            pltpu.VMEM((tm, d_conv), F32),
            pltpu.VMEM((n_e, 1), F32),
        ],
        compiler_params=pltpu.CompilerParams(
            dimension_semantics=("arbitrary", "arbitrary"), vmem_limit_bytes=VMEM_LIMIT),
        name="mix",
    )(h, h, conv_w, row(conv_b), row(cln_g), row(cln_b), attn, w_out_bf16, x, row(ln1_g), row(ln1_b),
      wr_hi, wr_lo, router_bias.reshape(n_e, 1))


def _moe_kernel(ntrip_ref, x1_ref, rank_ref, gate_ref, wgu_ref, wd_ref, wsgu_ref, wsd_ref,
                g2_ref, b2_ref, o_ref, xb_ref, *, tiers, d_e, d_s, per_step, per_trip, span):
    w = pl.program_id(0)
    step = pl.program_id(1)

    @pl.when(step == 0)
    def _():
        x1 = x1_ref[...]
        xb = x1.astype(BF16)
        xb_ref[...] = xb
        gu = _dot(xb, wsgu_ref[...])
        g = gu[:, :d_s]
        shared = _dot((g * jax.nn.sigmoid(g) * gu[:, d_s:]).astype(BF16), wsd_ref[...])
        o_ref[...] = ALPHA * x1 + shared

    n_spans = xb_ref.shape[0] // span

    def trip(c, carry, first, rows):
        onehots, gathered = [], []
        slot_gates = [[None] * n_spans for _ in range(per_trip)]
        for s in range(n_spans):
            lo, hi = s * span, (s + 1) * span
            hits = []
            for k in range(per_trip):
                e = step * per_step + first + k
                rank = rank_ref[pl.ds(e, 1), lo:hi]
                hit = (lax.broadcasted_iota(I32, (rows, 1), 0) + c * rows) == rank
                hits.append(hit)
                slot_gates[k][s] = jnp.sum(jnp.where(hit, gate_ref[pl.ds(e, 1), lo:hi], 0.0), axis=1, keepdims=True)
            onehot = jnp.where(jnp.concatenate(hits, axis=0), 1.0, 0.0).astype(BF16)
            onehots.append(onehot)
            gathered.append(_dot(onehot, xb_ref[lo:hi, :]))
        gated = []
        for k in range(per_trip):
            sl = slice(k * rows, (k + 1) * rows)
            xs = jnp.concatenate([gathered[s][sl] for s in range(n_spans)], axis=0).astype(BF16)
            gu = _dot(xs, wgu_ref[first + k])
            g = gu[:, :d_e]
            y = _dot((g * jax.nn.sigmoid(g) * gu[:, d_e:]).astype(BF16), wd_ref[first + k])
            gated.append(y * jnp.concatenate(slot_gates[k], axis=0))
        for s in range(n_spans):
            sl = slice(s * rows, (s + 1) * rows)
            yg = jnp.concatenate([gated[k][sl] for k in range(per_trip)], axis=0).astype(BF16)
            o_ref[s * span:(s + 1) * span, :] += lax.dot_general(
                onehots[s], yg, (((0,), (0,)), ((), ())), preferred_element_type=F32)
        return carry

    quads_per_step = per_step // per_trip

    def quad(t, carry):
        q = (w * pl.num_programs(1) + step) * quads_per_step + t
        for tier, rows in enumerate(tiers):
            lax.fori_loop(0, ntrip_ref[q * len(tiers) + tier],
                          functools.partial(trip, first=t * per_trip, rows=rows), 0)
        return carry

    lax.fori_loop(0, quads_per_step, quad, 0)

    @pl.when(step == pl.num_programs(1) - 1)
    def _():
        o_ref[...] = _layer_norm(o_ref[...], g2_ref[...], b2_ref[...])


def _moe(x1, rank, gates, counts, wgu, wd, ws_gate, ws_up, ws_down, ln2_g, ln2_b):
    N, D = x1.shape
    n_e, d_e, _ = wd.shape
    d_s = ws_down.shape[0]
    W = MOE_WINDOW
    n_win = N // W
    tiers = EXPERT_ROW_TIERS
    span = MOE_SPAN
    per_step, per_trip = MOE_EXPERTS_PER_STEP, MOE_EXPERTS_PER_TRIP
    n_steps = n_e // per_step
    assert len(tiers) >= 2 and all(r % SUBLANES == 0 and (per_trip * r) % (2 * SUBLANES) == 0 for r in tiers)
    assert tiers == tuple(sorted(tiers)) and n_e % per_step == 0 and per_step % per_trip == 0
    per_span = counts.reshape(N // span, span // MIX_ROWS, n_e)[:, -1]
    fullest = jnp.max(per_span.reshape(n_win, W // span, n_e // per_trip, per_trip), axis=(1, 3)).reshape(-1)
    lower = jnp.asarray((0,) + tiers[:-1], I32)
    fits = (fullest[:, None] > lower[None, :]) & (fullest[:, None] <= jnp.asarray(tiers, I32)[None, :])
    ntrip = fits.astype(I32).at[:, -1].set(
        jnp.where(fullest > tiers[-2], (fullest + tiers[-1] - 1) // tiers[-1], 0).astype(I32)).reshape(-1)
    wsgu = jnp.concatenate([ws_gate, ws_up], axis=-1).astype(BF16)
    kern = functools.partial(_moe_kernel, tiers=tiers, d_e=d_e, d_s=d_s, per_step=per_step,
                             per_trip=per_trip, span=span)
    const = lambda shape: pl.BlockSpec(shape, lambda w, e, nt: (0,) * len(shape))
    return pl.pallas_call(
        kern,
        grid_spec=pltpu.PrefetchScalarGridSpec(
            num_scalar_prefetch=1,
            grid=(n_win, n_steps),
            in_specs=[
                pl.BlockSpec((W, D), lambda w, e, nt: (w, 0), pipeline_mode=pl.Buffered(1)),
                pl.BlockSpec((n_e, W), lambda w, e, nt: (0, w)),
                pl.BlockSpec((n_e, W), lambda w, e, nt: (0, w)),
                pl.BlockSpec((per_step, D, 2 * d_e), lambda w, e, nt: (e, 0, 0)),
                pl.BlockSpec((per_step, d_e, D), lambda w, e, nt: (e, 0, 0)),
                const(wsgu.shape), const((d_s, D)), const((1, D)), const((1, D)),
            ],
            out_specs=pl.BlockSpec((W, D), lambda w, e, nt: (w, 0)),
            scratch_shapes=[pltpu.VMEM((W, D), BF16)],
        ),
        out_shape=jax.ShapeDtypeStruct((N, D), F32),
        compiler_params=pltpu.CompilerParams(
            dimension_semantics=("arbitrary", "arbitrary"), vmem_limit_bytes=VMEM_LIMIT),
        name="moe",
    )(ntrip, x1, rank, gates, wgu, wd, wsgu, ws_down.astype(BF16),
      ln2_g.reshape(1, D), ln2_b.reshape(1, D))


def kernel(x, positions, w_in, lambda_q1, lambda_k1, lambda_q2, lambda_k2, subln_g, conv_w, conv_b, conv_ln_g, conv_ln_b, w_out, ln1_g, ln1_b, w_router, router_bias, w_exp_gate, w_exp_up, w_exp_down, w_sh_gate, w_sh_up, w_sh_down, ln2_g, ln2_b):
    B, S, D = x.shape
    d_attn = N_HEADS * HEAD_DIM
    d_conv = conv_w.shape[-1]
    for l in range(DEPTH):
        lambda_init = 0.8 - 0.6 * math.exp(-0.3 * l)
        qT, k, vT, h = _proj(x, positions, w_in[l].astype(BF16), d_attn, d_conv)
        attn, wgu, wd = _attention(qT, k, vT, lambda_q1[l], lambda_k1[l], lambda_q2[l], lambda_k2[l], subln_g[l],
                                   lambda_init, w_exp_gate[l], w_exp_up[l], w_exp_down[l])
        x1, rank, gates, counts = _mix(h, attn, x, conv_w[l], conv_b[l], conv_ln_g[l], conv_ln_b[l],
                                       w_out[l].astype(BF16), ln1_g[l], ln1_b[l], w_router[l], router_bias[l])
        out = _moe(x1, rank, gates, counts, wgu, wd, w_sh_gate[l], w_sh_up[l], w_sh_down[l], ln2_g[l], ln2_b[l])
        x = out.reshape(B, S, D)
    return x
```

```python
import functools
import math

import jax
import jax.numpy as jnp
from jax import lax
from jax.experimental import pallas as pl
from jax.experimental.pallas import tpu as pltpu

F32 = jnp.float32
BF16 = jnp.bfloat16
I32 = jnp.int32

N_HEADS = 4
HEAD_DIM = 128
SUB_DIM = 64
ROPE_DIM = 16
ROPE_THETA = 500000.0
CHUNK = 64
CONV_WIDTH = 31
N_EXPERTS = 64
TOP_K = 8
N_GROUPS = 8
GROUP_SIZE = N_EXPERTS // N_GROUPS
TOPK_GROUPS = 4
ROUTED_SCALE = 2.5
LN_EPS = 1e-5
DEPTH = 1
ALPHA = (2.0 * DEPTH) ** 0.25
LOG2_E = math.log2(math.e)

LANES = 128
SUBLANES = 8
VMEM_LIMIT = 56 * 1024 * 1024

PROJ_ROWS = 512
ATTN_BLOCK = 256
ATTN_SEQS_PER_STEP = 2
MIX_ROWS = 256
CONV_HALO = 32
CONV_ROW_CHUNK = 64
MOE_WINDOW = 2048
MOE_SPAN = 256
EXPERT_ROW_TIERS = (48, 56, 64)
MOE_EXPERTS_PER_STEP = 4
MOE_EXPERTS_PER_TRIP = 4


def _dot(a, b):
    return jnp.dot(a, b, preferred_element_type=F32)


def _layer_norm(z, g, b):
    mu = jnp.mean(z, axis=-1, keepdims=True)
    zc = z - mu
    var = jnp.mean(zc * zc, axis=-1, keepdims=True)
    return zc * lax.rsqrt(var + LN_EPS) * g + b


def _proj_kernel(x_ref, pos_ref, invf_ref, w_ref, qT_ref, k_ref, vT_ref, h_ref, *, d_attn, d_conv, blk):
    xb = x_ref[...].astype(BF16)
    tm = xb.shape[0]
    ang = pos_ref[...].astype(F32) * invf_ref[...]
    cos = jnp.cos(ang)
    sin = jnp.sin(ang)
    d = lax.broadcasted_iota(I32, (1, LANES), 1) & (SUB_DIM - 1)
    half = ROPE_DIM // 2
    c_mul = jnp.where(d < ROPE_DIM, cos, 1.0)
    s_lo = jnp.where(d < half, -sin, 0.0)
    s_hi = jnp.where((d >= half) & (d < ROPE_DIM), sin, 0.0)

    def rope(t):
        outs = []
        for c in range(t.shape[1] // LANES):
            ts = t[:, c * LANES:(c + 1) * LANES]
            outs.append(ts * c_mul + pltpu.roll(ts, LANES - half, 1) * s_lo + pltpu.roll(ts, half, 1) * s_hi)
        return jnp.concatenate(outs, axis=1)

    q = rope(_dot(xb, w_ref[:, 0:d_attn])) * (SUB_DIM ** -0.5 * LOG2_E)
    for c in range(tm // blk):
        qT_ref[0, c] = q[c * blk:(c + 1) * blk, :].T.astype(BF16)
    k = rope(_dot(xb, w_ref[:, d_attn:2 * d_attn]))
    k_ref[0] = k.astype(BF16)
    v = _dot(xb, w_ref[:, 2 * d_attn:3 * d_attn])
    for c in range(tm // blk):
        vT_ref[0, c] = v[c * blk:(c + 1) * blk, :].T.astype(BF16)
    a = _dot(xb, w_ref[:, 3 * d_attn:3 * d_attn + d_conv])
    gate = _dot(xb, w_ref[:, 3 * d_attn + d_conv:3 * d_attn + 2 * d_conv])
    h_ref[0] = a * jax.nn.sigmoid(gate)


def _proj(x, positions, w_in_bf16, d_attn, d_conv):
    B, S, D = x.shape
    tm, blk = PROJ_ROWS, ATTN_BLOCK
    nb = S // blk
    inv_freq = ROPE_THETA ** (-jnp.arange(0, ROPE_DIM, 2, dtype=F32) / ROPE_DIM)
    sub = jnp.concatenate([inv_freq, inv_freq, jnp.zeros((SUB_DIM - ROPE_DIM,), F32)])
    invf = jnp.tile(sub, LANES // SUB_DIM).reshape(1, LANES)
    d_in = w_in_bf16.shape[1]
    kern = functools.partial(_proj_kernel, d_attn=d_attn, d_conv=d_conv, blk=blk)
    return pl.pallas_call(
        kern,
        grid=(B, S // tm),
        in_specs=[
            pl.BlockSpec((None, tm, D), lambda b, i: (b, i, 0)),
            pl.BlockSpec((None, tm, 1), lambda b, i: (b, i, 0)),
            pl.BlockSpec((1, LANES), lambda b, i: (0, 0)),
            pl.BlockSpec((D, d_in), lambda b, i: (0, 0)),
        ],
        out_specs=[
            pl.BlockSpec((1, tm // blk, d_attn, blk), lambda b, i: (b, i, 0, 0)),
            pl.BlockSpec((1, tm, d_attn), lambda b, i: (b, i, 0)),
            pl.BlockSpec((1, tm // blk, d_attn, blk), lambda b, i: (b, i, 0, 0)),
            pl.BlockSpec((1, tm, d_conv), lambda b, i: (b, i, 0)),
        ],
        out_shape=[
            jax.ShapeDtypeStruct((B, nb, d_attn, blk), BF16),
            jax.ShapeDtypeStruct((B, S, d_attn), BF16),
            jax.ShapeDtypeStruct((B, nb, d_attn, blk), BF16),
            jax.ShapeDtypeStruct((B, S, d_conv), F32),
        ],
        compiler_params=pltpu.CompilerParams(
            dimension_semantics=("parallel", "parallel"), vmem_limit_bytes=VMEM_LIMIT),
        name="proj",
    )(x, positions.reshape(B, S, 1), invf, w_in_bf16)


def _attn_kernel(lq1_ref, lk1_ref, lq2_ref, lk2_ref, g_ref, qT_ref, k_ref, vT_ref, wg_ref, wu_ref, wd_ref,
                 o_ref, wgu_out, wd_out, q_both, s_a, s_b, p_a, p_b, acc_ref, m_ref, *, blk, lambda_init):
    i = pl.program_id(1)
    d_e = wg_ref.shape[-1]
    wgu_out[:, :, :d_e] = wg_ref[...].astype(wgu_out.dtype)
    wgu_out[:, :, d_e:] = wu_ref[...].astype(wgu_out.dtype)
    wd_out[...] = wd_ref[...].astype(wd_out.dtype)
    lam = (jnp.exp(jnp.sum(lq1_ref[...] * lk1_ref[...], axis=-1, keepdims=True))
           - jnp.exp(jnp.sum(lq2_ref[...] * lk2_ref[...], axis=-1, keepdims=True)) + lambda_init)
    row = lax.broadcasted_iota(I32, (HEAD_DIM, 1), 0)
    ones = jnp.ones((2 * SUBLANES, blk), BF16)
    chains = [(bb, h) for bb in range(qT_ref.shape[0]) for h in range(N_HEADS)]

    def head_cols(h):
        return slice(h * HEAD_DIM, (h + 1) * HEAD_DIM)

    for c, (bb, h) in enumerate(chains):
        qT = qT_ref[bb, 0, head_cols(h), :]
        zero = jnp.zeros_like(qT)
        q_both[c] = jnp.concatenate([jnp.where(row < SUB_DIM, qT, zero), jnp.where(row >= SUB_DIM, qT, zero)], axis=1)

    def scores_into(c, j, s_ref):
        bb, h = chains[c]
        kb = k_ref[bb, pl.ds(pl.multiple_of(j * blk, blk), blk), head_cols(h)]
        s_ref[c] = _dot(kb, q_both[c])

    def weighted_values(c, j, p):
        bb, h = chains[c]
        return _dot(jnp.concatenate([vT_ref[bb, j, head_cols(h), :], ones], axis=0), p)

    def probs(c, s):
        m = m_ref[c]
        m_new = jnp.maximum(m, jnp.max(s, axis=0, keepdims=True))
        m_ref[c] = m_new
        return jnp.exp2(m - m_new), jnp.exp2(s - m_new).astype(BF16)

    def step(j, s_cur, s_nxt, p_prev, p_cur):
        for c in range(len(chains)):
            scores_into(c, j + 1, s_nxt)
            pv_prev = weighted_values(c, jnp.maximum(j - 1, 0), p_prev[c])
            alpha, p = probs(c, s_cur[c])
            p_cur[c] = p
            acc_ref[c] = alpha * (acc_ref[c] + pv_prev)

    def finish(s_cur, p_prev):
        kc = lax.broadcasted_iota(I32, (blk, 1), 0) // CHUNK
        qc = (lax.broadcasted_iota(I32, (1, 2 * blk), 1) & (blk - 1)) // CHUNK
        for c, (bb, h) in enumerate(chains):
            pv_prev = weighted_values(c, jnp.maximum(i - 1, 0), p_prev[c])
            alpha, p = probs(c, jnp.where(kc <= qc, s_cur[c], -jnp.inf))
            acc = alpha * (acc_ref[c] + pv_prev) + weighted_values(c, i, p)
            o = acc[:HEAD_DIM] * (1.0 / acc[HEAD_DIM:HEAD_DIM + 1])
            o = o[:, :blk] - lam * o[:, blk:]
            o = o * lax.rsqrt(jnp.mean(o * o, axis=0, keepdims=True) + LN_EPS)
            o = o * g_ref[...] * (1.0 - lambda_init)
            o_ref[bb, :, head_cols(h)] = o.T.astype(o_ref.dtype)

    m_ref[...] = jnp.full(m_ref.shape, -jnp.inf, F32)
    acc_ref[...] = jnp.zeros(acc_ref.shape, F32)
    p_b[...] = jnp.zeros(p_b.shape, BF16)
    for c in range(len(chains)):
        scores_into(c, 0, s_a)

    def pair(jj, c):
        step(2 * jj, s_a, s_b, p_b, p_a)
        step(2 * jj + 1, s_b, s_a, p_a, p_b)
        return c

    lax.fori_loop(0, i // 2, pair, 0)

    @pl.when(i % 2 == 1)
    def _():
        step(i - 1, s_a, s_b, p_b, p_a)
        finish(s_b, p_a)

    @pl.when(i % 2 == 0)
    def _():
        finish(s_a, p_b)


def _attention(qT, k, vT, lq1, lk1, lq2, lk2, sub_g, lambda_init, w_gate, w_up, w_down):
    B, nb, d_attn, blk = qT.shape
    S = nb * blk
    n_e, D, d_e = w_gate.shape
    seqs = ATTN_SEQS_PER_STEP if B % ATTN_SEQS_PER_STEP == 0 else 1
    n_chains = seqs * N_HEADS
    per_step = n_e // (B // seqs * nb)
    assert per_step * (B // seqs) * nb == n_e
    kern = functools.partial(_attn_kernel, blk=blk, lambda_init=lambda_init)
    vec = pl.BlockSpec((1, SUB_DIM), lambda b, i: (0, 0))
    w_slice = lambda shape: pl.BlockSpec((per_step,) + shape, lambda b, i: (b * nb + i, 0, 0))
    return pl.pallas_call(
        kern,
        grid=(B // seqs, nb),
        in_specs=[
            vec, vec, vec, vec,
            pl.BlockSpec((HEAD_DIM, 1), lambda b, i: (0, 0)),
            pl.BlockSpec((seqs, 1, d_attn, blk), lambda b, i: (b, i, 0, 0)),
            pl.BlockSpec((seqs, S, d_attn), lambda b, i: (b, 0, 0), pipeline_mode=pl.Buffered(1)),
            pl.BlockSpec((seqs, nb, d_attn, blk), lambda b, i: (b, 0, 0, 0), pipeline_mode=pl.Buffered(1)),
            w_slice((D, d_e)), w_slice((D, d_e)), w_slice((d_e, D)),
        ],
        out_specs=[
            pl.BlockSpec((seqs, blk, d_attn), lambda b, i: (b, i, 0)),
            w_slice((D, 2 * d_e)), w_slice((d_e, D)),
        ],
        out_shape=[
            jax.ShapeDtypeStruct((B, S, d_attn), BF16),
            jax.ShapeDtypeStruct((n_e, D, 2 * d_e), BF16),
            jax.ShapeDtypeStruct((n_e, d_e, D), BF16),
        ],
        scratch_shapes=[
            pltpu.VMEM((n_chains, HEAD_DIM, 2 * blk), BF16),
            pltpu.VMEM((n_chains, blk, 2 * blk), F32), pltpu.VMEM((n_chains, blk, 2 * blk), F32),
            pltpu.VMEM((n_chains, blk, 2 * blk), BF16), pltpu.VMEM((n_chains, blk, 2 * blk), BF16),
            pltpu.VMEM((n_chains, HEAD_DIM + 2 * SUBLANES, 2 * blk), F32),
            pltpu.VMEM((n_chains, 1, 2 * blk), F32),
        ],
        compiler_params=pltpu.CompilerParams(
            dimension_semantics=("parallel", "arbitrary"), vmem_limit_bytes=VMEM_LIMIT),
        name="attn",
    )(lq1.reshape(1, SUB_DIM), lk1.reshape(1, SUB_DIM), lq2.reshape(1, SUB_DIM), lk2.reshape(1, SUB_DIM),
      sub_g.reshape(HEAD_DIM, 1), qT, k, vT, w_gate, w_up, w_down)


def _route(scores, bias):
    n_e, tm = scores.shape
    neg = -jnp.inf
    biased = scores + bias
    g3 = biased.reshape(N_GROUPS, GROUP_SIZE, tm)
    io3 = lax.broadcasted_iota(I32, g3.shape, 1)
    m1 = jnp.max(g3, axis=1, keepdims=True)
    first = jnp.min(jnp.where(g3 == m1, io3, GROUP_SIZE), axis=1, keepdims=True)
    m2 = jnp.max(jnp.where(io3 == first, neg, g3), axis=1, keepdims=True)
    grp_score = (m1 + m2).reshape(N_GROUPS, tm)
    gio = lax.broadcasted_iota(I32, (N_GROUPS, tm), 0)
    grp_sel = jnp.zeros((N_GROUPS, tm), jnp.bool_)
    cur = grp_score
    for _ in range(TOPK_GROUPS):
        mx = jnp.max(cur, axis=0, keepdims=True)
        f = jnp.min(jnp.where(cur == mx, gio, N_GROUPS), axis=0, keepdims=True)
        hit = gio == f
        grp_sel = grp_sel | hit
        cur = jnp.where(hit, neg, cur)
    grp_f = jnp.where(grp_sel, 1.0, 0.0).astype(F32)
    exp_mask = jnp.broadcast_to(grp_f.reshape(N_GROUPS, 1, tm), (N_GROUPS, GROUP_SIZE, tm)).reshape(n_e, tm) > 0.5
    eio = lax.broadcasted_iota(I32, (n_e, tm), 0)
    cur = jnp.where(exp_mask, biased, neg)
    sel = jnp.zeros((n_e, tm), jnp.bool_)
    for _ in range(TOP_K):
        mx = jnp.max(cur, axis=0, keepdims=True)
        f = jnp.min(jnp.where(cur == mx, eio, n_e), axis=0, keepdims=True)
        hit = eio == f
        sel = sel | hit
        cur = jnp.where(hit, neg, cur)
    denom = jnp.sum(jnp.where(sel, scores, 0.0), axis=0, keepdims=True)
    gates = jnp.where(sel, scores / denom * ROUTED_SCALE, 0.0)
    return sel, gates


def _mix_kernel(h_ref, halo_ref, cw_ref, cb_ref, cg_ref, cbeta_ref, attn_ref, wo_ref, x_ref, g1_ref, b1_ref,
                wrh_ref, wrl_ref, rb_ref,
                x1_ref, rank_ref, gate_ref, cnt_ref,
                buf, shifted, cbuf, carry, *, tm, d_attn, tiles_per_span):
    tile = pl.program_id(0) * pl.num_programs(1) + pl.program_id(1)

    @pl.when(tile % tiles_per_span == 0)
    def _():
        carry[...] = jnp.zeros_like(carry)

    @pl.when(pl.program_id(1) == 0)
    def _():
        buf[0:CONV_HALO, :] = jnp.zeros((CONV_HALO, buf.shape[1]), F32)

    @pl.when(pl.program_id(1) != 0)
    def _():
        buf[0:CONV_HALO, :] = halo_ref[...]

    buf[CONV_HALO:, :] = h_ref[...]
    d_conv = buf.shape[1]
    for s in range(1, SUBLANES):
        shifted[s - 1] = buf[s:s + shifted.shape[1], :]
    off = CONV_HALO - (CONV_WIDTH - 1)
    for c in range(d_conv // LANES):
        cs = slice(c * LANES, (c + 1) * LANES)
        for r in range(tm // CONV_ROW_CHUNK):
            acc = jnp.zeros((CONV_ROW_CHUNK, LANES), F32)
            for j in range(CONV_WIDTH):
                s = (off + j) % SUBLANES
                start = r * CONV_ROW_CHUNK + off + j - s
                src = buf if s == 0 else shifted.at[s - 1]
                acc = acc + cw_ref[j:j + 1, cs] * src[start:start + CONV_ROW_CHUNK, cs]
            cbuf[r * CONV_ROW_CHUNK:(r + 1) * CONV_ROW_CHUNK, cs] = acc + cb_ref[:, cs]
    conv = _layer_norm(cbuf[...], cg_ref[...], cbeta_ref[...])
    conv = conv * jax.nn.sigmoid(conv)

    mix = _dot(attn_ref[...], wo_ref[0:d_attn, :]) + _dot(conv.astype(BF16), wo_ref[d_attn:, :])
    x1 = _layer_norm(ALPHA * x_ref[...] + mix, g1_ref[...], b1_ref[...])
    x1_ref[...] = x1

    xh = x1.astype(BF16)
    xl = (x1 - xh.astype(F32)).astype(BF16)
    logits = _dot(xh, wrh_ref[...]) + _dot(xl, wrh_ref[...]) + _dot(xh, wrl_ref[...])
    scores = jax.nn.sigmoid(logits).T
    sel, gates = _route(scores, rb_ref[...])
    gate_ref[...] = gates

    t_row = lax.broadcasted_iota(I32, (tm, tm), 0)
    t_col = lax.broadcasted_iota(I32, (tm, tm), 1)
    before = jnp.where(t_row < t_col, 1.0, 0.0).astype(BF16)
    sel_f = jnp.where(sel, 1.0, 0.0).astype(F32)
    rank = _dot(sel_f.astype(BF16), before) + carry[...]
    rank_ref[...] = jnp.where(sel, rank, -1.0).astype(I32)
    carry[...] = carry[...] + jnp.sum(sel_f, axis=1, keepdims=True)
    cnt_ref[...] = carry[...].astype(I32)


def _mix(h, attn, x, conv_w, conv_b, cln_g, cln_b, w_out_bf16, ln1_g, ln1_b, w_router, router_bias):
    B, S, D = x.shape
    d_conv = h.shape[-1]
    d_attn = attn.shape[-1]
    tm = MIX_ROWS
    nt = S // tm
    N = B * S
    n_e = w_router.shape[-1]
    wr_hi = w_router.astype(BF16)
    wr_lo = (w_router - wr_hi.astype(F32)).astype(BF16)
    halo_blocks = tm // CONV_HALO
    row = lambda a: a.reshape(1, -1)
    const = lambda shape: pl.BlockSpec(shape, lambda b, i: (0,) * len(shape))
    tiles_per_span = MOE_SPAN // tm
    kern = functools.partial(_mix_kernel, tm=tm, d_attn=d_attn, tiles_per_span=tiles_per_span)
    return pl.pallas_call(
        kern,
        grid=(B, nt),
        in_specs=[
            pl.BlockSpec((None, tm, d_conv), lambda b, i: (b, i, 0)),
            pl.BlockSpec((None, CONV_HALO, d_conv), lambda b, i: (b, jnp.maximum(i * halo_blocks - 1, 0), 0)),
            const(conv_w.shape), const((1, d_conv)), const((1, d_conv)), const((1, d_conv)),
            pl.BlockSpec((None, tm, d_attn), lambda b, i: (b, i, 0)),
            const(w_out_bf16.shape),
            pl.BlockSpec((None, tm, D), lambda b, i: (b, i, 0)),
            const((1, D)), const((1, D)),
            const(wr_hi.shape), const(wr_lo.shape), const((n_e, 1)),
        ],
        out_specs=[
            pl.BlockSpec((tm, D), lambda b, i: (b * nt + i, 0)),
            pl.BlockSpec((n_e, tm), lambda b, i: (0, b * nt + i)),
            pl.BlockSpec((n_e, tm), lambda b, i: (0, b * nt + i)),
            pl.BlockSpec((None, n_e, 1), lambda b, i: (b * nt + i, 0, 0)),
        ],
        out_shape=[
            jax.ShapeDtypeStruct((N, D), F32),
            jax.ShapeDtypeStruct((n_e, N), I32),
            jax.ShapeDtypeStruct((n_e, N), F32),
            jax.ShapeDtypeStruct((B * nt, n_e, 1), I32),
        ],
        scratch_shapes=[
            pltpu.VMEM((tm + CONV_HALO, d_conv), F32),
            pltpu.VMEM((SUBLANES - 1, tm + CONV_HALO - SUBLANES, d_conv), F32),
            pltpu.VMEM((tm, d_conv), F32),
            pltpu.VMEM((n_e, 1), F32),
        ],
        compiler_params=pltpu.CompilerParams(
            dimension_semantics=("arbitrary", "arbitrary"), vmem_limit_bytes=VMEM_LIMIT),
        name="mix",
    )(h, h, conv_w, row(conv_b), row(cln_g), row(cln_b), attn, w_out_bf16, x, row(ln1_g), row(ln1_b),
      wr_hi, wr_lo, router_bias.reshape(n_e, 1))


def _moe_kernel(ntrip_ref, x1_ref, rank_ref, gate_ref, wgu_ref, wd_ref, wsgu_ref, wsd_ref,
                g2_ref, b2_ref, o_ref, xb_ref, *, tiers, d_e, d_s, per_step, per_trip, span):
    w = pl.program_id(0)
    step = pl.program_id(1)

    @pl.when(step == 0)
    def _():
        x1 = x1_ref[...]
        xb = x1.astype(BF16)
        xb_ref[...] = xb
        gu = _dot(xb, wsgu_ref[...])
        g = gu[:, :d_s]
        shared = _dot((g * jax.nn.sigmoid(g) * gu[:, d_s:]).astype(BF16), wsd_ref[...])
        o_ref[...] = ALPHA * x1 + shared

    n_spans = xb_ref.shape[0] // span

    def trip(c, carry, first, rows):
        onehots, gathered = [], []
        slot_gates = [[None] * n_spans for _ in range(per_trip)]
        for s in range(n_spans):
            lo, hi = s * span, (s + 1) * span
            hits = []
            for k in range(per_trip):
                e = step * per_step + first + k
                rank = rank_ref[pl.ds(e, 1), lo:hi]
                hit = (lax.broadcasted_iota(I32, (rows, 1), 0) + c * rows) == rank
                hits.append(hit)
                slot_gates[k][s] = jnp.sum(jnp.where(hit, gate_ref[pl.ds(e, 1), lo:hi], 0.0), axis=1, keepdims=True)
            onehot = jnp.where(jnp.concatenate(hits, axis=0), 1.0, 0.0).astype(BF16)
            onehots.append(onehot)
            gathered.append(_dot(onehot, xb_ref[lo:hi, :]))
        gated = []
        for k in range(per_trip):
            sl = slice(k * rows, (k + 1) * rows)
            xs = jnp.concatenate([gathered[s][sl] for s in range(n_spans)], axis=0).astype(BF16)
            gu = _dot(xs, wgu_ref[first + k])
            g = gu[:, :d_e]
            y = _dot((g * jax.nn.sigmoid(g) * gu[:, d_e:]).astype(BF16), wd_ref[first + k])
            gated.append(y * jnp.concatenate(slot_gates[k], axis=0))
        for s in range(n_spans):
            sl = slice(s * rows, (s + 1) * rows)
            yg = jnp.concatenate([gated[k][sl] for k in range(per_trip)], axis=0).astype(BF16)
            o_ref[s * span:(s + 1) * span, :] += lax.dot_general(
                onehots[s], yg, (((0,), (0,)), ((), ())), preferred_element_type=F32)
        return carry

    quads_per_step = per_step // per_trip

    def quad(t, carry):
        q = (w * pl.num_programs(1) + step) * quads_per_step + t
        for tier, rows in enumerate(tiers):
            lax.fori_loop(0, ntrip_ref[q * len(tiers) + tier],
                          functools.partial(trip, first=t * per_trip, rows=rows), 0)
        return carry

    lax.fori_loop(0, quads_per_step, quad, 0)

    @pl.when(step == pl.num_programs(1) - 1)
    def _():
        o_ref[...] = _layer_norm(o_ref[...], g2_ref[...], b2_ref[...])


def _moe(x1, rank, gates, counts, wgu, wd, ws_gate, ws_up, ws_down, ln2_g, ln2_b):
    N, D = x1.shape
    n_e, d_e, _ = wd.shape
    d_s = ws_down.shape[0]
    W = MOE_WINDOW
    n_win = N // W
    tiers = EXPERT_ROW_TIERS
    span = MOE_SPAN
    per_step, per_trip = MOE_EXPERTS_PER_STEP, MOE_EXPERTS_PER_TRIP
    n_steps = n_e // per_step
    assert len(tiers) >= 2 and all(r % SUBLANES == 0 and (per_trip * r) % (2 * SUBLANES) == 0 for r in tiers)
    assert tiers == tuple(sorted(tiers)) and n_e % per_step == 0 and per_step % per_trip == 0
    per_span = counts.reshape(N // span, span // MIX_ROWS, n_e)[:, -1]
    fullest = jnp.max(per_span.reshape(n_win, W // span, n_e // per_trip, per_trip), axis=(1, 3)).reshape(-1)
    lower = jnp.asarray((0,) + tiers[:-1], I32)
    fits = (fullest[:, None] > lower[None, :]) & (fullest[:, None] <= jnp.asarray(tiers, I32)[None, :])
    ntrip = fits.astype(I32).at[:, -1].set(
        jnp.where(fullest > tiers[-2], (fullest + tiers[-1] - 1) // tiers[-1], 0).astype(I32)).reshape(-1)
    wsgu = jnp.concatenate([ws_gate, ws_up], axis=-1).astype(BF16)
    kern = functools.partial(_moe_kernel, tiers=tiers, d_e=d_e, d_s=d_s, per_step=per_step,
                             per_trip=per_trip, span=span)
    const = lambda shape: pl.BlockSpec(shape, lambda w, e, nt: (0,) * len(shape))
    return pl.pallas_call(
        kern,
        grid_spec=pltpu.PrefetchScalarGridSpec(
            num_scalar_prefetch=1,
            grid=(n_win, n_steps),
            in_specs=[
                pl.BlockSpec((W, D), lambda w, e, nt: (w, 0), pipeline_mode=pl.Buffered(1)),
                pl.BlockSpec((n_e, W), lambda w, e, nt: (0, w)),
                pl.BlockSpec((n_e, W), lambda w, e, nt: (0, w)),
                pl.BlockSpec((per_step, D, 2 * d_e), lambda w, e, nt: (e, 0, 0)),
                pl.BlockSpec((per_step, d_e, D), lambda w, e, nt: (e, 0, 0)),
                const(wsgu.shape), const((d_s, D)), const((1, D)), const((1, D)),
            ],
            out_specs=pl.BlockSpec((W, D), lambda w, e, nt: (w, 0)),
            scratch_shapes=[pltpu.VMEM((W, D), BF16)],
        ),
        out_shape=jax.ShapeDtypeStruct((N, D), F32),
        compiler_params=pltpu.CompilerParams(
            dimension_semantics=("arbitrary", "arbitrary"), vmem_limit_bytes=VMEM_LIMIT),
        name="moe",
    )(ntrip, x1, rank, gates, wgu, wd, wsgu, ws_down.astype(BF16),
      ln2_g.reshape(1, D), ln2_b.reshape(1, D))


def kernel(x, positions, w_in, lambda_q1, lambda_k1, lambda_q2, lambda_k2, subln_g, conv_w, conv_b, conv_ln_g, conv_ln_b, w_out, ln1_g, ln1_b, w_router, router_bias, w_exp_gate, w_exp_up, w_exp_down, w_sh_gate, w_sh_up, w_sh_down, ln2_g, ln2_b):
    B, S, D = x.shape
    d_attn = N_HEADS * HEAD_DIM
    d_conv = conv_w.shape[-1]
    for l in range(DEPTH):
        lambda_init = 0.8 - 0.6 * math.exp(-0.3 * l)
        qT, k, vT, h = _proj(x, positions, w_in[l].astype(BF16), d_attn, d_conv)
        attn, wgu, wd = _attention(qT, k, vT, lambda_q1[l], lambda_k1[l], lambda_q2[l], lambda_k2[l], subln_g[l],
                                   lambda_init, w_exp_gate[l], w_exp_up[l], w_exp_down[l])
        x1, rank, gates, counts = _mix(h, attn, x, conv_w[l], conv_b[l], conv_ln_g[l], conv_ln_b[l],
                                       w_out[l].astype(BF16), ln1_g[l], ln1_b[l], w_router[l], router_bias[l])
        out = _moe(x1, rank, gates, counts, wgu, wd, w_sh_gate[l], w_sh_up[l], w_sh_down[l], ln2_g[l], ln2_b[l])
        x = out.reshape(B, S, D)
    return x
```

```python
import functools
import math

import jax
import jax.numpy as jnp
from jax import lax
from jax.experimental import pallas as pl
from jax.experimental.pallas import tpu as pltpu

F32 = jnp.float32
BF16 = jnp.bfloat16
I32 = jnp.int32

N_HEADS = 4
HEAD_DIM = 128
SUB_DIM = 64
ROPE_DIM = 16
ROPE_THETA = 500000.0
CHUNK = 64
CONV_WIDTH = 31
N_EXPERTS = 64
TOP_K = 8
N_GROUPS = 8
GROUP_SIZE = N_EXPERTS // N_GROUPS
TOPK_GROUPS = 4
ROUTED_SCALE = 2.5
LN_EPS = 1e-5
DEPTH = 1
ALPHA = (2.0 * DEPTH) ** 0.25
LOG2_E = math.log2(math.e)

LANES = 128
SUBLANES = 8
VMEM_LIMIT = 56 * 1024 * 1024

PROJ_ROWS = 512
ATTN_BLOCK = 256
ATTN_SEQS_PER_STEP = 2
MIX_ROWS = 256
CONV_HALO = 32
CONV_ROW_CHUNK = 64
MOE_WINDOW = 2048
MOE_SPAN = 256
EXPERT_ROW_TIERS = (40, 48, 64)
MOE_EXPERTS_PER_TRIP = 4


def _dot(a, b):
    return jnp.dot(a, b, preferred_element_type=F32)


def _layer_norm(z, g, b):
    mu = jnp.mean(z, axis=-1, keepdims=True)
    zc = z - mu
    var = jnp.mean(zc * zc, axis=-1, keepdims=True)
    return zc * lax.rsqrt(var + LN_EPS) * g + b


def _proj_kernel(x_ref, pos_ref, invf_ref, w_ref, qT_ref, k_ref, vT_ref, h_ref, *, d_attn, d_conv, blk):
    xb = x_ref[...].astype(BF16)
    tm = xb.shape[0]
    ang = pos_ref[...].astype(F32) * invf_ref[...]
    cos = jnp.cos(ang)
    sin = jnp.sin(ang)
    d = lax.broadcasted_iota(I32, (1, LANES), 1) & (SUB_DIM - 1)
    half = ROPE_DIM // 2
    c_mul = jnp.where(d < ROPE_DIM, cos, 1.0)
    s_lo = jnp.where(d < half, -sin, 0.0)
    s_hi = jnp.where((d >= half) & (d < ROPE_DIM), sin, 0.0)

    def rope(t):
        outs = []
        for c in range(t.shape[1] // LANES):
            ts = t[:, c * LANES:(c + 1) * LANES]
            outs.append(ts * c_mul + pltpu.roll(ts, LANES - half, 1) * s_lo + pltpu.roll(ts, half, 1) * s_hi)
        return jnp.concatenate(outs, axis=1)

    q = rope(_dot(xb, w_ref[:, 0:d_attn])) * (SUB_DIM ** -0.5 * LOG2_E)
    for c in range(tm // blk):
        qT_ref[0, c] = q[c * blk:(c + 1) * blk, :].T.astype(BF16)
    k = rope(_dot(xb, w_ref[:, d_attn:2 * d_attn]))
    k_ref[0] = k.astype(BF16)
    v = _dot(xb, w_ref[:, 2 * d_attn:3 * d_attn])
    for c in range(tm // blk):
        vT_ref[0, c] = v[c * blk:(c + 1) * blk, :].T.astype(BF16)
    a = _dot(xb, w_ref[:, 3 * d_attn:3 * d_attn + d_conv])
    gate = _dot(xb, w_ref[:, 3 * d_attn + d_conv:3 * d_attn + 2 * d_conv])
    h_ref[0] = a * jax.nn.sigmoid(gate)


def _proj(x, positions, w_in_bf16, d_attn, d_conv):
    B, S, D = x.shape
    tm, blk = PROJ_ROWS, ATTN_BLOCK
    nb = S // blk
    inv_freq = ROPE_THETA ** (-jnp.arange(0, ROPE_DIM, 2, dtype=F32) / ROPE_DIM)
    sub = jnp.concatenate([inv_freq, inv_freq, jnp.zeros((SUB_DIM - ROPE_DIM,), F32)])
    invf = jnp.tile(sub, LANES // SUB_DIM).reshape(1, LANES)
    d_in = w_in_bf16.shape[1]
    kern = functools.partial(_proj_kernel, d_attn=d_attn, d_conv=d_conv, blk=blk)
    return pl.pallas_call(
        kern,
        grid=(B, S // tm),
        in_specs=[
            pl.BlockSpec((None, tm, D), lambda b, i: (b, i, 0)),
            pl.BlockSpec((None, tm, 1), lambda b, i: (b, i, 0)),
            pl.BlockSpec((1, LANES), lambda b, i: (0, 0)),
            pl.BlockSpec((D, d_in), lambda b, i: (0, 0)),
        ],
        out_specs=[
            pl.BlockSpec((1, tm // blk, d_attn, blk), lambda b, i: (b, i, 0, 0)),
            pl.BlockSpec((1, tm, d_attn), lambda b, i: (b, i, 0)),
            pl.BlockSpec((1, tm // blk, d_attn, blk), lambda b, i: (b, i, 0, 0)),
            pl.BlockSpec((1, tm, d_conv), lambda b, i: (b, i, 0)),
        ],
        out_shape=[
            jax.ShapeDtypeStruct((B, nb, d_attn, blk), BF16),
            jax.ShapeDtypeStruct((B, S, d_attn), BF16),
            jax.ShapeDtypeStruct((B, nb, d_attn, blk), BF16),
            jax.ShapeDtypeStruct((B, S, d_conv), F32),
        ],
        compiler_params=pltpu.CompilerParams(
            dimension_semantics=("parallel", "parallel"), vmem_limit_bytes=VMEM_LIMIT),
        name="proj",
    )(x, positions.reshape(B, S, 1), invf, w_in_bf16)


def _attn_kernel(lq1_ref, lk1_ref, lq2_ref, lk2_ref, g_ref, qT_ref, k_ref, vT_ref, wg_ref, wu_ref, wd_ref,
                 o_ref, wgu_out, wd_out, q_both, s_a, s_b, p_a, p_b, acc_ref, m_ref, *, blk, lambda_init):
    i = pl.program_id(1)
    d_e = wg_ref.shape[-1]
    wgu_out[:, :, :d_e] = wg_ref[...].astype(wgu_out.dtype)
    wgu_out[:, :, d_e:] = wu_ref[...].astype(wgu_out.dtype)
    wd_out[...] = wd_ref[...].astype(wd_out.dtype)
    lam = (jnp.exp(jnp.sum(lq1_ref[...] * lk1_ref[...], axis=-1, keepdims=True))
           - jnp.exp(jnp.sum(lq2_ref[...] * lk2_ref[...], axis=-1, keepdims=True)) + lambda_init)
    row = lax.broadcasted_iota(I32, (HEAD_DIM, 1), 0)
    ones = jnp.ones((2 * SUBLANES, blk), BF16)
    chains = [(bb, h) for bb in range(qT_ref.shape[0]) for h in range(N_HEADS)]

    def head_cols(h):
        return slice(h * HEAD_DIM, (h + 1) * HEAD_DIM)

    for c, (bb, h) in enumerate(chains):
        qT = qT_ref[bb, 0, head_cols(h), :]
        zero = jnp.zeros_like(qT)
        q_both[c] = jnp.concatenate([jnp.where(row < SUB_DIM, qT, zero), jnp.where(row >= SUB_DIM, qT, zero)], axis=1)

    def scores_into(c, j, s_ref):
        bb, h = chains[c]
        kb = k_ref[bb, pl.ds(pl.multiple_of(j * blk, blk), blk), head_cols(h)]
        s_ref[c] = _dot(kb, q_both[c])

    def weighted_values(c, j, p):
        bb, h = chains[c]
        return _dot(jnp.concatenate([vT_ref[bb, j, head_cols(h), :], ones], axis=0), p)

    def probs(c, s):
        m = m_ref[c]
        m_new = jnp.maximum(m, jnp.max(s, axis=0, keepdims=True))
        m_ref[c] = m_new
        return jnp.exp2(m - m_new), jnp.exp2(s - m_new).astype(BF16)

    def step(j, s_cur, s_nxt, p_prev, p_cur):
        for c in range(len(chains)):
            scores_into(c, j + 1, s_nxt)
            pv_prev = weighted_values(c, jnp.maximum(j - 1, 0), p_prev[c])
            alpha, p = probs(c, s_cur[c])
            p_cur[c] = p
            acc_ref[c] = alpha * (acc_ref[c] + pv_prev)

    def finish(s_cur, p_prev):
        kc = lax.broadcasted_iota(I32, (blk, 1), 0) // CHUNK
        qc = (lax.broadcasted_iota(I32, (1, 2 * blk), 1) & (blk - 1)) // CHUNK
        for c, (bb, h) in enumerate(chains):
            pv_prev = weighted_values(c, jnp.maximum(i - 1, 0), p_prev[c])
            alpha, p = probs(c, jnp.where(kc <= qc, s_cur[c], -jnp.inf))
            acc = alpha * (acc_ref[c] + pv_prev) + weighted_values(c, i, p)
            o = acc[:HEAD_DIM] * (1.0 / acc[HEAD_DIM:HEAD_DIM + 1])
            o = o[:, :blk] - lam * o[:, blk:]
            o = o * lax.rsqrt(jnp.mean(o * o, axis=0, keepdims=True) + LN_EPS)
            o = o * g_ref[...] * (1.0 - lambda_init)
            o_ref[bb, :, head_cols(h)] = o.T.astype(o_ref.dtype)

    m_ref[...] = jnp.full(m_ref.shape, -jnp.inf, F32)
    acc_ref[...] = jnp.zeros(acc_ref.shape, F32)
    p_b[...] = jnp.zeros(p_b.shape, BF16)
    for c in range(len(chains)):
        scores_into(c, 0, s_a)

    def pair(jj, c):
        step(2 * jj, s_a, s_b, p_b, p_a)
        step(2 * jj + 1, s_b, s_a, p_a, p_b)
        return c

    lax.fori_loop(0, i // 2, pair, 0)

    @pl.when(i % 2 == 1)
    def _():
        step(i - 1, s_a, s_b, p_b, p_a)
        finish(s_b, p_a)

    @pl.when(i % 2 == 0)
    def _():
        finish(s_a, p_b)


def _attention(qT, k, vT, lq1, lk1, lq2, lk2, sub_g, lambda_init, w_gate, w_up, w_down):
    B, nb, d_attn, blk = qT.shape
    S = nb * blk
    n_e, D, d_e = w_gate.shape
    seqs = ATTN_SEQS_PER_STEP if B % ATTN_SEQS_PER_STEP == 0 else 1
    n_chains = seqs * N_HEADS
    per_step = n_e // (B // seqs * nb)
    assert per_step * (B // seqs) * nb == n_e
    kern = functools.partial(_attn_kernel, blk=blk, lambda_init=lambda_init)
    vec = pl.BlockSpec((1, SUB_DIM), lambda b, i: (0, 0))
    w_slice = lambda shape: pl.BlockSpec((per_step,) + shape, lambda b, i: (b * nb + i, 0, 0))
    return pl.pallas_call(
        kern,
        grid=(B // seqs, nb),
        in_specs=[
            vec, vec, vec, vec,
            pl.BlockSpec((HEAD_DIM, 1), lambda b, i: (0, 0)),
            pl.BlockSpec((seqs, 1, d_attn, blk), lambda b, i: (b, i, 0, 0)),
            pl.BlockSpec((seqs, S, d_attn), lambda b, i: (b, 0, 0), pipeline_mode=pl.Buffered(1)),
            pl.BlockSpec((seqs, nb, d_attn, blk), lambda b, i: (b, 0, 0, 0), pipeline_mode=pl.Buffered(1)),
            w_slice((D, d_e)), w_slice((D, d_e)), w_slice((d_e, D)),
        ],
        out_specs=[
            pl.BlockSpec((seqs, blk, d_attn), lambda b, i: (b, i, 0)),
            w_slice((D, 2 * d_e)), w_slice((d_e, D)),
        ],
        out_shape=[
            jax.ShapeDtypeStruct((B, S, d_attn), BF16),
            jax.ShapeDtypeStruct((n_e, D, 2 * d_e), BF16),
            jax.ShapeDtypeStruct((n_e, d_e, D), BF16),
        ],
        scratch_shapes=[
            pltpu.VMEM((n_chains, HEAD_DIM, 2 * blk), BF16),
            pltpu.VMEM((n_chains, blk, 2 * blk), F32), pltpu.VMEM((n_chains, blk, 2 * blk), F32),
            pltpu.VMEM((n_chains, blk, 2 * blk), BF16), pltpu.VMEM((n_chains, blk, 2 * blk), BF16),
            pltpu.VMEM((n_chains, HEAD_DIM + 2 * SUBLANES, 2 * blk), F32),
            pltpu.VMEM((n_chains, 1, 2 * blk), F32),
        ],
        compiler_params=pltpu.CompilerParams(
            dimension_semantics=("parallel", "arbitrary"), vmem_limit_bytes=VMEM_LIMIT),
        name="attn",
    )(lq1.reshape(1, SUB_DIM), lk1.reshape(1, SUB_DIM), lq2.reshape(1, SUB_DIM), lk2.reshape(1, SUB_DIM),
      sub_g.reshape(HEAD_DIM, 1), qT, k, vT, w_gate, w_up, w_down)


def _route(scores, bias):
    n_e, tm = scores.shape
    neg = -jnp.inf
    biased = scores + bias
    g3 = biased.reshape(N_GROUPS, GROUP_SIZE, tm)
    io3 = lax.broadcasted_iota(I32, g3.shape, 1)
    m1 = jnp.max(g3, axis=1, keepdims=True)
    first = jnp.min(jnp.where(g3 == m1, io3, GROUP_SIZE), axis=1, keepdims=True)
    m2 = jnp.max(jnp.where(io3 == first, neg, g3), axis=1, keepdims=True)
    grp_score = (m1 + m2).reshape(N_GROUPS, tm)
    gio = lax.broadcasted_iota(I32, (N_GROUPS, tm), 0)
    grp_sel = jnp.zeros((N_GROUPS, tm), jnp.bool_)
    cur = grp_score
    for _ in range(TOPK_GROUPS):
        mx = jnp.max(cur, axis=0, keepdims=True)
        f = jnp.min(jnp.where(cur == mx, gio, N_GROUPS), axis=0, keepdims=True)
        hit = gio == f
        grp_sel = grp_sel | hit
        cur = jnp.where(hit, neg, cur)
    grp_f = jnp.where(grp_sel, 1.0, 0.0).astype(F32)
    exp_mask = jnp.broadcast_to(grp_f.reshape(N_GROUPS, 1, tm), (N_GROUPS, GROUP_SIZE, tm)).reshape(n_e, tm) > 0.5
    eio = lax.broadcasted_iota(I32, (n_e, tm), 0)
    cur = jnp.where(exp_mask, biased, neg)
    sel = jnp.zeros((n_e, tm), jnp.bool_)
    for _ in range(TOP_K):
        mx = jnp.max(cur, axis=0, keepdims=True)
        f = jnp.min(jnp.where(cur == mx, eio, n_e), axis=0, keepdims=True)
        hit = eio == f
        sel = sel | hit
        cur = jnp.where(hit, neg, cur)
    denom = jnp.sum(jnp.where(sel, scores, 0.0), axis=0, keepdims=True)
    gates = jnp.where(sel, scores / denom * ROUTED_SCALE, 0.0)
    return sel, gates


def _mix_kernel(h_ref, halo_ref, cw_ref, cb_ref, cg_ref, cbeta_ref, attn_ref, wo_ref, x_ref, g1_ref, b1_ref,
                wrh_ref, wrl_ref, rb_ref,
                x1_ref, rank_ref, gate_ref, cnt_ref,
                buf, shifted, cbuf, carry, *, tm, d_attn, tiles_per_span):
    tile = pl.program_id(0) * pl.num_programs(1) + pl.program_id(1)

    @pl.when(tile % tiles_per_span == 0)
    def _():
        carry[...] = jnp.zeros_like(carry)

    @pl.when(pl.program_id(1) == 0)
    def _():
        buf[0:CONV_HALO, :] = jnp.zeros((CONV_HALO, buf.shape[1]), F32)

    @pl.when(pl.program_id(1) != 0)
    def _():
        buf[0:CONV_HALO, :] = halo_ref[...]

    buf[CONV_HALO:, :] = h_ref[...]
    d_conv = buf.shape[1]
    for s in range(1, SUBLANES):
        shifted[s - 1] = buf[s:s + shifted.shape[1], :]
    off = CONV_HALO - (CONV_WIDTH - 1)
    for c in range(d_conv // LANES):
        cs = slice(c * LANES, (c + 1) * LANES)
        for r in range(tm // CONV_ROW_CHUNK):
            acc = jnp.zeros((CONV_ROW_CHUNK, LANES), F32)
            for j in range(CONV_WIDTH):
                s = (off + j) % SUBLANES
                start = r * CONV_ROW_CHUNK + off + j - s
                src = buf if s == 0 else shifted.at[s - 1]
                acc = acc + cw_ref[j:j + 1, cs] * src[start:start + CONV_ROW_CHUNK, cs]
            cbuf[r * CONV_ROW_CHUNK:(r + 1) * CONV_ROW_CHUNK, cs] = acc + cb_ref[:, cs]
    conv = _layer_norm(cbuf[...], cg_ref[...], cbeta_ref[...])
    conv = conv * jax.nn.sigmoid(conv)

    mix = _dot(attn_ref[...], wo_ref[0:d_attn, :]) + _dot(conv.astype(BF16), wo_ref[d_attn:, :])
    x1 = _layer_norm(ALPHA * x_ref[...] + mix, g1_ref[...], b1_ref[...])
    x1_ref[...] = x1

    xh = x1.astype(BF16)
    xl = (x1 - xh.astype(F32)).astype(BF16)
    logits = _dot(xh, wrh_ref[...]) + _dot(xl, wrh_ref[...]) + _dot(xh, wrl_ref[...])
    scores = jax.nn.sigmoid(logits).T
    sel, gates = _route(scores, rb_ref[...])
    gate_ref[...] = gates

    t_row = lax.broadcasted_iota(I32, (tm, tm), 0)
    t_col = lax.broadcasted_iota(I32, (tm, tm), 1)
    before = jnp.where(t_row < t_col, 1.0, 0.0).astype(BF16)
    sel_f = jnp.where(sel, 1.0, 0.0).astype(F32)
    rank = _dot(sel_f.astype(BF16), before) + carry[...]
    rank_ref[...] = jnp.where(sel, rank, -1.0).astype(I32)
    carry[...] = carry[...] + jnp.sum(sel_f, axis=1, keepdims=True)
    cnt_ref[...] = carry[...].astype(I32)


def _mix(h, attn, x, conv_w, conv_b, cln_g, cln_b, w_out_bf16, ln1_g, ln1_b, w_router, router_bias):
    B, S, D = x.shape
    d_conv = h.shape[-1]
    d_attn = attn.shape[-1]
    tm = MIX_ROWS
    nt = S // tm
    N = B * S
    n_e = w_router.shape[-1]
    wr_hi = w_router.astype(BF16)
    wr_lo = (w_router - wr_hi.astype(F32)).astype(BF16)
    halo_blocks = tm // CONV_HALO
    row = lambda a: a.reshape(1, -1)
    const = lambda shape: pl.BlockSpec(shape, lambda b, i: (0,) * len(shape))
    tiles_per_span = MOE_SPAN // tm
    kern = functools.partial(_mix_kernel, tm=tm, d_attn=d_attn, tiles_per_span=tiles_per_span)
    return pl.pallas_call(
        kern,
        grid=(B, nt),
        in_specs=[
            pl.BlockSpec((None, tm, d_conv), lambda b, i: (b, i, 0)),
            pl.BlockSpec((None, CONV_HALO, d_conv), lambda b, i: (b, jnp.maximum(i * halo_blocks - 1, 0), 0)),
            const(conv_w.shape), const((1, d_conv)), const((1, d_conv)), const((1, d_conv)),
            pl.BlockSpec((None, tm, d_attn), lambda b, i: (b, i, 0)),
            const(w_out_bf16.shape),
            pl.BlockSpec((None, tm, D), lambda b, i: (b, i, 0)),
            const((1, D)), const((1, D)),
            const(wr_hi.shape), const(wr_lo.shape), const((n_e, 1)),
        ],
        out_specs=[
            pl.BlockSpec((tm, D), lambda b, i: (b * nt + i, 0)),
            pl.BlockSpec((n_e, tm), lambda b, i: (0, b * nt + i)),
            pl.BlockSpec((n_e, tm), lambda b, i: (0, b * nt + i)),
            pl.BlockSpec((None, n_e, 1), lambda b, i: (b * nt + i, 0, 0)),
        ],
        out_shape=[
            jax.ShapeDtypeStruct((N, D), F32),
            jax.ShapeDtypeStruct((n_e, N), I32),
            jax.ShapeDtypeStruct((n_e, N), F32),
            jax.ShapeDtypeStruct((B * nt, n_e, 1), I32),
        ],
        scratch_shapes=[
            pltpu.VMEM((tm + CONV_HALO, d_conv), F32),
            pltpu.VMEM((SUBLANES - 1, tm + CONV_HALO - SUBLANES, d_conv), F32),
            pltpu.VMEM((tm, d_conv), F32),
            pltpu.VMEM((n_e, 1), F32),
        ],
        compiler_params=pltpu.CompilerParams(
            dimension_semantics=("arbitrary", "arbitrary"), vmem_limit_bytes=VMEM_LIMIT),
        name="mix",
    )(h, h, conv_w, row(conv_b), row(cln_g), row(cln_b), attn, w_out_bf16, x, row(ln1_g), row(ln1_b),
      wr_hi, wr_lo, router_bias.reshape(n_e, 1))


def _moe_kernel(ntrip_ref, order_ref, x1_ref, rank_ref, gate_ref, *refs, tiers, d_e, d_s, per_trip, span):
    wgu_refs, wd_refs = refs[:per_trip], refs[per_trip:2 * per_trip]
    wsgu_ref, wsd_ref, g2_ref, b2_ref, o_ref, xb_ref = refs[2 * per_trip:]
    w = pl.program_id(0)
    step = pl.program_id(1)
    q = w * pl.num_programs(1) + step

    @pl.when(step == 0)
    def _():
        x1 = x1_ref[...]
        xb = x1.astype(BF16)
        xb_ref[...] = xb
        gu = _dot(xb, wsgu_ref[...])
        g = gu[:, :d_s]
        shared = _dot((g * jax.nn.sigmoid(g) * gu[:, d_s:]).astype(BF16), wsd_ref[...])
        o_ref[...] = ALPHA * x1 + shared

    n_spans = xb_ref.shape[0] // span
    experts = [order_ref[q * per_trip + k] for k in range(per_trip)]

    def trip(c, carry, rows):
        onehots, gathered = [], []
        slot_gates = [[None] * n_spans for _ in range(per_trip)]
        for s in range(n_spans):
            lo, hi = s * span, (s + 1) * span
            hits = []
            for k, e in enumerate(experts):
                rank = rank_ref[pl.ds(e, 1), lo:hi]
                hit = (lax.broadcasted_iota(I32, (rows, 1), 0) + c * rows) == rank
                hits.append(hit)
                slot_gates[k][s] = jnp.sum(jnp.where(hit, gate_ref[pl.ds(e, 1), lo:hi], 0.0), axis=1, keepdims=True)
            onehot = jnp.where(jnp.concatenate(hits, axis=0), 1.0, 0.0).astype(BF16)
            onehots.append(onehot)
            gathered.append(_dot(onehot, xb_ref[lo:hi, :]))
        gated = []
        for k in range(per_trip):
            sl = slice(k * rows, (k + 1) * rows)
            xs = jnp.concatenate([gathered[s][sl] for s in range(n_spans)], axis=0).astype(BF16)
            gu = _dot(xs, wgu_refs[k][0])
            g = gu[:, :d_e]
            y = _dot((g * jax.nn.sigmoid(g) * gu[:, d_e:]).astype(BF16), wd_refs[k][0])
            gated.append(y * jnp.concatenate(slot_gates[k], axis=0))
        for s in range(n_spans):
            sl = slice(s * rows, (s + 1) * rows)
            yg = jnp.concatenate([gated[k][sl] for k in range(per_trip)], axis=0).astype(BF16)
            o_ref[s * span:(s + 1) * span, :] += lax.dot_general(
                onehots[s], yg, (((0,), (0,)), ((), ())), preferred_element_type=F32)
        return carry

    for tier, rows in enumerate(tiers):
        lax.fori_loop(0, ntrip_ref[q * len(tiers) + tier], functools.partial(trip, rows=rows), 0)

    @pl.when(step == pl.num_programs(1) - 1)
    def _():
        o_ref[...] = _layer_norm(o_ref[...], g2_ref[...], b2_ref[...])


def _moe(x1, rank, gates, counts, wgu, wd, ws_gate, ws_up, ws_down, ln2_g, ln2_b):
    N, D = x1.shape
    n_e, d_e, _ = wd.shape
    d_s = ws_down.shape[0]
    W = MOE_WINDOW
    n_win = N // W
    tiers = EXPERT_ROW_TIERS
    span = MOE_SPAN
    per_trip = MOE_EXPERTS_PER_TRIP
    n_steps = n_e // per_trip
    assert len(tiers) >= 2 and all(r % SUBLANES == 0 and (per_trip * r) % (2 * SUBLANES) == 0 for r in tiers)
    assert tiers == tuple(sorted(tiers)) and n_e % per_trip == 0
    per_span = counts.reshape(N // span, span // MIX_ROWS, n_e)[:, -1]
    load = jnp.max(per_span.reshape(n_win, W // span, n_e), axis=1)
    ids = jnp.arange(n_e, dtype=I32)
    before = ((load[:, None, :] < load[:, :, None])
              | ((load[:, None, :] == load[:, :, None]) & (ids[None, None, :] < ids[None, :, None])))
    place = jnp.sum(before, axis=-1).astype(I32)
    at_place = place[:, :, None] == ids[None, None, :]
    order = jnp.sum(jnp.where(at_place, ids[None, :, None], 0), axis=1).astype(I32)
    sorted_load = jnp.sum(jnp.where(at_place, load[:, :, None], 0), axis=1)
    fullest = jnp.max(sorted_load.reshape(n_win, n_steps, per_trip), axis=-1).reshape(-1)
    lower = jnp.asarray((0,) + tiers[:-1], I32)
    fits = (fullest[:, None] > lower[None, :]) & (fullest[:, None] <= jnp.asarray(tiers, I32)[None, :])
    ntrip = fits.astype(I32).at[:, -1].set(
        jnp.where(fullest > tiers[-2], (fullest + tiers[-1] - 1) // tiers[-1], 0).astype(I32)).reshape(-1)
    wsgu = jnp.concatenate([ws_gate, ws_up], axis=-1).astype(BF16)
    kern = functools.partial(_moe_kernel, tiers=tiers, d_e=d_e, d_s=d_s, per_trip=per_trip, span=span)
    const = lambda shape: pl.BlockSpec(shape, lambda w, s, nt, od: (0,) * len(shape))

    def expert_block(shape, k):
        return pl.BlockSpec((1,) + shape, lambda w, s, nt, od: (od[(w * n_steps + s) * per_trip + k], 0, 0))

    return pl.pallas_call(
        kern,
        grid_spec=pltpu.PrefetchScalarGridSpec(
            num_scalar_prefetch=2,
            grid=(n_win, n_steps),
            in_specs=[
                pl.BlockSpec((W, D), lambda w, s, nt, od: (w, 0), pipeline_mode=pl.Buffered(1)),
                pl.BlockSpec((n_e, W), lambda w, s, nt, od: (0, w)),
                pl.BlockSpec((n_e, W), lambda w, s, nt, od: (0, w)),
                *[expert_block((D, 2 * d_e), k) for k in range(per_trip)],
                *[expert_block((d_e, D), k) for k in range(per_trip)],
                const(wsgu.shape), const((d_s, D)), const((1, D)), const((1, D)),
            ],
            out_specs=pl.BlockSpec((W, D), lambda w, s, nt, od: (w, 0)),
            scratch_shapes=[pltpu.VMEM((W, D), BF16)],
        ),
        out_shape=jax.ShapeDtypeStruct((N, D), F32),
        compiler_params=pltpu.CompilerParams(
            dimension_semantics=("arbitrary", "arbitrary"), vmem_limit_bytes=VMEM_LIMIT),
        name="moe",
    )(ntrip, order.reshape(-1), x1, rank, gates, *([wgu] * per_trip), *([wd] * per_trip),
      wsgu, ws_down.astype(BF16), ln2_g.reshape(1, D), ln2_b.reshape(1, D))


def kernel(x, positions, w_in, lambda_q1, lambda_k1, lambda_q2, lambda_k2, subln_g, conv_w, conv_b, conv_ln_g, conv_ln_b, w_out, ln1_g, ln1_b, w_router, router_bias, w_exp_gate, w_exp_up, w_exp_down, w_sh_gate, w_sh_up, w_sh_down, ln2_g, ln2_b):
    B, S, D = x.shape
    d_attn = N_HEADS * HEAD_DIM
    d_conv = conv_w.shape[-1]
    for l in range(DEPTH):
        lambda_init = 0.8 - 0.6 * math.exp(-0.3 * l)
        qT, k, vT, h = _proj(x, positions, w_in[l].astype(BF16), d_attn, d_conv)
        attn, wgu, wd = _attention(qT, k, vT, lambda_q1[l], lambda_k1[l], lambda_q2[l], lambda_k2[l], subln_g[l],
                                   lambda_init, w_exp_gate[l], w_exp_up[l], w_exp_down[l])
        x1, rank, gates, counts = _mix(h, attn, x, conv_w[l], conv_b[l], conv_ln_g[l], conv_ln_b[l],
                                       w_out[l].astype(BF16), ln1_g[l], ln1_b[l], w_router[l], router_bias[l])
        out = _moe(x1, rank, gates, counts, wgu, wd, w_sh_gate[l], w_sh_up[l], w_sh_down[l], ln2_g[l], ln2_b[l])
        x = out.reshape(B, S, D)
    return x
```

```python
import functools
import math

import jax
import jax.numpy as jnp
from jax import lax
from jax.experimental import pallas as pl
from jax.experimental.pallas import tpu as pltpu

F32 = jnp.float32
BF16 = jnp.bfloat16
I32 = jnp.int32

N_HEADS = 4
HEAD_DIM = 128
SUB_DIM = 64
ROPE_DIM = 16
ROPE_THETA = 500000.0
CHUNK = 64
CONV_WIDTH = 31
N_EXPERTS = 64
TOP_K = 8
N_GROUPS = 8
GROUP_SIZE = N_EXPERTS // N_GROUPS
TOPK_GROUPS = 4
ROUTED_SCALE = 2.5
LN_EPS = 1e-5
DEPTH = 1
ALPHA = (2.0 * DEPTH) ** 0.25
LOG2_E = math.log2(math.e)

LANES = 128
SUBLANES = 8
VMEM_LIMIT = 56 * 1024 * 1024

PROJ_ROWS = 512
ATTN_BLOCK = 256
ATTN_SEQS_PER_STEP = 2
MIX_ROWS = 256
CONV_HALO = 32
CONV_ROW_CHUNK = 64
MOE_WINDOW = 2048
MOE_SPAN = 256
EXPERT_ROW_TIERS = (40, 48, 64)
MOE_EXPERTS_PER_TRIP = 4


def _dot(a, b):
    return jnp.dot(a, b, preferred_element_type=F32)


def _layer_norm(z, g, b):
    mu = jnp.mean(z, axis=-1, keepdims=True)
    zc = z - mu
    var = jnp.mean(zc * zc, axis=-1, keepdims=True)
    return zc * lax.rsqrt(var + LN_EPS) * g + b


def _proj_kernel(x_ref, pos_ref, invf_ref, w_ref, qT_ref, k_ref, vT_ref, h_ref, *, d_attn, d_conv, blk):
    xb = x_ref[...].astype(BF16)
    tm = xb.shape[0]
    pos_row = pos_ref[pl.ds(pl.program_id(0), 1), :].astype(F32)
    ang = jnp.broadcast_to(pos_row, (LANES, tm)).T * invf_ref[...]
    cos = jnp.cos(ang)
    sin = jnp.sin(ang)
    d = lax.broadcasted_iota(I32, (1, LANES), 1) & (SUB_DIM - 1)
    half = ROPE_DIM // 2
    c_mul = jnp.where(d < ROPE_DIM, cos, 1.0)
    s_lo = jnp.where(d < half, -sin, 0.0)
    s_hi = jnp.where((d >= half) & (d < ROPE_DIM), sin, 0.0)

    def rope(t):
        outs = []
        for c in range(t.shape[1] // LANES):
            ts = t[:, c * LANES:(c + 1) * LANES]
            outs.append(ts * c_mul + pltpu.roll(ts, LANES - half, 1) * s_lo + pltpu.roll(ts, half, 1) * s_hi)
        return jnp.concatenate(outs, axis=1)

    q = rope(_dot(xb, w_ref[:, 0:d_attn])) * (SUB_DIM ** -0.5 * LOG2_E)
    for c in range(tm // blk):
        qT_ref[0, c] = q[c * blk:(c + 1) * blk, :].T.astype(BF16)
    k = rope(_dot(xb, w_ref[:, d_attn:2 * d_attn]))
    k_ref[0] = k.astype(BF16)
    v = _dot(xb, w_ref[:, 2 * d_attn:3 * d_attn])
    for c in range(tm // blk):
        vT_ref[0, c] = v[c * blk:(c + 1) * blk, :].T.astype(BF16)
    a = _dot(xb, w_ref[:, 3 * d_attn:3 * d_attn + d_conv])
    gate = _dot(xb, w_ref[:, 3 * d_attn + d_conv:3 * d_attn + 2 * d_conv])
    h_ref[0] = a * jax.nn.sigmoid(gate)


def _proj(x, positions, w_in_bf16, d_attn, d_conv):
    B, S, D = x.shape
    tm, blk = PROJ_ROWS, ATTN_BLOCK
    nb = S // blk
    inv_freq = ROPE_THETA ** (-jnp.arange(0, ROPE_DIM, 2, dtype=F32) / ROPE_DIM)
    sub = jnp.concatenate([inv_freq, inv_freq, jnp.zeros((SUB_DIM - ROPE_DIM,), F32)])
    invf = jnp.tile(sub, LANES // SUB_DIM).reshape(1, LANES)
    d_in = w_in_bf16.shape[1]
    kern = functools.partial(_proj_kernel, d_attn=d_attn, d_conv=d_conv, blk=blk)
    return pl.pallas_call(
        kern,
        grid=(B, S // tm),
        in_specs=[
            pl.BlockSpec((None, tm, D), lambda b, i: (b, i, 0)),
            pl.BlockSpec((B, tm), lambda b, i: (0, i)),
            pl.BlockSpec((1, LANES), lambda b, i: (0, 0)),
            pl.BlockSpec((D, d_in), lambda b, i: (0, 0)),
        ],
        out_specs=[
            pl.BlockSpec((1, tm // blk, d_attn, blk), lambda b, i: (b, i, 0, 0)),
            pl.BlockSpec((1, tm, d_attn), lambda b, i: (b, i, 0)),
            pl.BlockSpec((1, tm // blk, d_attn, blk), lambda b, i: (b, i, 0, 0)),
            pl.BlockSpec((1, tm, d_conv), lambda b, i: (b, i, 0)),
        ],
        out_shape=[
            jax.ShapeDtypeStruct((B, nb, d_attn, blk), BF16),
            jax.ShapeDtypeStruct((B, S, d_attn), BF16),
            jax.ShapeDtypeStruct((B, nb, d_attn, blk), BF16),
            jax.ShapeDtypeStruct((B, S, d_conv), F32),
        ],
        compiler_params=pltpu.CompilerParams(
            dimension_semantics=("parallel", "parallel"), vmem_limit_bytes=VMEM_LIMIT),
        name="proj",
    )(x, positions, invf, w_in_bf16)


def _attn_kernel(lq1_ref, lk1_ref, lq2_ref, lk2_ref, g_ref, qT_ref, k_ref, vT_ref, wg_ref, wu_ref, wd_ref,
                 o_ref, wgu_out, wd_out, q_both, s_a, s_b, p_a, p_b, acc_ref, m_ref, *, blk, lambda_init):
    i = pl.program_id(1)
    d_e = wg_ref.shape[-1]
    wgu_out[:, :, :d_e] = wg_ref[...].astype(wgu_out.dtype)
    wgu_out[:, :, d_e:] = wu_ref[...].astype(wgu_out.dtype)
    wd_out[...] = wd_ref[...].astype(wd_out.dtype)
    lam = (jnp.exp(jnp.sum(lq1_ref[...] * lk1_ref[...], axis=-1, keepdims=True))
           - jnp.exp(jnp.sum(lq2_ref[...] * lk2_ref[...], axis=-1, keepdims=True)) + lambda_init)
    row = lax.broadcasted_iota(I32, (HEAD_DIM, 1), 0)
    ones = jnp.ones((2 * SUBLANES, blk), BF16)
    chains = [(bb, h) for bb in range(qT_ref.shape[0]) for h in range(N_HEADS)]

    def head_cols(h):
        return slice(h * HEAD_DIM, (h + 1) * HEAD_DIM)

    for c, (bb, h) in enumerate(chains):
        qT = qT_ref[bb, 0, head_cols(h), :]
        zero = jnp.zeros_like(qT)
        q_both[c] = jnp.concatenate([jnp.where(row < SUB_DIM, qT, zero), jnp.where(row >= SUB_DIM, qT, zero)], axis=1)

    def scores_into(c, j, s_ref):
        bb, h = chains[c]
        kb = k_ref[bb, pl.ds(pl.multiple_of(j * blk, blk), blk), head_cols(h)]
        s_ref[c] = _dot(kb, q_both[c])

    def weighted_values(c, j, p):
        bb, h = chains[c]
        return _dot(jnp.concatenate([vT_ref[bb, j, head_cols(h), :], ones], axis=0), p)

    def probs(c, s):
        m = m_ref[c]
        m_new = jnp.maximum(m, jnp.max(s, axis=0, keepdims=True))
        m_ref[c] = m_new
        return jnp.exp2(m - m_new), jnp.exp2(s - m_new).astype(BF16)

    def step(j, s_cur, s_nxt, p_prev, p_cur):
        for c in range(len(chains)):
            scores_into(c, j + 1, s_nxt)
            pv_prev = weighted_values(c, jnp.maximum(j - 1, 0), p_prev[c])
            alpha, p = probs(c, s_cur[c])
            p_cur[c] = p
            acc_ref[c] = alpha * (acc_ref[c] + pv_prev)

    def finish(s_cur, p_prev):
        kc = lax.broadcasted_iota(I32, (blk, 1), 0) // CHUNK
        qc = (lax.broadcasted_iota(I32, (1, 2 * blk), 1) & (blk - 1)) // CHUNK
        for c, (bb, h) in enumerate(chains):
            pv_prev = weighted_values(c, jnp.maximum(i - 1, 0), p_prev[c])
            alpha, p = probs(c, jnp.where(kc <= qc, s_cur[c], -jnp.inf))
            acc = alpha * (acc_ref[c] + pv_prev) + weighted_values(c, i, p)
            o = acc[:HEAD_DIM] * (1.0 / acc[HEAD_DIM:HEAD_DIM + 1])
            o = o[:, :blk] - lam * o[:, blk:]
            o = o * lax.rsqrt(jnp.mean(o * o, axis=0, keepdims=True) + LN_EPS)
            o_ref[bb, :, head_cols(h)] = (o.T * g_ref[...] * (1.0 - lambda_init)).astype(o_ref.dtype)

    m_ref[...] = jnp.full(m_ref.shape, -jnp.inf, F32)
    acc_ref[...] = jnp.zeros(acc_ref.shape, F32)
    p_b[...] = jnp.zeros(p_b.shape, BF16)
    for c in range(len(chains)):
        scores_into(c, 0, s_a)

    def pair(jj, c):
        step(2 * jj, s_a, s_b, p_b, p_a)
        step(2 * jj + 1, s_b, s_a, p_a, p_b)
        return c

    lax.fori_loop(0, i // 2, pair, 0)

    @pl.when(i % 2 == 1)
    def _():
        step(i - 1, s_a, s_b, p_b, p_a)
        finish(s_b, p_a)

    @pl.when(i % 2 == 0)
    def _():
        finish(s_a, p_b)


def _attention(qT, k, vT, lq1, lk1, lq2, lk2, sub_g, lambda_init, w_gate, w_up, w_down):
    B, nb, d_attn, blk = qT.shape
    S = nb * blk
    n_e, D, d_e = w_gate.shape
    seqs = ATTN_SEQS_PER_STEP if B % ATTN_SEQS_PER_STEP == 0 else 1
    n_chains = seqs * N_HEADS
    per_step = n_e // (B // seqs * nb)
    assert per_step * (B // seqs) * nb == n_e
    kern = functools.partial(_attn_kernel, blk=blk, lambda_init=lambda_init)
    vec = pl.BlockSpec((1, SUB_DIM), lambda b, i: (0, 0))
    w_slice = lambda shape: pl.BlockSpec((per_step,) + shape, lambda b, i: (b * nb + i, 0, 0))
    return pl.pallas_call(
        kern,
        grid=(B // seqs, nb),
        in_specs=[
            vec, vec, vec, vec,
            pl.BlockSpec((1, HEAD_DIM), lambda b, i: (0, 0)),
            pl.BlockSpec((seqs, 1, d_attn, blk), lambda b, i: (b, i, 0, 0)),
            pl.BlockSpec((seqs, S, d_attn), lambda b, i: (b, 0, 0), pipeline_mode=pl.Buffered(1)),
            pl.BlockSpec((seqs, nb, d_attn, blk), lambda b, i: (b, 0, 0, 0), pipeline_mode=pl.Buffered(1)),
            w_slice((D, d_e)), w_slice((D, d_e)), w_slice((d_e, D)),
        ],
        out_specs=[
            pl.BlockSpec((seqs, blk, d_attn), lambda b, i: (b, i, 0)),
            w_slice((D, 2 * d_e)), w_slice((d_e, D)),
        ],
        out_shape=[
            jax.ShapeDtypeStruct((B, S, d_attn), BF16),
            jax.ShapeDtypeStruct((n_e, D, 2 * d_e), BF16),
            jax.ShapeDtypeStruct((n_e, d_e, D), BF16),
        ],
        scratch_shapes=[
            pltpu.VMEM((n_chains, HEAD_DIM, 2 * blk), BF16),
            pltpu.VMEM((n_chains, blk, 2 * blk), F32), pltpu.VMEM((n_chains, blk, 2 * blk), F32),
            pltpu.VMEM((n_chains, blk, 2 * blk), BF16), pltpu.VMEM((n_chains, blk, 2 * blk), BF16),
            pltpu.VMEM((n_chains, HEAD_DIM + 2 * SUBLANES, 2 * blk), F32),
            pltpu.VMEM((n_chains, 1, 2 * blk), F32),
        ],
        compiler_params=pltpu.CompilerParams(
            dimension_semantics=("parallel", "arbitrary"), vmem_limit_bytes=VMEM_LIMIT),
        name="attn",
    )(lq1.reshape(1, SUB_DIM), lk1.reshape(1, SUB_DIM), lq2.reshape(1, SUB_DIM), lk2.reshape(1, SUB_DIM),
      sub_g.reshape(1, HEAD_DIM), qT, k, vT, w_gate, w_up, w_down)


def _route(scores, biased):
    n_e, tm = scores.shape
    neg = -jnp.inf
    g3 = biased.reshape(N_GROUPS, GROUP_SIZE, tm)
    io3 = lax.broadcasted_iota(I32, g3.shape, 1)
    m1 = jnp.max(g3, axis=1, keepdims=True)
    first = jnp.min(jnp.where(g3 == m1, io3, GROUP_SIZE), axis=1, keepdims=True)
    m2 = jnp.max(jnp.where(io3 == first, neg, g3), axis=1, keepdims=True)
    grp_score = (m1 + m2).reshape(N_GROUPS, tm)
    gio = lax.broadcasted_iota(I32, (N_GROUPS, tm), 0)
    grp_sel = jnp.zeros((N_GROUPS, tm), jnp.bool_)
    cur = grp_score
    for _ in range(TOPK_GROUPS):
        mx = jnp.max(cur, axis=0, keepdims=True)
        f = jnp.min(jnp.where(cur == mx, gio, N_GROUPS), axis=0, keepdims=True)
        hit = gio == f
        grp_sel = grp_sel | hit
        cur = jnp.where(hit, neg, cur)
    grp_f = jnp.where(grp_sel, 1.0, 0.0).astype(F32)
    exp_mask = jnp.broadcast_to(grp_f.reshape(N_GROUPS, 1, tm), (N_GROUPS, GROUP_SIZE, tm)).reshape(n_e, tm) > 0.5
    eio = lax.broadcasted_iota(I32, (n_e, tm), 0)
    cur = jnp.where(exp_mask, biased, neg)
    sel = jnp.zeros((n_e, tm), jnp.bool_)
    for _ in range(TOP_K):
        mx = jnp.max(cur, axis=0, keepdims=True)
        f = jnp.min(jnp.where(cur == mx, eio, n_e), axis=0, keepdims=True)
        hit = eio == f
        sel = sel | hit
        cur = jnp.where(hit, neg, cur)
    denom = jnp.sum(jnp.where(sel, scores, 0.0), axis=0, keepdims=True)
    gates = jnp.where(sel, scores / denom * ROUTED_SCALE, 0.0)
    return sel, gates


def _mix_kernel(h_ref, halo_ref, cw_ref, cb_ref, cg_ref, cbeta_ref, attn_ref, wo_ref, x_ref, g1_ref, b1_ref,
                wrh_ref, wrl_ref, rb_ref,
                x1_ref, rank_ref, gate_ref, cnt_ref,
                buf, shifted, cbuf, carry, *, tm, d_attn, tiles_per_span):
    tile = pl.program_id(0) * pl.num_programs(1) + pl.program_id(1)

    @pl.when(tile % tiles_per_span == 0)
    def _():
        carry[...] = jnp.zeros_like(carry)

    @pl.when(pl.program_id(1) == 0)
    def _():
        buf[0:CONV_HALO, :] = jnp.zeros((CONV_HALO, buf.shape[1]), F32)

    @pl.when(pl.program_id(1) != 0)
    def _():
        buf[0:CONV_HALO, :] = halo_ref[...]

    buf[CONV_HALO:, :] = h_ref[...]
    d_conv = buf.shape[1]
    for s in range(1, SUBLANES):
        shifted[s - 1] = buf[s:s + shifted.shape[1], :]
    off = CONV_HALO - (CONV_WIDTH - 1)
    for c in range(d_conv // LANES):
        cs = slice(c * LANES, (c + 1) * LANES)
        for r in range(tm // CONV_ROW_CHUNK):
            acc = jnp.zeros((CONV_ROW_CHUNK, LANES), F32)
            for j in range(CONV_WIDTH):
                s = (off + j) % SUBLANES
                start = r * CONV_ROW_CHUNK + off + j - s
                src = buf if s == 0 else shifted.at[s - 1]
                acc = acc + cw_ref[j:j + 1, cs] * src[start:start + CONV_ROW_CHUNK, cs]
            cbuf[r * CONV_ROW_CHUNK:(r + 1) * CONV_ROW_CHUNK, cs] = acc + cb_ref[:, cs]
    conv = _layer_norm(cbuf[...], cg_ref[...], cbeta_ref[...])
    conv = conv * jax.nn.sigmoid(conv)

    mix = _dot(attn_ref[...], wo_ref[0:d_attn, :]) + _dot(conv.astype(BF16), wo_ref[d_attn:, :])
    x1 = _layer_norm(ALPHA * x_ref[...] + mix, g1_ref[...], b1_ref[...])
    x1_ref[...] = x1

    xh = x1.astype(BF16)
    xl = (x1 - xh.astype(F32)).astype(BF16)
    logits = _dot(xh, wrh_ref[...]) + _dot(xl, wrh_ref[...]) + _dot(xh, wrl_ref[...])
    scores = jax.nn.sigmoid(logits)
    sel, gates = _route(scores.T, (scores + rb_ref[...]).T)
    gate_ref[...] = gates

    t_row = lax.broadcasted_iota(I32, (tm, tm), 0)
    t_col = lax.broadcasted_iota(I32, (tm, tm), 1)
    before = jnp.where(t_row < t_col, 1.0, 0.0).astype(BF16)
    sel_f = jnp.where(sel, 1.0, 0.0).astype(F32)
    rank = _dot(sel_f.astype(BF16), before) + carry[...]
    rank_ref[...] = jnp.where(sel, rank, -1.0).astype(I32)
    carry[...] = carry[...] + jnp.sum(sel_f, axis=1, keepdims=True)
    cnt_ref[...] = carry[...].astype(I32)


def _mix(h, attn, x, conv_w, conv_b, cln_g, cln_b, w_out_bf16, ln1_g, ln1_b, w_router, router_bias):
    B, S, D = x.shape
    d_conv = h.shape[-1]
    d_attn = attn.shape[-1]
    tm = MIX_ROWS
    nt = S // tm
    N = B * S
    n_e = w_router.shape[-1]
    wr_hi = w_router.astype(BF16)
    wr_lo = (w_router - wr_hi.astype(F32)).astype(BF16)
    halo_blocks = tm // CONV_HALO
    row = lambda a: a.reshape(1, -1)
    const = lambda shape: pl.BlockSpec(shape, lambda b, i: (0,) * len(shape))
    tiles_per_span = MOE_SPAN // tm
    kern = functools.partial(_mix_kernel, tm=tm, d_attn=d_attn, tiles_per_span=tiles_per_span)
    return pl.pallas_call(
        kern,
        grid=(B, nt),
        in_specs=[
            pl.BlockSpec((None, tm, d_conv), lambda b, i: (b, i, 0)),
            pl.BlockSpec((None, CONV_HALO, d_conv), lambda b, i: (b, jnp.maximum(i * halo_blocks - 1, 0), 0)),
            const(conv_w.shape), const((1, d_conv)), const((1, d_conv)), const((1, d_conv)),
            pl.BlockSpec((None, tm, d_attn), lambda b, i: (b, i, 0)),
            const(w_out_bf16.shape),
            pl.BlockSpec((None, tm, D), lambda b, i: (b, i, 0)),
            const((1, D)), const((1, D)),
            const(wr_hi.shape), const(wr_lo.shape), const((1, n_e)),
        ],
        out_specs=[
            pl.BlockSpec((tm, D), lambda b, i: (b * nt + i, 0)),
            pl.BlockSpec((n_e, tm), lambda b, i: (0, b * nt + i)),
            pl.BlockSpec((n_e, tm), lambda b, i: (0, b * nt + i)),
            pl.BlockSpec((None, n_e, 1), lambda b, i: (b * nt + i, 0, 0)),
        ],
        out_shape=[
            jax.ShapeDtypeStruct((N, D), F32),
            jax.ShapeDtypeStruct((n_e, N), I32),
            jax.ShapeDtypeStruct((n_e, N), F32),
            jax.ShapeDtypeStruct((B * nt, n_e, 1), I32),
        ],
        scratch_shapes=[
            pltpu.VMEM((tm + CONV_HALO, d_conv), F32),
            pltpu.VMEM((SUBLANES - 1, tm + CONV_HALO - SUBLANES, d_conv), F32),
            pltpu.VMEM((tm, d_conv), F32),
            pltpu.VMEM((n_e, 1), F32),
        ],
        compiler_params=pltpu.CompilerParams(
            dimension_semantics=("arbitrary", "arbitrary"), vmem_limit_bytes=VMEM_LIMIT),
        name="mix",
    )(h, h, conv_w, row(conv_b), row(cln_g), row(cln_b), attn, w_out_bf16, x, row(ln1_g), row(ln1_b),
      wr_hi, wr_lo, router_bias.reshape(1, n_e))


def _moe_kernel(ntrip_ref, order_ref, x1_ref, rank_ref, gate_ref, *refs, tiers, d_e, d_s, per_trip, span):
    wgu_refs, wd_refs = refs[:per_trip], refs[per_trip:2 * per_trip]
    wsgu_ref, wsd_ref, g2_ref, b2_ref, o_ref, xb_ref = refs[2 * per_trip:]
    w = pl.program_id(0)
    step = pl.program_id(1)
    q = w * pl.num_programs(1) + step

    @pl.when(step == 0)
    def _():
        x1 = x1_ref[...]
        xb = x1.astype(BF16)
        xb_ref[...] = xb
        gu = _dot(xb, wsgu_ref[...])
        g = gu[:, :d_s]
        shared = _dot((g * jax.nn.sigmoid(g) * gu[:, d_s:]).astype(BF16), wsd_ref[...])
        o_ref[...] = ALPHA * x1 + shared

    n_spans = xb_ref.shape[0] // span
    experts = [order_ref[q * per_trip + k] for k in range(per_trip)]

    def trip(c, carry, rows):
        onehots, gathered = [], []
        slot_gates = [[None] * n_spans for _ in range(per_trip)]
        for s in range(n_spans):
            lo, hi = s * span, (s + 1) * span
            hits = []
            for k, e in enumerate(experts):
                rank = rank_ref[pl.ds(e, 1), lo:hi]
                hit = (lax.broadcasted_iota(I32, (rows, 1), 0) + c * rows) == rank
                hits.append(hit)
                slot_gates[k][s] = jnp.sum(jnp.where(hit, gate_ref[pl.ds(e, 1), lo:hi], 0.0), axis=1, keepdims=True)
            onehot = jnp.where(jnp.concatenate(hits, axis=0), 1.0, 0.0).astype(BF16)
            onehots.append(onehot)
            gathered.append(_dot(onehot, xb_ref[lo:hi, :]))
        gated = []
        for k in range(per_trip):
            sl = slice(k * rows, (k + 1) * rows)
            xs = jnp.concatenate([gathered[s][sl] for s in range(n_spans)], axis=0).astype(BF16)
            gu = _dot(xs, wgu_refs[k][0])
            g = gu[:, :d_e]
            y = _dot((g * jax.nn.sigmoid(g) * gu[:, d_e:]).astype(BF16), wd_refs[k][0])
            gated.append(y * jnp.concatenate(slot_gates[k], axis=0))
        for s in range(n_spans):
            sl = slice(s * rows, (s + 1) * rows)
            yg = jnp.concatenate([gated[k][sl] for k in range(per_trip)], axis=0).astype(BF16)
            o_ref[s * span:(s + 1) * span, :] += lax.dot_general(
                onehots[s], yg, (((0,), (0,)), ((), ())), preferred_element_type=F32)
        return carry

    for tier, rows in enumerate(tiers):
        lax.fori_loop(0, ntrip_ref[q * len(tiers) + tier], functools.partial(trip, rows=rows), 0)

    @pl.when(step == pl.num_programs(1) - 1)
    def _():
        o_ref[...] = _layer_norm(o_ref[...], g2_ref[...], b2_ref[...])


def _moe(x1, rank, gates, counts, wgu, wd, ws_gate, ws_up, ws_down, ln2_g, ln2_b):
    N, D = x1.shape
    n_e, d_e, _ = wd.shape
    d_s = ws_down.shape[0]
    W = MOE_WINDOW
    n_win = N // W
    tiers = EXPERT_ROW_TIERS
    span = MOE_SPAN
    per_trip = MOE_EXPERTS_PER_TRIP
    n_steps = n_e // per_trip
    assert len(tiers) >= 2 and all(r % SUBLANES == 0 and (per_trip * r) % (2 * SUBLANES) == 0 for r in tiers)
    assert tiers == tuple(sorted(tiers)) and n_e % per_trip == 0
    per_span = counts.reshape(N // span, span // MIX_ROWS, n_e)[:, -1]
    load = jnp.max(per_span.reshape(n_win, W // span, n_e), axis=1)
    ids = jnp.arange(n_e, dtype=I32)
    before = ((load[:, None, :] < load[:, :, None])
              | ((load[:, None, :] == load[:, :, None]) & (ids[None, None, :] < ids[None, :, None])))
    place = jnp.sum(before, axis=-1).astype(I32)
    at_place = place[:, :, None] == ids[None, None, :]
    order = jnp.sum(jnp.where(at_place, ids[None, :, None], 0), axis=1).astype(I32)
    sorted_load = jnp.sum(jnp.where(at_place, load[:, :, None], 0), axis=1)
    fullest = jnp.max(sorted_load.reshape(n_win, n_steps, per_trip), axis=-1).reshape(-1)
    lower = jnp.asarray((0,) + tiers[:-1], I32)
    fits = (fullest[:, None] > lower[None, :]) & (fullest[:, None] <= jnp.asarray(tiers, I32)[None, :])
    ntrip = fits.astype(I32).at[:, -1].set(
        jnp.where(fullest > tiers[-2], (fullest + tiers[-1] - 1) // tiers[-1], 0).astype(I32)).reshape(-1)
    wsgu = jnp.concatenate([ws_gate, ws_up], axis=-1).astype(BF16)
    kern = functools.partial(_moe_kernel, tiers=tiers, d_e=d_e, d_s=d_s, per_trip=per_trip, span=span)
    const = lambda shape: pl.BlockSpec(shape, lambda w, s, nt, od: (0,) * len(shape))

    def expert_block(shape, k):
        return pl.BlockSpec((1,) + shape, lambda w, s, nt, od: (od[(w * n_steps + s) * per_trip + k], 0, 0))

    return pl.pallas_call(
        kern,
        grid_spec=pltpu.PrefetchScalarGridSpec(
            num_scalar_prefetch=2,
            grid=(n_win, n_steps),
            in_specs=[
                pl.BlockSpec((W, D), lambda w, s, nt, od: (w, 0), pipeline_mode=pl.Buffered(1)),
                pl.BlockSpec((n_e, W), lambda w, s, nt, od: (0, w)),
                pl.BlockSpec((n_e, W), lambda w, s, nt, od: (0, w)),
                *[expert_block((D, 2 * d_e), k) for k in range(per_trip)],
                *[expert_block((d_e, D), k) for k in range(per_trip)],
                const(wsgu.shape), const((d_s, D)), const((1, D)), const((1, D)),
            ],
            out_specs=pl.BlockSpec((W, D), lambda w, s, nt, od: (w, 0)),
            scratch_shapes=[pltpu.VMEM((W, D), BF16)],
        ),
        out_shape=jax.ShapeDtypeStruct((N, D), F32),
        compiler_params=pltpu.CompilerParams(
            dimension_semantics=("arbitrary", "arbitrary"), vmem_limit_bytes=VMEM_LIMIT),
        name="moe",
    )(ntrip, order.reshape(-1), x1, rank, gates, *([wgu] * per_trip), *([wd] * per_trip),
      wsgu, ws_down.astype(BF16), ln2_g.reshape(1, D), ln2_b.reshape(1, D))


def kernel(x, positions, w_in, lambda_q1, lambda_k1, lambda_q2, lambda_k2, subln_g, conv_w, conv_b, conv_ln_g, conv_ln_b, w_out, ln1_g, ln1_b, w_router, router_bias, w_exp_gate, w_exp_up, w_exp_down, w_sh_gate, w_sh_up, w_sh_down, ln2_g, ln2_b):
    B, S, D = x.shape
    d_attn = N_HEADS * HEAD_DIM
    d_conv = conv_w.shape[-1]
    for l in range(DEPTH):
        lambda_init = 0.8 - 0.6 * math.exp(-0.3 * l)
        qT, k, vT, h = _proj(x, positions, w_in[l].astype(BF16), d_attn, d_conv)
        attn, wgu, wd = _attention(qT, k, vT, lambda_q1[l], lambda_k1[l], lambda_q2[l], lambda_k2[l], subln_g[l],
                                   lambda_init, w_exp_gate[l], w_exp_up[l], w_exp_down[l])
        x1, rank, gates, counts = _mix(h, attn, x, conv_w[l], conv_b[l], conv_ln_g[l], conv_ln_b[l],
                                       w_out[l].astype(BF16), ln1_g[l], ln1_b[l], w_router[l], router_bias[l])
        out = _moe(x1, rank, gates, counts, wgu, wd, w_sh_gate[l], w_sh_up[l], w_sh_down[l], ln2_g[l], ln2_b[l])
        x = out.reshape(B, S, D)
    return x
```

```python
import functools
import math

import jax
import jax.numpy as jnp
from jax import lax
from jax.experimental import pallas as pl
from jax.experimental.pallas import tpu as pltpu

F32 = jnp.float32
BF16 = jnp.bfloat16
I32 = jnp.int32

N_HEADS = 4
HEAD_DIM = 128
SUB_DIM = 64
ROPE_DIM = 16
ROPE_THETA = 500000.0
CHUNK = 64
CONV_WIDTH = 31
N_EXPERTS = 64
TOP_K = 8
N_GROUPS = 8
GROUP_SIZE = N_EXPERTS // N_GROUPS
TOPK_GROUPS = 4
ROUTED_SCALE = 2.5
LN_EPS = 1e-5
DEPTH = 1
ALPHA = (2.0 * DEPTH) ** 0.25
LOG2_E = math.log2(math.e)

LANES = 128
SUBLANES = 8
VMEM_LIMIT = 56 * 1024 * 1024

PROJ_ROWS = 512
ATTN_BLOCK = 256
ATTN_SEQS_PER_STEP = 2
MIX_ROWS = 256
CONV_HALO = 32
CONV_ROW_CHUNK = 64
MOE_WINDOW = 2048
MOE_SPAN = 256
EXPERT_ROW_TIERS = (40, 48, 64)
MOE_EXPERTS_PER_TRIP = 4


def _dot(a, b):
    return jnp.dot(a, b, preferred_element_type=F32)


def _layer_norm(z, g, b):
    mu = jnp.mean(z, axis=-1, keepdims=True)
    zc = z - mu
    var = jnp.mean(zc * zc, axis=-1, keepdims=True)
    return zc * lax.rsqrt(var + LN_EPS) * g + b


def _proj_kernel(x_ref, pos_ref, invf_ref, w_ref, qT_ref, k_ref, vT_ref, h_ref, *, d_attn, d_conv, blk):
    xb = x_ref[...].astype(BF16)
    tm = xb.shape[0]
    pos_row = pos_ref[pl.ds(pl.program_id(0), 1), :].astype(F32)
    ang = jnp.broadcast_to(pos_row, (LANES, tm)).T * invf_ref[...]
    cos = jnp.cos(ang)
    sin = jnp.sin(ang)
    d = lax.broadcasted_iota(I32, (1, LANES), 1) & (SUB_DIM - 1)
    half = ROPE_DIM // 2
    c_mul = jnp.where(d < ROPE_DIM, cos, 1.0)
    s_lo = jnp.where(d < half, -sin, 0.0)
    s_hi = jnp.where((d >= half) & (d < ROPE_DIM), sin, 0.0)

    def rope(t):
        outs = []
        for c in range(t.shape[1] // LANES):
            ts = t[:, c * LANES:(c + 1) * LANES]
            outs.append(ts * c_mul + pltpu.roll(ts, LANES - half, 1) * s_lo + pltpu.roll(ts, half, 1) * s_hi)
        return jnp.concatenate(outs, axis=1)

    q = rope(_dot(xb, w_ref[:, 0:d_attn])) * (SUB_DIM ** -0.5 * LOG2_E)
    for c in range(tm // blk):
        qT_ref[0, c] = q[c * blk:(c + 1) * blk, :].T.astype(BF16)
    k = rope(_dot(xb, w_ref[:, d_attn:2 * d_attn]))
    k_ref[0] = k.astype(BF16)
    v = _dot(xb, w_ref[:, 2 * d_attn:3 * d_attn])
    for c in range(tm // blk):
        vT_ref[0, c] = v[c * blk:(c + 1) * blk, :].T.astype(BF16)
    a = _dot(xb, w_ref[:, 3 * d_attn:3 * d_attn + d_conv])
    gate = _dot(xb, w_ref[:, 3 * d_attn + d_conv:3 * d_attn + 2 * d_conv])
    h_ref[0] = a * jax.nn.sigmoid(gate)


def _proj(x, positions, w_in_bf16, d_attn, d_conv):
    B, S, D = x.shape
    tm, blk = PROJ_ROWS, ATTN_BLOCK
    nb = S // blk
    inv_freq = ROPE_THETA ** (-jnp.arange(0, ROPE_DIM, 2, dtype=F32) / ROPE_DIM)
    sub = jnp.concatenate([inv_freq, inv_freq, jnp.zeros((SUB_DIM - ROPE_DIM,), F32)])
    invf = jnp.tile(sub, LANES // SUB_DIM).reshape(1, LANES)
    d_in = w_in_bf16.shape[1]
    kern = functools.partial(_proj_kernel, d_attn=d_attn, d_conv=d_conv, blk=blk)
    return pl.pallas_call(
        kern,
        grid=(B, S // tm),
        in_specs=[
            pl.BlockSpec((None, tm, D), lambda b, i: (b, i, 0)),
            pl.BlockSpec((B, tm), lambda b, i: (0, i)),
            pl.BlockSpec((1, LANES), lambda b, i: (0, 0)),
            pl.BlockSpec((D, d_in), lambda b, i: (0, 0)),
        ],
        out_specs=[
            pl.BlockSpec((1, tm // blk, d_attn, blk), lambda b, i: (b, i, 0, 0)),
            pl.BlockSpec((1, tm, d_attn), lambda b, i: (b, i, 0)),
            pl.BlockSpec((1, tm // blk, d_attn, blk), lambda b, i: (b, i, 0, 0)),
            pl.BlockSpec((1, tm, d_conv), lambda b, i: (b, i, 0)),
        ],
        out_shape=[
            jax.ShapeDtypeStruct((B, nb, d_attn, blk), BF16),
            jax.ShapeDtypeStruct((B, S, d_attn), BF16),
            jax.ShapeDtypeStruct((B, nb, d_attn, blk), BF16),
            jax.ShapeDtypeStruct((B, S, d_conv), F32),
        ],
        compiler_params=pltpu.CompilerParams(
            dimension_semantics=("parallel", "parallel"), vmem_limit_bytes=VMEM_LIMIT),
        name="proj",
    )(x, positions, invf, w_in_bf16)


def _attn_kernel(lq1_ref, lk1_ref, lq2_ref, lk2_ref, g_ref, qT_ref, k_ref, vT_ref, wg_ref, wu_ref, wd_ref,
                 o_ref, wgu_out, wd_out, q_both, s_a, s_b, p_a, p_b, acc_ref, m_ref, *, blk, lambda_init):
    i = pl.program_id(1)
    d_e = wg_ref.shape[-1]
    wgu_out[:, :, :d_e] = wg_ref[...].astype(wgu_out.dtype)
    wgu_out[:, :, d_e:] = wu_ref[...].astype(wgu_out.dtype)
    wd_out[...] = wd_ref[...].astype(wd_out.dtype)
    lam = (jnp.exp(jnp.sum(lq1_ref[...] * lk1_ref[...], axis=-1, keepdims=True))
           - jnp.exp(jnp.sum(lq2_ref[...] * lk2_ref[...], axis=-1, keepdims=True)) + lambda_init)
    row = lax.broadcasted_iota(I32, (HEAD_DIM, 1), 0)
    ones = jnp.ones((2 * SUBLANES, blk), BF16)
    chains = [(bb, h) for bb in range(qT_ref.shape[0]) for h in range(N_HEADS)]

    def head_cols(h):
        return slice(h * HEAD_DIM, (h + 1) * HEAD_DIM)

    for c, (bb, h) in enumerate(chains):
        qT = qT_ref[bb, 0, head_cols(h), :]
        zero = jnp.zeros_like(qT)
        q_both[c] = jnp.concatenate([jnp.where(row < SUB_DIM, qT, zero), jnp.where(row >= SUB_DIM, qT, zero)], axis=1)

    def scores_into(c, j, s_ref):
        bb, h = chains[c]
        kb = k_ref[bb, pl.ds(pl.multiple_of(j * blk, blk), blk), head_cols(h)]
        s_ref[c] = _dot(kb, q_both[c])

    def weighted_values(c, j, p):
        bb, h = chains[c]
        return _dot(jnp.concatenate([vT_ref[bb, j, head_cols(h), :], ones], axis=0), p)

    def probs(c, s):
        m = m_ref[c]
        m_new = jnp.maximum(m, jnp.max(s, axis=0, keepdims=True))
        m_ref[c] = m_new
        return jnp.exp2(m - m_new), jnp.exp2(s - m_new).astype(BF16)

    def step(j, s_cur, s_nxt, p_prev, p_cur):
        for c in range(len(chains)):
            scores_into(c, j + 1, s_nxt)
            pv_prev = weighted_values(c, jnp.maximum(j - 1, 0), p_prev[c])
            alpha, p = probs(c, s_cur[c])
            p_cur[c] = p
            acc_ref[c] = alpha * (acc_ref[c] + pv_prev)

    def finish(s_cur, p_prev):
        kc = lax.broadcasted_iota(I32, (blk, 1), 0) // CHUNK
        qc = (lax.broadcasted_iota(I32, (1, 2 * blk), 1) & (blk - 1)) // CHUNK
        for c, (bb, h) in enumerate(chains):
            pv_prev = weighted_values(c, jnp.maximum(i - 1, 0), p_prev[c])
            alpha, p = probs(c, jnp.where(kc <= qc, s_cur[c], -jnp.inf))
            acc = alpha * (acc_ref[c] + pv_prev) + weighted_values(c, i, p)
            o = acc[:HEAD_DIM] * (1.0 / acc[HEAD_DIM:HEAD_DIM + 1])
            o = o[:, :blk] - lam * o[:, blk:]
            o = o * lax.rsqrt(jnp.mean(o * o, axis=0, keepdims=True) + LN_EPS)
            o_ref[bb, :, head_cols(h)] = (o.T * g_ref[...] * (1.0 - lambda_init)).astype(o_ref.dtype)

    m_ref[...] = jnp.full(m_ref.shape, -jnp.inf, F32)
    acc_ref[...] = jnp.zeros(acc_ref.shape, F32)
    p_b[...] = jnp.zeros(p_b.shape, BF16)
    for c in range(len(chains)):
        scores_into(c, 0, s_a)

    def pair(jj, c):
        step(2 * jj, s_a, s_b, p_b, p_a)
        step(2 * jj + 1, s_b, s_a, p_a, p_b)
        return c

    lax.fori_loop(0, i // 2, pair, 0)

    @pl.when(i % 2 == 1)
    def _():
        step(i - 1, s_a, s_b, p_b, p_a)
        finish(s_b, p_a)

    @pl.when(i % 2 == 0)
    def _():
        finish(s_a, p_b)


def _attention(qT, k, vT, lq1, lk1, lq2, lk2, sub_g, lambda_init, w_gate, w_up, w_down):
    B, nb, d_attn, blk = qT.shape
    S = nb * blk
    n_e, D, d_e = w_gate.shape
    seqs = ATTN_SEQS_PER_STEP if B % ATTN_SEQS_PER_STEP == 0 else 1
    n_chains = seqs * N_HEADS
    per_step = n_e // (B // seqs * nb)
    assert per_step * (B // seqs) * nb == n_e
    kern = functools.partial(_attn_kernel, blk=blk, lambda_init=lambda_init)
    vec = pl.BlockSpec((1, SUB_DIM), lambda b, i: (0, 0))
    w_slice = lambda shape: pl.BlockSpec((per_step,) + shape, lambda b, i: (b * nb + i, 0, 0))
    return pl.pallas_call(
        kern,
        grid=(B // seqs, nb),
        in_specs=[
            vec, vec, vec, vec,
            pl.BlockSpec((1, HEAD_DIM), lambda b, i: (0, 0)),
            pl.BlockSpec((seqs, 1, d_attn, blk), lambda b, i: (b, i, 0, 0)),
            pl.BlockSpec((seqs, S, d_attn), lambda b, i: (b, 0, 0), pipeline_mode=pl.Buffered(1)),
            pl.BlockSpec((seqs, nb, d_attn, blk), lambda b, i: (b, 0, 0, 0), pipeline_mode=pl.Buffered(1)),
            w_slice((D, d_e)), w_slice((D, d_e)), w_slice((d_e, D)),
        ],
        out_specs=[
            pl.BlockSpec((seqs, blk, d_attn), lambda b, i: (b, i, 0)),
            w_slice((D, 2 * d_e)), w_slice((d_e, D)),
        ],
        out_shape=[
            jax.ShapeDtypeStruct((B, S, d_attn), BF16),
            jax.ShapeDtypeStruct((n_e, D, 2 * d_e), BF16),
            jax.ShapeDtypeStruct((n_e, d_e, D), BF16),
        ],
        scratch_shapes=[
            pltpu.VMEM((n_chains, HEAD_DIM, 2 * blk), BF16),
            pltpu.VMEM((n_chains, blk, 2 * blk), F32), pltpu.VMEM((n_chains, blk, 2 * blk), F32),
            pltpu.VMEM((n_chains, blk, 2 * blk), BF16), pltpu.VMEM((n_chains, blk, 2 * blk), BF16),
            pltpu.VMEM((n_chains, HEAD_DIM + 2 * SUBLANES, 2 * blk), F32),
            pltpu.VMEM((n_chains, 1, 2 * blk), F32),
        ],
        compiler_params=pltpu.CompilerParams(
            dimension_semantics=("parallel", "arbitrary"), vmem_limit_bytes=VMEM_LIMIT),
        name="attn",
    )(lq1.reshape(1, SUB_DIM), lk1.reshape(1, SUB_DIM), lq2.reshape(1, SUB_DIM), lk2.reshape(1, SUB_DIM),
      sub_g.reshape(1, HEAD_DIM), qT, k, vT, w_gate, w_up, w_down)


def _route(scores, biased):
    n_e, tm = scores.shape
    neg = -jnp.inf
    g3 = biased.reshape(N_GROUPS, GROUP_SIZE, tm)
    io3 = lax.broadcasted_iota(I32, g3.shape, 1)
    m1 = jnp.max(g3, axis=1, keepdims=True)
    first = jnp.min(jnp.where(g3 == m1, io3, GROUP_SIZE), axis=1, keepdims=True)
    m2 = jnp.max(jnp.where(io3 == first, neg, g3), axis=1, keepdims=True)
    grp_score = (m1 + m2).reshape(N_GROUPS, tm)
    gio = lax.broadcasted_iota(I32, (N_GROUPS, tm), 0)
    grp_sel = jnp.zeros((N_GROUPS, tm), jnp.bool_)
    cur = grp_score
    for _ in range(TOPK_GROUPS):
        mx = jnp.max(cur, axis=0, keepdims=True)
        f = jnp.min(jnp.where(cur == mx, gio, N_GROUPS), axis=0, keepdims=True)
        hit = gio == f
        grp_sel = grp_sel | hit
        cur = jnp.where(hit, neg, cur)
    grp_f = jnp.where(grp_sel, 1.0, 0.0).astype(F32)
    exp_mask = jnp.broadcast_to(grp_f.reshape(N_GROUPS, 1, tm), (N_GROUPS, GROUP_SIZE, tm)).reshape(n_e, tm) > 0.5
    eio = lax.broadcasted_iota(I32, (n_e, tm), 0)
    cur = jnp.where(exp_mask, biased, neg)
    sel = jnp.zeros((n_e, tm), jnp.bool_)
    for _ in range(TOP_K):
        mx = jnp.max(cur, axis=0, keepdims=True)
        f = jnp.min(jnp.where(cur == mx, eio, n_e), axis=0, keepdims=True)
        hit = eio == f
        sel = sel | hit
        cur = jnp.where(hit, neg, cur)
    denom = jnp.sum(jnp.where(sel, scores, 0.0), axis=0, keepdims=True)
    gates = jnp.where(sel, scores / denom * ROUTED_SCALE, 0.0)
    return sel, gates


def _mix_kernel(h_ref, halo_ref, cw_ref, cb_ref, cg_ref, cbeta_ref, attn_ref, wo_ref, x_ref, g1_ref, b1_ref,
                wrh_ref, wrl_ref, rb_ref,
                x1_ref, xb_ref, rank_ref, gate_ref, cnt_ref,
                buf, shifted, cbuf, carry, *, tm, d_attn, tiles_per_span):
    tile = pl.program_id(0) * pl.num_programs(1) + pl.program_id(1)

    @pl.when(tile % tiles_per_span == 0)
    def _():
        carry[...] = jnp.zeros_like(carry)

    @pl.when(pl.program_id(1) == 0)
    def _():
        buf[0:CONV_HALO, :] = jnp.zeros((CONV_HALO, buf.shape[1]), F32)

    @pl.when(pl.program_id(1) != 0)
    def _():
        buf[0:CONV_HALO, :] = halo_ref[...]

    buf[CONV_HALO:, :] = h_ref[...]
    d_conv = buf.shape[1]
    for s in range(1, SUBLANES):
        shifted[s - 1] = buf[s:s + shifted.shape[1], :]
    off = CONV_HALO - (CONV_WIDTH - 1)
    for c in range(d_conv // LANES):
        cs = slice(c * LANES, (c + 1) * LANES)
        for r in range(tm // CONV_ROW_CHUNK):
            acc = jnp.zeros((CONV_ROW_CHUNK, LANES), F32)
            for j in range(CONV_WIDTH):
                s = (off + j) % SUBLANES
                start = r * CONV_ROW_CHUNK + off + j - s
                src = buf if s == 0 else shifted.at[s - 1]
                acc = acc + cw_ref[j:j + 1, cs] * src[start:start + CONV_ROW_CHUNK, cs]
            cbuf[r * CONV_ROW_CHUNK:(r + 1) * CONV_ROW_CHUNK, cs] = acc + cb_ref[:, cs]
    conv = _layer_norm(cbuf[...], cg_ref[...], cbeta_ref[...])
    conv = conv * jax.nn.sigmoid(conv)

    mix = _dot(attn_ref[...], wo_ref[0:d_attn, :]) + _dot(conv.astype(BF16), wo_ref[d_attn:, :])
    x1 = _layer_norm(ALPHA * x_ref[...] + mix, g1_ref[...], b1_ref[...])
    x1_ref[...] = x1

    xh = x1.astype(BF16)
    xb_ref[...] = xh
    xl = (x1 - xh.astype(F32)).astype(BF16)
    logits = _dot(xh, wrh_ref[...]) + _dot(xl, wrh_ref[...]) + _dot(xh, wrl_ref[...])
    scores = jax.nn.sigmoid(logits)
    sel, gates = _route(scores.T, (scores + rb_ref[...]).T)
    gate_ref[...] = gates

    t_row = lax.broadcasted_iota(I32, (tm, tm), 0)
    t_col = lax.broadcasted_iota(I32, (tm, tm), 1)
    before = jnp.where(t_row < t_col, 1.0, 0.0).astype(BF16)
    sel_f = jnp.where(sel, 1.0, 0.0).astype(F32)
    rank = _dot(sel_f.astype(BF16), before) + carry[...]
    rank_ref[...] = jnp.where(sel, rank, -1.0).astype(I32)
    carry[...] = carry[...] + jnp.sum(sel_f, axis=1, keepdims=True)
    cnt_ref[...] = carry[...].astype(I32)


def _mix(h, attn, x, conv_w, conv_b, cln_g, cln_b, w_out_bf16, ln1_g, ln1_b, w_router, router_bias):
    B, S, D = x.shape
    d_conv = h.shape[-1]
    d_attn = attn.shape[-1]
    tm = MIX_ROWS
    nt = S // tm
    N = B * S
    n_e = w_router.shape[-1]
    wr_hi = w_router.astype(BF16)
    wr_lo = (w_router - wr_hi.astype(F32)).astype(BF16)
    halo_blocks = tm // CONV_HALO
    row = lambda a: a.reshape(1, -1)
    const = lambda shape: pl.BlockSpec(shape, lambda b, i: (0,) * len(shape))
    tiles_per_span = MOE_SPAN // tm
    kern = functools.partial(_mix_kernel, tm=tm, d_attn=d_attn, tiles_per_span=tiles_per_span)
    return pl.pallas_call(
        kern,
        grid=(B, nt),
        in_specs=[
            pl.BlockSpec((None, tm, d_conv), lambda b, i: (b, i, 0)),
            pl.BlockSpec((None, CONV_HALO, d_conv), lambda b, i: (b, jnp.maximum(i * halo_blocks - 1, 0), 0)),
            const(conv_w.shape), const((1, d_conv)), const((1, d_conv)), const((1, d_conv)),
            pl.BlockSpec((None, tm, d_attn), lambda b, i: (b, i, 0)),
            const(w_out_bf16.shape),
            pl.BlockSpec((None, tm, D), lambda b, i: (b, i, 0)),
            const((1, D)), const((1, D)),
            const(wr_hi.shape), const(wr_lo.shape), const((1, n_e)),
        ],
        out_specs=[
            pl.BlockSpec((tm, D), lambda b, i: (b * nt + i, 0)),
            pl.BlockSpec((tm, D), lambda b, i: (b * nt + i, 0)),
            pl.BlockSpec((n_e, tm), lambda b, i: (0, b * nt + i)),
            pl.BlockSpec((n_e, tm), lambda b, i: (0, b * nt + i)),
            pl.BlockSpec((None, n_e, 1), lambda b, i: (b * nt + i, 0, 0)),
        ],
        out_shape=[
            jax.ShapeDtypeStruct((N, D), F32),
            jax.ShapeDtypeStruct((N, D), BF16),
            jax.ShapeDtypeStruct((n_e, N), I32),
            jax.ShapeDtypeStruct((n_e, N), F32),
            jax.ShapeDtypeStruct((B * nt, n_e, 1), I32),
        ],
        scratch_shapes=[
            pltpu.VMEM((tm + CONV_HALO, d_conv), F32),
            pltpu.VMEM((SUBLANES - 1, tm + CONV_HALO - SUBLANES, d_conv), F32),
            pltpu.VMEM((tm, d_conv), F32),
            pltpu.VMEM((n_e, 1), F32),
        ],
        compiler_params=pltpu.CompilerParams(
            dimension_semantics=("arbitrary", "arbitrary"), vmem_limit_bytes=VMEM_LIMIT),
        name="mix",
    )(h, h, conv_w, row(conv_b), row(cln_g), row(cln_b), attn, w_out_bf16, x, row(ln1_g), row(ln1_b),
      wr_hi, wr_lo, router_bias.reshape(1, n_e))


def _moe_kernel(ntrip_ref, order_ref, x1_ref, xb_ref, rank_ref, gate_ref, *refs, tiers, d_e, d_s, per_trip, span):
    wgu_refs, wd_refs = refs[:per_trip], refs[per_trip:2 * per_trip]
    wsgu_ref, wsd_ref, g2_ref, b2_ref, o_ref = refs[2 * per_trip:]
    w = pl.program_id(0)
    step = pl.program_id(1)
    q = w * pl.num_programs(1) + step

    @pl.when(step == 0)
    def _():
        gu = _dot(xb_ref[...], wsgu_ref[...])
        g = gu[:, :d_s]
        o_ref[...] = _dot((g * jax.nn.sigmoid(g) * gu[:, d_s:]).astype(BF16), wsd_ref[...])

    res_rows = x1_ref.shape[0]
    res = pl.ds(pl.multiple_of(step * res_rows, res_rows), res_rows)
    o_ref[res, :] = o_ref[res, :] + ALPHA * x1_ref[...]

    n_spans = xb_ref.shape[0] // span
    experts = [order_ref[q * per_trip + k] for k in range(per_trip)]

    def trip(c, carry, rows):
        onehots, gathered = [], []
        slot_gates = [[None] * n_spans for _ in range(per_trip)]
        for s in range(n_spans):
            lo, hi = s * span, (s + 1) * span
            hits = []
            for k, e in enumerate(experts):
                rank = rank_ref[pl.ds(e, 1), lo:hi]
                hit = (lax.broadcasted_iota(I32, (rows, 1), 0) + c * rows) == rank
                hits.append(hit)
                slot_gates[k][s] = jnp.sum(jnp.where(hit, gate_ref[pl.ds(e, 1), lo:hi], 0.0), axis=1, keepdims=True)
            onehot = jnp.where(jnp.concatenate(hits, axis=0), 1.0, 0.0).astype(BF16)
            onehots.append(onehot)
            gathered.append(_dot(onehot, xb_ref[lo:hi, :]))
        gated = []
        for k in range(per_trip):
            sl = slice(k * rows, (k + 1) * rows)
            xs = jnp.concatenate([gathered[s][sl] for s in range(n_spans)], axis=0).astype(BF16)
            gu = _dot(xs, wgu_refs[k][0])
            g = gu[:, :d_e]
            y = _dot((g * jax.nn.sigmoid(g) * gu[:, d_e:]).astype(BF16), wd_refs[k][0])
            gated.append(y * jnp.concatenate(slot_gates[k], axis=0))
        for s in range(n_spans):
            sl = slice(s * rows, (s + 1) * rows)
            yg = jnp.concatenate([gated[k][sl] for k in range(per_trip)], axis=0).astype(BF16)
            o_ref[s * span:(s + 1) * span, :] += lax.dot_general(
                onehots[s], yg, (((0,), (0,)), ((), ())), preferred_element_type=F32)
        return carry

    for tier, rows in enumerate(tiers):
        lax.fori_loop(0, ntrip_ref[q * len(tiers) + tier], functools.partial(trip, rows=rows), 0)

    @pl.when(step == pl.num_programs(1) - 1)
    def _():
        o_ref[...] = _layer_norm(o_ref[...], g2_ref[...], b2_ref[...])


def _moe(x1, xb, rank, gates, counts, wgu, wd, ws_gate, ws_up, ws_down, ln2_g, ln2_b):
    N, D = x1.shape
    n_e, d_e, _ = wd.shape
    d_s = ws_down.shape[0]
    W = MOE_WINDOW
    n_win = N // W
    tiers = EXPERT_ROW_TIERS
    span = MOE_SPAN
    per_trip = MOE_EXPERTS_PER_TRIP
    n_steps = n_e // per_trip
    assert len(tiers) >= 2 and all(r % SUBLANES == 0 and (per_trip * r) % (2 * SUBLANES) == 0 for r in tiers)
    assert tiers == tuple(sorted(tiers)) and n_e % per_trip == 0 and W % (n_steps * SUBLANES) == 0
    per_span = counts.reshape(N // span, span // MIX_ROWS, n_e)[:, -1]
    load = jnp.max(per_span.reshape(n_win, W // span, n_e), axis=1)
    ids = jnp.arange(n_e, dtype=I32)
    before = ((load[:, None, :] < load[:, :, None])
              | ((load[:, None, :] == load[:, :, None]) & (ids[None, None, :] < ids[None, :, None])))
    place = jnp.sum(before, axis=-1).astype(I32)
    at_place = place[:, :, None] == ids[None, None, :]
    order = jnp.sum(jnp.where(at_place, ids[None, :, None], 0), axis=1).astype(I32)
    sorted_load = jnp.sum(jnp.where(at_place, load[:, :, None], 0), axis=1)
    fullest = jnp.max(sorted_load.reshape(n_win, n_steps, per_trip), axis=-1).reshape(-1)
    lower = jnp.asarray((0,) + tiers[:-1], I32)
    fits = (fullest[:, None] > lower[None, :]) & (fullest[:, None] <= jnp.asarray(tiers, I32)[None, :])
    ntrip = fits.astype(I32).at[:, -1].set(
        jnp.where(fullest > tiers[-2], (fullest + tiers[-1] - 1) // tiers[-1], 0).astype(I32)).reshape(-1)
    wsgu = jnp.concatenate([ws_gate, ws_up], axis=-1).astype(BF16)
    kern = functools.partial(_moe_kernel, tiers=tiers, d_e=d_e, d_s=d_s, per_trip=per_trip, span=span)
    const = lambda shape: pl.BlockSpec(shape, lambda w, s, nt, od: (0,) * len(shape))

    def expert_block(shape, k):
        return pl.BlockSpec((1,) + shape, lambda w, s, nt, od: (od[(w * n_steps + s) * per_trip + k], 0, 0))

    return pl.pallas_call(
        kern,
        grid_spec=pltpu.PrefetchScalarGridSpec(
            num_scalar_prefetch=2,
            grid=(n_win, n_steps),
            in_specs=[
                pl.BlockSpec((W // n_steps, D), lambda w, s, nt, od: (w * n_steps + s, 0)),
                pl.BlockSpec((W, D), lambda w, s, nt, od: (w, 0)),
                pl.BlockSpec((n_e, W), lambda w, s, nt, od: (0, w)),
                pl.BlockSpec((n_e, W), lambda w, s, nt, od: (0, w)),
                *[expert_block((D, 2 * d_e), k) for k in range(per_trip)],
                *[expert_block((d_e, D), k) for k in range(per_trip)],
                const(wsgu.shape), const((d_s, D)), const((1, D)), const((1, D)),
            ],
            out_specs=pl.BlockSpec((W, D), lambda w, s, nt, od: (w, 0)),
        ),
        out_shape=jax.ShapeDtypeStruct((N, D), F32),
        compiler_params=pltpu.CompilerParams(
            dimension_semantics=("arbitrary", "arbitrary"), vmem_limit_bytes=VMEM_LIMIT),
        name="moe",
    )(ntrip, order.reshape(-1), x1, xb, rank, gates, *([wgu] * per_trip), *([wd] * per_trip),
      wsgu, ws_down.astype(BF16), ln2_g.reshape(1, D), ln2_b.reshape(1, D))


def kernel(x, positions, w_in, lambda_q1, lambda_k1, lambda_q2, lambda_k2, subln_g, conv_w, conv_b, conv_ln_g, conv_ln_b, w_out, ln1_g, ln1_b, w_router, router_bias, w_exp_gate, w_exp_up, w_exp_down, w_sh_gate, w_sh_up, w_sh_down, ln2_g, ln2_b):
    B, S, D = x.shape
    d_attn = N_HEADS * HEAD_DIM
    d_conv = conv_w.shape[-1]
    for l in range(DEPTH):
        lambda_init = 0.8 - 0.6 * math.exp(-0.3 * l)
        qT, k, vT, h = _proj(x, positions, w_in[l].astype(BF16), d_attn, d_conv)
        attn, wgu, wd = _attention(qT, k, vT, lambda_q1[l], lambda_k1[l], lambda_q2[l], lambda_k2[l], subln_g[l],
                                   lambda_init, w_exp_gate[l], w_exp_up[l], w_exp_down[l])
        x1, xb, rank, gates, counts = _mix(h, attn, x, conv_w[l], conv_b[l], conv_ln_g[l], conv_ln_b[l],
                                       w_out[l].astype(BF16), ln1_g[l], ln1_b[l], w_router[l], router_bias[l])
        out = _moe(x1, xb, rank, gates, counts, wgu, wd, w_sh_gate[l], w_sh_up[l], w_sh_down[l], ln2_g[l], ln2_b[l])
        x = out.reshape(B, S, D)
    return x
```

```python
import functools
import math

import jax
import jax.numpy as jnp
from jax import lax
from jax.experimental import pallas as pl
from jax.experimental.pallas import tpu as pltpu

F32 = jnp.float32
BF16 = jnp.bfloat16
I32 = jnp.int32

N_HEADS = 4
HEAD_DIM = 128
SUB_DIM = 64
ROPE_DIM = 16
ROPE_THETA = 500000.0
CHUNK = 64
CONV_WIDTH = 31
N_EXPERTS = 64
TOP_K = 8
N_GROUPS = 8
GROUP_SIZE = N_EXPERTS // N_GROUPS
TOPK_GROUPS = 4
ROUTED_SCALE = 2.5
LN_EPS = 1e-5
DEPTH = 1
ALPHA = (2.0 * DEPTH) ** 0.25
LOG2_E = math.log2(math.e)

LANES = 128
SUBLANES = 8
VMEM_LIMIT = 58 * 1024 * 1024

PROJ_ROWS = 512
ATTN_BLOCK = 256
ATTN_SEQS_PER_STEP = 2
MIX_ROWS = 256
CONV_HALO = 32
CONV_ROW_CHUNK = 64
MOE_WINDOW = 2048
MOE_SPAN = 256
EXPERT_ROW_TIERS = (40, 48, 64)
MOE_EXPERTS_PER_TRIP = 4


def _dot(a, b):
    return jnp.dot(a, b, preferred_element_type=F32)


def _layer_norm(z, g, b):
    mu = jnp.mean(z, axis=-1, keepdims=True)
    zc = z - mu
    var = jnp.mean(zc * zc, axis=-1, keepdims=True)
    return zc * lax.rsqrt(var + LN_EPS) * g + b


def _proj_kernel(x_ref, pos_ref, invf_ref, w_ref, qT_ref, k_ref, vT_ref, h_ref, *, d_attn, d_conv, blk):
    xb = x_ref[...].astype(BF16)
    tm = xb.shape[0]
    pos_row = pos_ref[pl.ds(pl.program_id(0), 1), :].astype(F32)
    ang = jnp.broadcast_to(pos_row, (LANES, tm)).T * invf_ref[...]
    cos = jnp.cos(ang)
    sin = jnp.sin(ang)
    d = lax.broadcasted_iota(I32, (1, LANES), 1) & (SUB_DIM - 1)
    half = ROPE_DIM // 2
    c_mul = jnp.where(d < ROPE_DIM, cos, 1.0)
    s_lo = jnp.where(d < half, -sin, 0.0)
    s_hi = jnp.where((d >= half) & (d < ROPE_DIM), sin, 0.0)

    def rope(t):
        outs = []
        for c in range(t.shape[1] // LANES):
            ts = t[:, c * LANES:(c + 1) * LANES]
            outs.append(ts * c_mul + pltpu.roll(ts, LANES - half, 1) * s_lo + pltpu.roll(ts, half, 1) * s_hi)
        return jnp.concatenate(outs, axis=1)

    q = rope(_dot(xb, w_ref[:, 0:d_attn])) * (SUB_DIM ** -0.5 * LOG2_E)
    for c in range(tm // blk):
        qT_ref[0, c] = q[c * blk:(c + 1) * blk, :].T.astype(BF16)
    k = rope(_dot(xb, w_ref[:, d_attn:2 * d_attn]))
    k_ref[0] = k.astype(BF16)
    v = _dot(xb, w_ref[:, 2 * d_attn:3 * d_attn])
    for c in range(tm // blk):
        vT_ref[0, c] = v[c * blk:(c + 1) * blk, :].T.astype(BF16)
    a = _dot(xb, w_ref[:, 3 * d_attn:3 * d_attn + d_conv])
    gate = _dot(xb, w_ref[:, 3 * d_attn + d_conv:3 * d_attn + 2 * d_conv])
    h_ref[0] = a * jax.nn.sigmoid(gate)


def _proj(x, positions, w_in_bf16, d_attn, d_conv):
    B, S, D = x.shape
    tm, blk = PROJ_ROWS, ATTN_BLOCK
    nb = S // blk
    inv_freq = ROPE_THETA ** (-jnp.arange(0, ROPE_DIM, 2, dtype=F32) / ROPE_DIM)
    sub = jnp.concatenate([inv_freq, inv_freq, jnp.zeros((SUB_DIM - ROPE_DIM,), F32)])
    invf = jnp.tile(sub, LANES // SUB_DIM).reshape(1, LANES)
    d_in = w_in_bf16.shape[1]
    kern = functools.partial(_proj_kernel, d_attn=d_attn, d_conv=d_conv, blk=blk)
    return pl.pallas_call(
        kern,
        grid=(B, S // tm),
        in_specs=[
            pl.BlockSpec((None, tm, D), lambda b, i: (b, i, 0)),
            pl.BlockSpec((B, tm), lambda b, i: (0, i)),
            pl.BlockSpec((1, LANES), lambda b, i: (0, 0)),
            pl.BlockSpec((D, d_in), lambda b, i: (0, 0)),
        ],
        out_specs=[
            pl.BlockSpec((1, tm // blk, d_attn, blk), lambda b, i: (b, i, 0, 0)),
            pl.BlockSpec((1, tm, d_attn), lambda b, i: (b, i, 0)),
            pl.BlockSpec((1, tm // blk, d_attn, blk), lambda b, i: (b, i, 0, 0)),
            pl.BlockSpec((1, tm, d_conv), lambda b, i: (b, i, 0)),
        ],
        out_shape=[
            jax.ShapeDtypeStruct((B, nb, d_attn, blk), BF16),
            jax.ShapeDtypeStruct((B, S, d_attn), BF16),
            jax.ShapeDtypeStruct((B, nb, d_attn, blk), BF16),
            jax.ShapeDtypeStruct((B, S, d_conv), F32),
        ],
        compiler_params=pltpu.CompilerParams(
            dimension_semantics=("parallel", "parallel"), vmem_limit_bytes=VMEM_LIMIT),
        name="proj",
    )(x, positions, invf, w_in_bf16)


def _attn_kernel(lq1_ref, lk1_ref, lq2_ref, lk2_ref, g_ref, qT_ref, k_ref, vT_ref, wg_ref, wu_ref, wd_ref,
                 o_ref, wgu_out, wd_out, q_both, s_a, s_b, p_a, p_b, acc_ref, m_ref, k_all, v_all,
                 *, blk, lambda_init):
    i = pl.program_id(1)
    for bb in range(k_ref.shape[0]):
        k_all[bb, pl.ds(pl.multiple_of(i * blk, blk), blk), :] = k_ref[bb]
        v_all[bb, i] = vT_ref[bb, 0]
    d_e = wg_ref.shape[-1]
    wgu_out[:, :, :d_e] = wg_ref[...].astype(wgu_out.dtype)
    wgu_out[:, :, d_e:] = wu_ref[...].astype(wgu_out.dtype)
    wd_out[...] = wd_ref[...].astype(wd_out.dtype)
    lam = (jnp.exp(jnp.sum(lq1_ref[...] * lk1_ref[...], axis=-1, keepdims=True))
           - jnp.exp(jnp.sum(lq2_ref[...] * lk2_ref[...], axis=-1, keepdims=True)) + lambda_init)
    row = lax.broadcasted_iota(I32, (HEAD_DIM, 1), 0)
    ones = jnp.ones((2 * SUBLANES, blk), BF16)
    chains = [(bb, h) for bb in range(qT_ref.shape[0]) for h in range(N_HEADS)]

    def head_cols(h):
        return slice(h * HEAD_DIM, (h + 1) * HEAD_DIM)

    for c, (bb, h) in enumerate(chains):
        qT = qT_ref[bb, 0, head_cols(h), :]
        zero = jnp.zeros_like(qT)
        q_both[c] = jnp.concatenate([jnp.where(row < SUB_DIM, qT, zero), jnp.where(row >= SUB_DIM, qT, zero)], axis=1)

    def scores_into(c, j, s_ref):
        bb, h = chains[c]
        kb = k_all[bb, pl.ds(pl.multiple_of(j * blk, blk), blk), head_cols(h)]
        s_ref[c] = _dot(kb, q_both[c])

    def weighted_values(c, j, p):
        bb, h = chains[c]
        return _dot(jnp.concatenate([v_all[bb, j, head_cols(h), :], ones], axis=0), p)

    def probs(c, s):
        m = m_ref[c]
        m_new = jnp.maximum(m, jnp.max(s, axis=0, keepdims=True))
        m_ref[c] = m_new
        return jnp.exp2(m - m_new), jnp.exp2(s - m_new).astype(BF16)

    def step(j, s_cur, s_nxt, p_prev, p_cur):
        for c in range(len(chains)):
            scores_into(c, j + 1, s_nxt)
            pv_prev = weighted_values(c, jnp.maximum(j - 1, 0), p_prev[c])
            alpha, p = probs(c, s_cur[c])
            p_cur[c] = p
            acc_ref[c] = alpha * (acc_ref[c] + pv_prev)

    def finish(s_cur, p_prev):
        kc = lax.broadcasted_iota(I32, (blk, 1), 0) // CHUNK
        qc = (lax.broadcasted_iota(I32, (1, 2 * blk), 1) & (blk - 1)) // CHUNK
        for c, (bb, h) in enumerate(chains):
            pv_prev = weighted_values(c, jnp.maximum(i - 1, 0), p_prev[c])
            alpha, p = probs(c, jnp.where(kc <= qc, s_cur[c], -jnp.inf))
            acc = alpha * (acc_ref[c] + pv_prev) + weighted_values(c, i, p)
            o = acc[:HEAD_DIM] * (1.0 / acc[HEAD_DIM:HEAD_DIM + 1])
            o = o[:, :blk] - lam * o[:, blk:]
            o = o * lax.rsqrt(jnp.mean(o * o, axis=0, keepdims=True) + LN_EPS)
            o_ref[bb, :, head_cols(h)] = (o.T * g_ref[...] * (1.0 - lambda_init)).astype(o_ref.dtype)

    m_ref[...] = jnp.full(m_ref.shape, -jnp.inf, F32)
    acc_ref[...] = jnp.zeros(acc_ref.shape, F32)
    p_b[...] = jnp.zeros(p_b.shape, BF16)
    for c in range(len(chains)):
        scores_into(c, 0, s_a)

    def pair(jj, c):
        step(2 * jj, s_a, s_b, p_b, p_a)
        step(2 * jj + 1, s_b, s_a, p_a, p_b)
        return c

    lax.fori_loop(0, i // 2, pair, 0)

    @pl.when(i % 2 == 1)
    def _():
        step(i - 1, s_a, s_b, p_b, p_a)
        finish(s_b, p_a)

    @pl.when(i % 2 == 0)
    def _():
        finish(s_a, p_b)


def _attention(qT, k, vT, lq1, lk1, lq2, lk2, sub_g, lambda_init, w_gate, w_up, w_down):
    B, nb, d_attn, blk = qT.shape
    S = nb * blk
    n_e, D, d_e = w_gate.shape
    seqs = ATTN_SEQS_PER_STEP if B % ATTN_SEQS_PER_STEP == 0 else 1
    n_chains = seqs * N_HEADS
    per_step = n_e // (B // seqs * nb)
    assert per_step * (B // seqs) * nb == n_e
    kern = functools.partial(_attn_kernel, blk=blk, lambda_init=lambda_init)
    vec = pl.BlockSpec((1, SUB_DIM), lambda b, i: (0, 0))
    w_slice = lambda shape: pl.BlockSpec((per_step,) + shape, lambda b, i: (b * nb + i, 0, 0))
    return pl.pallas_call(
        kern,
        grid=(B // seqs, nb),
        in_specs=[
            vec, vec, vec, vec,
            pl.BlockSpec((1, HEAD_DIM), lambda b, i: (0, 0)),
            pl.BlockSpec((seqs, 1, d_attn, blk), lambda b, i: (b, i, 0, 0)),
            pl.BlockSpec((seqs, blk, d_attn), lambda b, i: (b, i, 0)),
            pl.BlockSpec((seqs, 1, d_attn, blk), lambda b, i: (b, i, 0, 0)),
            w_slice((D, d_e)), w_slice((D, d_e)), w_slice((d_e, D)),
        ],
        out_specs=[
            pl.BlockSpec((seqs, blk, d_attn), lambda b, i: (b, i, 0)),
            w_slice((D, 2 * d_e)), w_slice((d_e, D)),
        ],
        out_shape=[
            jax.ShapeDtypeStruct((B, S, d_attn), BF16),
            jax.ShapeDtypeStruct((n_e, D, 2 * d_e), BF16),
            jax.ShapeDtypeStruct((n_e, d_e, D), BF16),
        ],
        scratch_shapes=[
            pltpu.VMEM((n_chains, HEAD_DIM, 2 * blk), BF16),
            pltpu.VMEM((n_chains, blk, 2 * blk), F32), pltpu.VMEM((n_chains, blk, 2 * blk), F32),
            pltpu.VMEM((n_chains, blk, 2 * blk), BF16), pltpu.VMEM((n_chains, blk, 2 * blk), BF16),
            pltpu.VMEM((n_chains, HEAD_DIM + 2 * SUBLANES, 2 * blk), F32),
            pltpu.VMEM((n_chains, 1, 2 * blk), F32),
            pltpu.VMEM((seqs, S, d_attn), BF16), pltpu.VMEM((seqs, nb, d_attn, blk), BF16),
        ],
        compiler_params=pltpu.CompilerParams(
            dimension_semantics=("parallel", "arbitrary"), vmem_limit_bytes=VMEM_LIMIT),
        name="attn",
    )(lq1.reshape(1, SUB_DIM), lk1.reshape(1, SUB_DIM), lq2.reshape(1, SUB_DIM), lk2.reshape(1, SUB_DIM),
      sub_g.reshape(1, HEAD_DIM), qT, k, vT, w_gate, w_up, w_down)


def _route(scores, biased):
    n_e, tm = scores.shape
    neg = -jnp.inf
    g3 = biased.reshape(N_GROUPS, GROUP_SIZE, tm)
    io3 = lax.broadcasted_iota(I32, g3.shape, 1)
    m1 = jnp.max(g3, axis=1, keepdims=True)
    first = jnp.min(jnp.where(g3 == m1, io3, GROUP_SIZE), axis=1, keepdims=True)
    m2 = jnp.max(jnp.where(io3 == first, neg, g3), axis=1, keepdims=True)
    grp_score = (m1 + m2).reshape(N_GROUPS, tm)
    gio = lax.broadcasted_iota(I32, (N_GROUPS, tm), 0)
    grp_sel = jnp.zeros((N_GROUPS, tm), jnp.bool_)
    cur = grp_score
    for _ in range(TOPK_GROUPS):
        mx = jnp.max(cur, axis=0, keepdims=True)
        f = jnp.min(jnp.where(cur == mx, gio, N_GROUPS), axis=0, keepdims=True)
        hit = gio == f
        grp_sel = grp_sel | hit
        cur = jnp.where(hit, neg, cur)
    grp_f = jnp.where(grp_sel, 1.0, 0.0).astype(F32)
    exp_mask = jnp.broadcast_to(grp_f.reshape(N_GROUPS, 1, tm), (N_GROUPS, GROUP_SIZE, tm)).reshape(n_e, tm) > 0.5
    eio = lax.broadcasted_iota(I32, (n_e, tm), 0)
    cur = jnp.where(exp_mask, biased, neg)
    sel = jnp.zeros((n_e, tm), jnp.bool_)
    for _ in range(TOP_K):
        mx = jnp.max(cur, axis=0, keepdims=True)
        f = jnp.min(jnp.where(cur == mx, eio, n_e), axis=0, keepdims=True)
        hit = eio == f
        sel = sel | hit
        cur = jnp.where(hit, neg, cur)
    denom = jnp.sum(jnp.where(sel, scores, 0.0), axis=0, keepdims=True)
    gates = jnp.where(sel, scores / denom * ROUTED_SCALE, 0.0)
    return sel, gates


def _mix_kernel(h_ref, halo_ref, cw_ref, cb_ref, cg_ref, cbeta_ref, attn_ref, wo_ref, x_ref, g1_ref, b1_ref,
                wrh_ref, wrl_ref, rb_ref,
                x1_ref, xb_ref, rank_ref, gate_ref, cnt_ref,
                buf, shifted, cbuf, carry, *, tm, d_attn, tiles_per_span):
    tile = pl.program_id(0) * pl.num_programs(1) + pl.program_id(1)

    @pl.when(tile % tiles_per_span == 0)
    def _():
        carry[...] = jnp.zeros_like(carry)

    @pl.when(pl.program_id(1) == 0)
    def _():
        buf[0:CONV_HALO, :] = jnp.zeros((CONV_HALO, buf.shape[1]), F32)

    @pl.when(pl.program_id(1) != 0)
    def _():
        buf[0:CONV_HALO, :] = halo_ref[...]

    buf[CONV_HALO:, :] = h_ref[...]
    d_conv = buf.shape[1]
    for s in range(1, SUBLANES):
        shifted[s - 1] = buf[s:s + shifted.shape[1], :]
    off = CONV_HALO - (CONV_WIDTH - 1)
    for c in range(d_conv // LANES):
        cs = slice(c * LANES, (c + 1) * LANES)
        for r in range(tm // CONV_ROW_CHUNK):
            acc = jnp.zeros((CONV_ROW_CHUNK, LANES), F32)
            for j in range(CONV_WIDTH):
                s = (off + j) % SUBLANES
                start = r * CONV_ROW_CHUNK + off + j - s
                src = buf if s == 0 else shifted.at[s - 1]
                acc = acc + cw_ref[j:j + 1, cs] * src[start:start + CONV_ROW_CHUNK, cs]
            cbuf[r * CONV_ROW_CHUNK:(r + 1) * CONV_ROW_CHUNK, cs] = acc + cb_ref[:, cs]
    conv = _layer_norm(cbuf[...], cg_ref[...], cbeta_ref[...])
    conv = conv * jax.nn.sigmoid(conv)

    mix = _dot(attn_ref[...], wo_ref[0:d_attn, :]) + _dot(conv.astype(BF16), wo_ref[d_attn:, :])
    x1 = _layer_norm(ALPHA * x_ref[...] + mix, g1_ref[...], b1_ref[...])
    x1_ref[...] = x1

    xh = x1.astype(BF16)
    xb_ref[...] = xh
    xl = (x1 - xh.astype(F32)).astype(BF16)
    logits = _dot(xh, wrh_ref[...]) + _dot(xl, wrh_ref[...]) + _dot(xh, wrl_ref[...])
    scores = jax.nn.sigmoid(logits)
    sel, gates = _route(scores.T, (scores + rb_ref[...]).T)
    gate_ref[...] = gates

    t_row = lax.broadcasted_iota(I32, (tm, tm), 0)
    t_col = lax.broadcasted_iota(I32, (tm, tm), 1)
    before = jnp.where(t_row < t_col, 1.0, 0.0).astype(BF16)
    sel_f = jnp.where(sel, 1.0, 0.0).astype(F32)
    rank = _dot(sel_f.astype(BF16), before) + carry[...]
    rank_ref[...] = jnp.where(sel, rank, -1.0).astype(I32)
    carry[...] = carry[...] + jnp.sum(sel_f, axis=1, keepdims=True)
    cnt_ref[...] = carry[...].astype(I32)


def _mix(h, attn, x, conv_w, conv_b, cln_g, cln_b, w_out_bf16, ln1_g, ln1_b, w_router, router_bias):
    B, S, D = x.shape
    d_conv = h.shape[-1]
    d_attn = attn.shape[-1]
    tm = MIX_ROWS
    nt = S // tm
    N = B * S
    n_e = w_router.shape[-1]
    wr_hi = w_router.astype(BF16)
    wr_lo = (w_router - wr_hi.astype(F32)).astype(BF16)
    halo_blocks = tm // CONV_HALO
    row = lambda a: a.reshape(1, -1)
    const = lambda shape: pl.BlockSpec(shape, lambda b, i: (0,) * len(shape))
    tiles_per_span = MOE_SPAN // tm
    kern = functools.partial(_mix_kernel, tm=tm, d_attn=d_attn, tiles_per_span=tiles_per_span)
    return pl.pallas_call(
        kern,
        grid=(B, nt),
        in_specs=[
            pl.BlockSpec((None, tm, d_conv), lambda b, i: (b, i, 0)),
            pl.BlockSpec((None, CONV_HALO, d_conv), lambda b, i: (b, jnp.maximum(i * halo_blocks - 1, 0), 0)),
            const(conv_w.shape), const((1, d_conv)), const((1, d_conv)), const((1, d_conv)),
            pl.BlockSpec((None, tm, d_attn), lambda b, i: (b, i, 0)),
            const(w_out_bf16.shape),
            pl.BlockSpec((None, tm, D), lambda b, i: (b, i, 0)),
            const((1, D)), const((1, D)),
            const(wr_hi.shape), const(wr_lo.shape), const((1, n_e)),
        ],
        out_specs=[
            pl.BlockSpec((tm, D), lambda b, i: (b * nt + i, 0)),
            pl.BlockSpec((tm, D), lambda b, i: (b * nt + i, 0)),
            pl.BlockSpec((n_e, tm), lambda b, i: (0, b * nt + i)),
            pl.BlockSpec((n_e, tm), lambda b, i: (0, b * nt + i)),
            pl.BlockSpec((None, n_e, 1), lambda b, i: (b * nt + i, 0, 0)),
        ],
        out_shape=[
            jax.ShapeDtypeStruct((N, D), F32),
            jax.ShapeDtypeStruct((N, D), BF16),
            jax.ShapeDtypeStruct((n_e, N), I32),
            jax.ShapeDtypeStruct((n_e, N), F32),
            jax.ShapeDtypeStruct((B * nt, n_e, 1), I32),
        ],
        scratch_shapes=[
            pltpu.VMEM((tm + CONV_HALO, d_conv), F32),
            pltpu.VMEM((SUBLANES - 1, tm + CONV_HALO - SUBLANES, d_conv), F32),
            pltpu.VMEM((tm, d_conv), F32),
            pltpu.VMEM((n_e, 1), F32),
        ],
        compiler_params=pltpu.CompilerParams(
            dimension_semantics=("arbitrary", "arbitrary"), vmem_limit_bytes=VMEM_LIMIT),
        name="mix",
    )(h, h, conv_w, row(conv_b), row(cln_g), row(cln_b), attn, w_out_bf16, x, row(ln1_g), row(ln1_b),
      wr_hi, wr_lo, router_bias.reshape(1, n_e))


def _moe_kernel(ntrip_ref, order_ref, x1_ref, xb_ref, rank_ref, gate_ref, *refs, tiers, d_e, d_s, per_trip, span):
    wgu_refs, wd_refs = refs[:per_trip], refs[per_trip:2 * per_trip]
    wsgu_ref, wsd_ref, g2_ref, b2_ref, o_ref = refs[2 * per_trip:]
    w = pl.program_id(0)
    step = pl.program_id(1)
    q = w * pl.num_programs(1) + step

    @pl.when(step == 0)
    def _():
        gu = _dot(xb_ref[...], wsgu_ref[...])
        g = gu[:, :d_s]
        o_ref[...] = _dot((g * jax.nn.sigmoid(g) * gu[:, d_s:]).astype(BF16), wsd_ref[...])

    res_rows = x1_ref.shape[0]
    res = pl.ds(pl.multiple_of(step * res_rows, res_rows), res_rows)
    o_ref[res, :] = o_ref[res, :] + ALPHA * x1_ref[...]

    n_spans = xb_ref.shape[0] // span
    experts = [order_ref[q * per_trip + k] for k in range(per_trip)]

    def trip(c, carry, rows):
        onehots, gathered = [], []
        slot_gates = [[None] * n_spans for _ in range(per_trip)]
        for s in range(n_spans):
            lo, hi = s * span, (s + 1) * span
            hits = []
            for k, e in enumerate(experts):
                rank = rank_ref[pl.ds(e, 1), lo:hi]
                hit = (lax.broadcasted_iota(I32, (rows, 1), 0) + c * rows) == rank
                hits.append(hit)
                slot_gates[k][s] = jnp.sum(jnp.where(hit, gate_ref[pl.ds(e, 1), lo:hi], 0.0), axis=1, keepdims=True)
            onehot = jnp.where(jnp.concatenate(hits, axis=0), 1.0, 0.0).astype(BF16)
            onehots.append(onehot)
            gathered.append(_dot(onehot, xb_ref[lo:hi, :]))
        gated = []
        for k in range(per_trip):
            sl = slice(k * rows, (k + 1) * rows)
            xs = jnp.concatenate([gathered[s][sl] for s in range(n_spans)], axis=0).astype(BF16)
            gu = _dot(xs, wgu_refs[k][0])
            g = gu[:, :d_e]
            y = _dot((g * jax.nn.sigmoid(g) * gu[:, d_e:]).astype(BF16), wd_refs[k][0])
            gated.append(y * jnp.concatenate(slot_gates[k], axis=0))
        for s in range(n_spans):
            sl = slice(s * rows, (s + 1) * rows)
            yg = jnp.concatenate([gated[k][sl] for k in range(per_trip)], axis=0).astype(BF16)
            o_ref[s * span:(s + 1) * span, :] += lax.dot_general(
                onehots[s], yg, (((0,), (0,)), ((), ())), preferred_element_type=F32)
        return carry

    for tier, rows in enumerate(tiers):
        lax.fori_loop(0, ntrip_ref[q * len(tiers) + tier], functools.partial(trip, rows=rows), 0)

    @pl.when(step == pl.num_programs(1) - 1)
    def _():
        o_ref[...] = _layer_norm(o_ref[...], g2_ref[...], b2_ref[...])


def _moe(x1, xb, rank, gates, counts, wgu, wd, ws_gate, ws_up, ws_down, ln2_g, ln2_b):
    N, D = x1.shape
    n_e, d_e, _ = wd.shape
    d_s = ws_down.shape[0]
    W = MOE_WINDOW
    n_win = N // W
    tiers = EXPERT_ROW_TIERS
    span = MOE_SPAN
    per_trip = MOE_EXPERTS_PER_TRIP
    n_steps = n_e // per_trip
    assert len(tiers) >= 2 and all(r % SUBLANES == 0 and (per_trip * r) % (2 * SUBLANES) == 0 for r in tiers)
    assert tiers == tuple(sorted(tiers)) and n_e % per_trip == 0 and W % (n_steps * SUBLANES) == 0
    per_span = counts.reshape(N // span, span // MIX_ROWS, n_e)[:, -1]
    load = jnp.max(per_span.reshape(n_win, W // span, n_e), axis=1)
    ids = jnp.arange(n_e, dtype=I32)
    before = ((load[:, None, :] < load[:, :, None])
              | ((load[:, None, :] == load[:, :, None]) & (ids[None, None, :] < ids[None, :, None])))
    place = jnp.sum(before, axis=-1).astype(I32)
    at_place = place[:, :, None] == ids[None, None, :]
    order = jnp.sum(jnp.where(at_place, ids[None, :, None], 0), axis=1).astype(I32)
    sorted_load = jnp.sum(jnp.where(at_place, load[:, :, None], 0), axis=1)
    fullest = jnp.max(sorted_load.reshape(n_win, n_steps, per_trip), axis=-1).reshape(-1)
    lower = jnp.asarray((0,) + tiers[:-1], I32)
    fits = (fullest[:, None] > lower[None, :]) & (fullest[:, None] <= jnp.asarray(tiers, I32)[None, :])
    ntrip = fits.astype(I32).at[:, -1].set(
        jnp.where(fullest > tiers[-2], (fullest + tiers[-1] - 1) // tiers[-1], 0).astype(I32)).reshape(-1)
    wsgu = jnp.concatenate([ws_gate, ws_up], axis=-1).astype(BF16)
    kern = functools.partial(_moe_kernel, tiers=tiers, d_e=d_e, d_s=d_s, per_trip=per_trip, span=span)
    const = lambda shape: pl.BlockSpec(shape, lambda w, s, nt, od: (0,) * len(shape))

    def expert_block(shape, k):
        return pl.BlockSpec((1,) + shape, lambda w, s, nt, od: (od[(w * n_steps + s) * per_trip + k], 0, 0))

    return pl.pallas_call(
        kern,
        grid_spec=pltpu.PrefetchScalarGridSpec(
            num_scalar_prefetch=2,
            grid=(n_win, n_steps),
            in_specs=[
                pl.BlockSpec((W // n_steps, D), lambda w, s, nt, od: (w * n_steps + s, 0)),
                pl.BlockSpec((W, D), lambda w, s, nt, od: (w, 0)),
                pl.BlockSpec((n_e, W), lambda w, s, nt, od: (0, w)),
                pl.BlockSpec((n_e, W), lambda w, s, nt, od: (0, w)),
                *[expert_block((D, 2 * d_e), k) for k in range(per_trip)],
                *[expert_block((d_e, D), k) for k in range(per_trip)],
                const(wsgu.shape), const((d_s, D)), const((1, D)), const((1, D)),
            ],
            out_specs=pl.BlockSpec((W, D), lambda w, s, nt, od: (w, 0)),
        ),
        out_shape=jax.ShapeDtypeStruct((N, D), F32),
        compiler_params=pltpu.CompilerParams(
            dimension_semantics=("arbitrary", "arbitrary"), vmem_limit_bytes=VMEM_LIMIT),
        name="moe",
    )(ntrip, order.reshape(-1), x1, xb, rank, gates, *([wgu] * per_trip), *([wd] * per_trip),
      wsgu, ws_down.astype(BF16), ln2_g.reshape(1, D), ln2_b.reshape(1, D))


def kernel(x, positions, w_in, lambda_q1, lambda_k1, lambda_q2, lambda_k2, subln_g, conv_w, conv_b, conv_ln_g, conv_ln_b, w_out, ln1_g, ln1_b, w_router, router_bias, w_exp_gate, w_exp_up, w_exp_down, w_sh_gate, w_sh_up, w_sh_down, ln2_g, ln2_b):
    B, S, D = x.shape
    d_attn = N_HEADS * HEAD_DIM
    d_conv = conv_w.shape[-1]
    for l in range(DEPTH):
        lambda_init = 0.8 - 0.6 * math.exp(-0.3 * l)
        qT, k, vT, h = _proj(x, positions, w_in[l].astype(BF16), d_attn, d_conv)
        attn, wgu, wd = _attention(qT, k, vT, lambda_q1[l], lambda_k1[l], lambda_q2[l], lambda_k2[l], subln_g[l],
                                   lambda_init, w_exp_gate[l], w_exp_up[l], w_exp_down[l])
        x1, xb, rank, gates, counts = _mix(h, attn, x, conv_w[l], conv_b[l], conv_ln_g[l], conv_ln_b[l],
                                       w_out[l].astype(BF16), ln1_g[l], ln1_b[l], w_router[l], router_bias[l])
        out = _moe(x1, xb, rank, gates, counts, wgu, wd, w_sh_gate[l], w_sh_up[l], w_sh_down[l], ln2_g[l], ln2_b[l])
        x = out.reshape(B, S, D)
    return x
```
